```python
import math
import jax
import jax.numpy as jnp
from jax import lax
import numpy as np

D_MODEL = 4096
BATCH = 2
SEQ = 8192
DEPTH = 2

N_HEADS = 32
HEAD_DIM = D_MODEL // N_HEADS
ATT_WIDTH = N_HEADS * HEAD_DIM
ROT_DIM = HEAD_DIM // 4
ROPE_THETA = 500000.0
DILATED_BRANCHES = ((128, 1), (512, 4), (2048, 16))
BLOCK = 128
N_EXPERTS = 64
EXPERT_DIM = D_MODEL // 16
TOP_K = 8
N_GROUPS = 8
TOPK_GROUPS = 4
ROUTED_SCALE = 2.5
N_A = DEPTH // 2
N_B = DEPTH - N_A
DEEPNORM_ALPHA = (2 * DEPTH) ** 0.25
DEEPNORM_BETA = (8 * DEPTH) ** -0.25
LN_EPS = 1e-5

kernel_name = "yoco_dilated_stickbreaking_moe_deepnorm_adaln"


def _layernorm(x, g, b):
    xf = x.astype(jnp.float32)
    mu = jnp.mean(xf, axis=-1, keepdims=True)
    var = jnp.mean(jnp.square(xf - mu), axis=-1, keepdims=True)
    y = (xf - mu) * lax.rsqrt(var + LN_EPS)
    return (y * g.astype(jnp.float32) + b.astype(jnp.float32)).astype(x.dtype)


def _modulate(x, shift, scale):
    return x * (1 + scale[:, None, :]) + shift[:, None, :]


def _rope_partial(t, cos, sin):
    half = ROT_DIM // 2
    x1, x2, rest = t[..., :half], t[..., half:ROT_DIM], t[..., ROT_DIM:]
    c, s = cos[:, :, None, :], sin[:, :, None, :]
    out = jnp.concatenate([x1 * c - x2 * s, x2 * c + x1 * s, rest.astype(jnp.float32)], axis=-1)
    return out.astype(t.dtype)


def _dilated_branch(q, k, v, window, dilation):
    B, S, H, Dh = q.shape
    L = S // dilation
    nb = -(-L // BLOCK)
    Lp = nb * BLOCK
    sub_win = window // dilation

    def to_blocks(t):
        t = t.reshape(B, L, dilation, H, Dh).transpose(0, 2, 1, 3, 4)
        t = jnp.pad(t, ((0, 0), (0, 0), (0, Lp - L), (0, 0), (0, 0)))
        return t.reshape(B, dilation, nb, BLOCK, H, Dh)

    def with_prev(t):
        prev = jnp.concatenate([jnp.zeros_like(t[:, :, :1]), t[:, :, :-1]], axis=2)
        return jnp.concatenate([prev, t], axis=3)

    qb = to_blocks(q)
    kk = with_prev(to_blocks(k))
    vv = with_prev(to_blocks(v)).astype(jnp.float32)
    s = jnp.einsum('bcnqhd,bcnkhd->bcnhqk', qb, kk).astype(jnp.float32) * (Dh ** -0.5)
    qi = jnp.arange(BLOCK)[:, None]
    ki = jnp.arange(2 * BLOCK)[None, :]
    dist = qi + BLOCK - ki
    band = (dist >= 0) & (dist <= sub_win)
    valid_prev = (jnp.arange(nb)[:, None, None] > 0) | (ki[None] >= BLOCK)
    mask = band[None] & valid_prev
    s = jnp.where(mask[None, None, :, None], s, -jnp.inf)
    m = jnp.max(s, axis=-1, keepdims=True)
    p = jnp.exp(s - m)
    den = jnp.sum(p, axis=-1, keepdims=True)
    o = jnp.einsum('bcnhqk,bcnkhd->bcnqhd', p, vv) / den.transpose(0, 1, 2, 4, 3, 5)
    lse = (m + jnp.log(den))[..., 0].transpose(0, 1, 2, 4, 3)

    def from_blocks(t):
        t = t.reshape((B, dilation, Lp) + t.shape[4:])[:, :, :L]
        t = jnp.swapaxes(t, 1, 2)
        return t.reshape((B, S) + t.shape[3:])

    return from_blocks(o), from_blocks(lse)


def _dilated_attention(h, w_qkv, w_o, cos, sin):
    B, S, _ = h.shape
    qkv = (h @ w_qkv).reshape(B, S, 3, N_HEADS, HEAD_DIM)
    q = _rope_partial(qkv[:, :, 0], cos, sin)
    k = _rope_partial(qkv[:, :, 1], cos, sin)
    v = qkv[:, :, 2]
    outs, lses = [], []
    for window, dilation in DILATED_BRANCHES:
        o_g, lse_g = _dilated_branch(q, k, v, window, dilation)
        outs.append(o_g)
        lses.append(lse_g)
    wts = jax.nn.softmax(jnp.stack(lses, axis=0), axis=0)
    o = jnp.sum(wts[..., None] * jnp.stack(outs, axis=0), axis=0)
    return o.reshape(B, S, ATT_WIDTH).astype(h.dtype) @ w_o


def _stick_breaking_attention(h, w_q, w_o, k, v):
    B, S, _ = h.shape
    q = (h @ w_q).reshape(B, S, N_HEADS, HEAD_DIM)
    nqb = S // BLOCK
    qb = jnp.swapaxes(q.reshape(B, nqb, BLOCK, N_HEADS, HEAD_DIM), 0, 1)
    spos = jnp.arange(S)
    vf = v.astype(jnp.float32)
    scale = HEAD_DIM ** -0.5

    def block(args):
        n, qblk = args
        z = jnp.einsum('bqhd,bshd->bhqs', qblk, k).astype(jnp.float32) * scale
        tpos = n * BLOCK + jnp.arange(BLOCK)
        causal = spos[None, :] < tpos[:, None]
        log1m = jnp.where(causal, -jax.nn.softplus(z), 0.0)
        shifted = jnp.concatenate([log1m[..., 1:], jnp.zeros_like(log1m[..., :1])], axis=-1)
        tail = lax.cumsum(shifted, axis=3, reverse=True)
        a = jnp.where(causal, jnp.exp(jax.nn.log_sigmoid(z) + tail), 0.0)
        return jnp.einsum('bhqs,bshd->bqhd', a, vf)

    o = lax.map(block, (jnp.arange(nqb), qb))
    o = jnp.swapaxes(o, 0, 1).reshape(B, S, ATT_WIDTH).astype(h.dtype)
    return o @ w_o


def _moe(h, router_w, router_b, w_gate, w_up, w_down, sh_gate, sh_up, sh_down):
    B, S, D = h.shape
    T = B * S
    t = h.reshape(T, D)
    scores = jax.nn.sigmoid((t @ router_w).astype(jnp.float32))
    sel = scores + router_b.astype(jnp.float32)
    grp = sel.reshape(T, N_GROUPS, N_EXPERTS // N_GROUPS)
    gscore = jnp.sum(lax.top_k(grp, 2)[0], axis=-1)
    _, gidx = lax.top_k(gscore, TOPK_GROUPS)
    gmask = jnp.sum(jax.nn.one_hot(gidx, N_GROUPS, dtype=jnp.float32), axis=1) > 0
    emask = jnp.repeat(gmask, N_EXPERTS // N_GROUPS, axis=1)
    _, eidx = lax.top_k(jnp.where(emask, sel, -jnp.inf), TOP_K)
    w = jnp.take_along_axis(scores, eidx, axis=-1)
    w = w / jnp.sum(w, axis=-1, keepdims=True) * ROUTED_SCALE
    gates = jnp.sum(jax.nn.one_hot(eidx, N_EXPERTS, dtype=jnp.float32) * w[..., None], axis=1)
    chunk = math.gcd(T, 1024)

    def expert_block(args):
        tc, gc = args
        a = jax.nn.silu(jnp.einsum('td,edf->tef', tc, w_gate)) * jnp.einsum('td,edf->tef', tc, w_up)
        return jnp.einsum('tef,efd->td', a * gc[..., None], w_down)

    routed = lax.map(expert_block, (t.reshape(-1, chunk, D),
                                    gates.astype(t.dtype).reshape(-1, chunk, N_EXPERTS))).reshape(T, D)
    shared = (jax.nn.silu(t @ sh_gate) * (t @ sh_up)) @ sh_down
    return (routed + shared).reshape(B, S, D)


def setup_inputs(seed: int = 0) -> dict:
    key = jax.random.key(seed)
    ks = jax.random.split(key, 24)
    f32 = jnp.float32
    D, F, E = D_MODEL, EXPERT_DIM, N_EXPERTS

    def nrm(k, shape, scale):
        return jax.random.normal(k, shape, f32) * scale

    x = nrm(ks[0], (BATCH, SEQ, D), 1.0)
    c = nrm(ks[1], (BATCH, D), 1.0)
    offset = jax.random.randint(ks[2], (BATCH, 1), 0, 4096, dtype=jnp.int32)
    positions = offset + jnp.arange(SEQ, dtype=jnp.int32)[None, :]
    ada_w = nrm(ks[3], (DEPTH, D, 6 * D), 0.1 * D ** -0.5)
    ada_b = nrm(ks[4], (DEPTH, 6 * D), 0.02)
    ln_g = 1.0 + nrm(ks[5], (DEPTH, 2, D), 0.02)
    ln_b = nrm(ks[6], (DEPTH, 2, D), 0.02)
    qkv_col = jnp.concatenate([jnp.ones((2 * ATT_WIDTH,), f32),
                               jnp.full((ATT_WIDTH,), DEEPNORM_BETA, f32)]) * D ** -0.5
    a_w_qkv = nrm(ks[7], (N_A, D, 3 * ATT_WIDTH), 1.0) * qkv_col
    a_w_o = nrm(ks[8], (N_A, ATT_WIDTH, D), ATT_WIDTH ** -0.5 * DEEPNORM_BETA)
    kv_ada_w = nrm(ks[9], (D, 2 * D), 0.1 * D ** -0.5)
    kv_ada_b = nrm(ks[10], (2 * D,), 0.02)
    kv_col = jnp.concatenate([jnp.ones((ATT_WIDTH,), f32),
                              jnp.full((ATT_WIDTH,), DEEPNORM_BETA, f32)]) * D ** -0.5
    b_w_kv = nrm(ks[11], (D, 2 * ATT_WIDTH), 1.0) * kv_col
    b_w_q = nrm(ks[12], (N_B, D, ATT_WIDTH), D ** -0.5)
    b_w_o = nrm(ks[13], (N_B, ATT_WIDTH, D), ATT_WIDTH ** -0.5 * DEEPNORM_BETA)
    router_w = nrm(ks[14], (DEPTH, D, E), D ** -0.5)
    router_b = nrm(ks[15], (DEPTH, E), 0.01)
    w_gate = nrm(ks[16], (DEPTH, E, D, F), D ** -0.5)
    w_up = nrm(ks[17], (DEPTH, E, D, F), D ** -0.5)
    w_down = nrm(ks[18], (DEPTH, E, F, D), F ** -0.5 * DEEPNORM_BETA)
    sh_gate = nrm(ks[19], (DEPTH, D, F), D ** -0.5)
    sh_up = nrm(ks[20], (DEPTH, D, F), D ** -0.5)
    sh_down = nrm(ks[21], (DEPTH, F, D), F ** -0.5 * DEEPNORM_BETA)
    return {"x": x, "c": c, "positions": positions, "ada_w": ada_w, "ada_b": ada_b,
            "ln_g": ln_g, "ln_b": ln_b, "a_w_qkv": a_w_qkv, "a_w_o": a_w_o,
            "kv_ada_w": kv_ada_w, "kv_ada_b": kv_ada_b, "b_w_kv": b_w_kv,
            "b_w_q": b_w_q, "b_w_o": b_w_o, "router_w": router_w, "router_b": router_b,
            "w_gate": w_gate, "w_up": w_up, "w_down": w_down,
            "sh_gate": sh_gate, "sh_up": sh_up, "sh_down": sh_down}


def reference(x, c, positions, ada_w, ada_b, ln_g, ln_b, a_w_qkv, a_w_o, kv_ada_w, kv_ada_b,
              b_w_kv, b_w_q, b_w_o, router_w, router_b, w_gate, w_up, w_down,
              sh_gate, sh_up, sh_down):
    B, S, D = x.shape
    inv_freq = ROPE_THETA ** (-jnp.arange(0, ROT_DIM, 2, dtype=jnp.float32) / ROT_DIM)
    ang = positions.astype(jnp.float32)[..., None] * inv_freq
    cos, sin = jnp.cos(ang), jnp.sin(ang)
    cond = jax.nn.silu(c)
    k_sh, v_sh = None, None
    for l in range(DEPTH):
        mods = cond @ ada_w[l] + ada_b[l]
        sh_a, sc_a, g_a, sh_m, sc_m, g_m = jnp.split(mods, 6, axis=-1)
        h = _modulate(x, sh_a, sc_a)
        if l < N_A:
            y = _dilated_attention(h, a_w_qkv[l], a_w_o[l], cos, sin)
        else:
            y = _stick_breaking_attention(h, b_w_q[l - N_A], b_w_o[l - N_A], k_sh, v_sh)
        x = _layernorm(DEEPNORM_ALPHA * x + (1 + g_a[:, None, :]) * y, ln_g[l, 0], ln_b[l, 0])
        h = _modulate(x, sh_m, sc_m)
        y = _moe(h, router_w[l], router_b[l], w_gate[l], w_up[l], w_down[l],
                 sh_gate[l], sh_up[l], sh_down[l])
        x = _layernorm(DEEPNORM_ALPHA * x + (1 + g_m[:, None, :]) * y, ln_g[l, 1], ln_b[l, 1])
        if l == N_A - 1:
            kv_shift, kv_scale = jnp.split(cond @ kv_ada_w + kv_ada_b, 2, axis=-1)
            kv = (_modulate(x, kv_shift, kv_scale) @ b_w_kv).reshape(B, S, 2, N_HEADS, HEAD_DIM)
            k_sh, v_sh = kv[:, :, 0], kv[:, :, 1]
    return x
```

```python
import functools
import math

import jax
import jax.numpy as jnp
from jax import lax
from jax.experimental import pallas as pl
from jax.experimental.pallas import tpu as pltpu

F32 = jnp.float32
BF16 = jnp.bfloat16
U32 = jnp.uint32
I32 = jnp.int32

HEAD_DIM = 128
ROT_DIM = HEAD_DIM // 4
ROPE_THETA = 500000.0
DILATED_BRANCHES = ((128, 1), (512, 4), (2048, 16))
BLOCK = 128
TOP_K = 8
N_GROUPS = 8
TOPK_GROUPS = 4
ROUTED_SCALE = 2.5
DEPTH = 2
DEEPNORM_ALPHA = (2 * DEPTH) ** 0.25
LN_EPS = 1e-5

LANES = 128
VMEM_LIMIT_BYTES = 56 * 1024 * 1024

MM_TM = 1024
MM_TN = 512
LN_TM = 256
ROUTER_TM = 512
EXPERT_TM = 256
DISPATCH_TD = 128
COMBINE_TC = 128
SUPER = 2048
SB_TQ = 256
SB_SKIP = 100.0

NEG_INF = float("-inf")
HI_MASK = 0xFFFF0000


def _tile(pref, dim):
    t = min(pref, dim)
    assert dim % t == 0, (pref, dim)
    return t


def _cparams(*sem):
    return pltpu.CompilerParams(dimension_semantics=sem, vmem_limit_bytes=VMEM_LIMIT_BYTES)


def _dot(a, b):
    return jnp.dot(a, b, preferred_element_type=F32)


def _dot_nt(a, b):
    return lax.dot_general(a, b, (((1,), (1,)), ((), ())), preferred_element_type=F32)


def _unpack_pair(w):
    lo = lax.bitcast_convert_type(w << 16, F32).astype(BF16)
    hi = lax.bitcast_convert_type(w & jnp.uint32(HI_MASK), F32).astype(BF16)
    return lo, hi


def _unpack_pair_f32(w):
    lo = lax.bitcast_convert_type(w << 16, F32)
    hi = lax.bitcast_convert_type(w & jnp.uint32(HI_MASK), F32)
    return lo, hi


def _pack_pair(lo, hi):
    lo_b = lax.bitcast_convert_type(lo.astype(BF16).astype(F32), U32)
    hi_b = lax.bitcast_convert_type(hi.astype(BF16).astype(F32), U32)
    return (hi_b & jnp.uint32(HI_MASK)) | (lo_b >> 16)


def _load_slab_words(ref, n_tok):
    ch = ref.shape[0] // n_tok
    return jnp.concatenate([ref[pl.ds(c, n_tok, stride=ch), :] for c in range(ch)], axis=1)


def _store_slab_words(ref, words):
    n_tok = words.shape[0]
    ch = ref.shape[0] // n_tok
    for c in range(ch):
        ref[pl.ds(c, n_tok, stride=ch), :] = words[:, c * LANES:(c + 1) * LANES]


def _adaln_kernel(c_ref, w_ref, b_ref, o_ref, *, nk):
    k = pl.program_id(2)

    @pl.when(k == 0)
    def _():
        o_ref[...] = jnp.zeros_like(o_ref)

    cond = jax.nn.silu(c_ref[...])
    o_ref[0] += jnp.dot(cond, w_ref[0], preferred_element_type=F32,
                        precision=lax.Precision.HIGHEST)

    @pl.when(k == nk - 1)
    def _():
        o_ref[0] += b_ref[0]


def _adaln(c8, w, b):
    L, D, N = w.shape
    tk = _tile(512, D)
    tn = _tile(2048, N)
    nk = D // tk
    return pl.pallas_call(
        functools.partial(_adaln_kernel, nk=nk),
        grid=(L, N // tn, nk),
        in_specs=[pl.BlockSpec((8, tk), lambda l, j, k: (0, k)),
                  pl.BlockSpec((1, tk, tn), lambda l, j, k: (l, k, j)),
                  pl.BlockSpec((1, 1, tn), lambda l, j, k: (l, 0, j))],
        out_specs=pl.BlockSpec((1, 8, tn), lambda l, j, k: (l, 0, j)),
        out_shape=jax.ShapeDtypeStruct((L, 8, N), F32),
        compiler_params=_cparams("parallel", "parallel", "arbitrary"),
        name="adaln",
    )(c8, w, b.reshape(L, 1, N))


def _rope_kernel(pos_ref, freq_ref, sign_ref, cos_ref, sin_ref):
    ang = pos_ref[...] * freq_ref[...]
    cos_ref[...] = jnp.cos(ang)
    sin_ref[...] = jnp.sin(ang) * sign_ref[...]


def _rope_tables(positions):
    T = positions.size
    half = ROT_DIM // 2
    inv_freq = ROPE_THETA ** (-jnp.arange(0, ROT_DIM, 2, dtype=F32) / ROT_DIM)
    freq = jnp.concatenate([inv_freq, inv_freq, jnp.zeros((HEAD_DIM - ROT_DIM,), F32)])[None, :]
    sign = jnp.concatenate([-jnp.ones((half,), F32), jnp.ones((HEAD_DIM - half,), F32)])[None, :]
    pos = jnp.broadcast_to(positions.astype(F32).reshape(T, 1), (T, HEAD_DIM))
    tm = _tile(2048, T)
    row = pl.BlockSpec((tm, HEAD_DIM), lambda i: (i, 0))
    vec = pl.BlockSpec((1, HEAD_DIM), lambda i: (0, 0))
    return pl.pallas_call(
        _rope_kernel,
        grid=(T // tm,),
        in_specs=[row, vec, vec],
        out_specs=[row, row],
        out_shape=[jax.ShapeDtypeStruct((T, HEAD_DIM), F32)] * 2,
        compiler_params=_cparams("parallel"),
        name="rope_tables",
    )(pos, freq, sign)


def _emit_mod(xn, rows_ref, idx, fmt, out_ref):
    shift = rows_ref[0, 1 + 2 * idx:2 + 2 * idx, :]
    scale = rows_ref[0, 2 + 2 * idx:3 + 2 * idx, :]
    h = xn * (1.0 + scale) + shift
    if fmt == "bf16":
        out_ref[...] = h.astype(BF16)
    else:
        half = h.shape[1] // 2
        _store_slab_words(out_ref, _pack_pair(h[:, :half], h[:, half:]))


def _modulate_kernel(x_ref, rows_ref, o_ref):
    _emit_mod(x_ref[...], rows_ref, 0, "bf16", o_ref)


def _ln_kernel(x_ref, y_ref, rows_ref, g_ref, b_ref, xo_ref, *h_refs, fmts):
    gate = rows_ref[0, 0:1, :]
    z = DEEPNORM_ALPHA * x_ref[...] + (1.0 + gate) * y_ref[...]
    mu = jnp.mean(z, axis=-1, keepdims=True)
    zc = z - mu
    var = jnp.mean(zc * zc, axis=-1, keepdims=True)
    xn = zc * lax.rsqrt(var + LN_EPS) * g_ref[...] + b_ref[...]
    xo_ref[...] = xn
    for idx, (fmt, h_ref) in enumerate(zip(fmts, h_refs)):
        _emit_mod(xn, rows_ref, idx, fmt, h_ref)


def _mod_rows(gate, pairs):
    B, D = pairs[0][0].shape if pairs else gate.shape
    rows = [gate if gate is not None else jnp.zeros((B, D), F32)]
    for shift, scale in pairs:
        rows += [shift, scale]
    rows += [jnp.zeros((B, D), F32)] * (8 - len(rows))
    return jnp.stack(rows, axis=1)


def _out_struct(T, D, tm, fmt):
    if fmt == "bf16":
        return jax.ShapeDtypeStruct((T, D), BF16), (tm, D)
    ch = D // 2 // LANES
    return jax.ShapeDtypeStruct((T * ch, LANES), U32), (tm * ch, LANES)


def _modulate(x, rows, S):
    T, D = x.shape
    tm = _tile(LN_TM, S)
    nb = S // tm
    return pl.pallas_call(
        _modulate_kernel,
        grid=(T // tm,),
        in_specs=[pl.BlockSpec((tm, D), lambda i: (i, 0)),
                  pl.BlockSpec((1, 8, D), lambda i: (i // nb, 0, 0))],
        out_specs=pl.BlockSpec((tm, D), lambda i: (i, 0)),
        out_shape=jax.ShapeDtypeStruct((T, D), BF16),
        compiler_params=_cparams("parallel"),
        name="modulate",
    )(x, rows)


def _ln(x, y, rows, g, b, fmts, S):
    T, D = x.shape
    tm = _tile(LN_TM, S)
    nb = S // tm
    row = pl.BlockSpec((tm, D), lambda i: (i, 0))
    vec = pl.BlockSpec((1, D), lambda i: (0, 0))
    out_shape = [jax.ShapeDtypeStruct((T, D), F32)]
    out_specs = [row]
    for fmt in fmts:
        st, blk = _out_struct(T, D, tm, fmt)
        out_shape.append(st)
        out_specs.append(pl.BlockSpec(blk, lambda i: (i, 0)))
    return pl.pallas_call(
        functools.partial(_ln_kernel, fmts=tuple(fmts)),
        grid=(T // tm,),
        in_specs=[row, row, pl.BlockSpec((1, 8, D), lambda i: (i // nb, 0, 0)), vec, vec],
        out_specs=out_specs,
        out_shape=out_shape,
        compiler_params=_cparams("parallel"),
        name="deepnorm_ln",
    )(x, y, rows, g.reshape(1, D), b.reshape(1, D))


def _swap_halves(x):
    half = ROT_DIM // 2
    lane = lax.broadcasted_iota(I32, x.shape, 1)
    return jnp.where(lane < half, pltpu.roll(x, HEAD_DIM - half, 1), pltpu.roll(x, half, 1))


def _mm_kernel(*refs, a_heads, out_heads, n_rope_tiles):
    if n_rope_tiles:
        a_ref, w_ref, cos_ref, sin_ref, o_ref = refs
    else:
        a_ref, w_ref, o_ref = refs
    if a_heads:
        a = jnp.concatenate([a_ref[0, h] for h in range(a_heads)], axis=-1)
    else:
        a = a_ref[...]
    acc = _dot(a, w_ref[...])
    if not out_heads:
        o_ref[...] = acc
        return

    def write(rot):
        for hh in range(out_heads):
            xh = acc[:, hh * HEAD_DIM:(hh + 1) * HEAD_DIM]
            if rot:
                xh = xh * cos_ref[...] + _swap_halves(xh) * sin_ref[...]
            o_ref[0, 0, hh] = xh.astype(BF16)

    if not n_rope_tiles:
        write(False)
    else:
        j = pl.program_id(1)

        @pl.when(j < n_rope_tiles)
        def _():
            write(True)

        @pl.when(j >= n_rope_tiles)
        def _():
            write(False)


def _mm(a, w, *, B, S, head_major_out=False, rope=None, n_rope_cols=0):
    K, N = w.shape
    T = B * S
    H = K // HEAD_DIM
    tm = _tile(MM_TM, S)
    tn = _tile(MM_TN, N)
    nsb = S // tm
    a_heads = H if a.ndim == 4 else 0
    if a_heads:
        a_spec = pl.BlockSpec((1, H, tm, HEAD_DIM), lambda i, j: (i // nsb, 0, i % nsb, 0))
    else:
        a_spec = pl.BlockSpec((tm, K), lambda i, j: (i, 0))
    in_specs = [a_spec, pl.BlockSpec((K, tn), lambda i, j: (0, j))]
    args = [a, w]
    out_heads = tn // HEAD_DIM if head_major_out else 0
    n_rope_tiles = 0
    if head_major_out:
        D = H * HEAD_DIM
        G = N // D
        tiles_per_group = D // tn
        if rope is not None:
            assert n_rope_cols % tn == 0
            n_rope_tiles = n_rope_cols // tn
            tab = pl.BlockSpec((tm, HEAD_DIM), lambda i, j: (i, 0))
            in_specs += [tab, tab]
            args += list(rope)
        out_spec = pl.BlockSpec(
            (1, 1, out_heads, tm, HEAD_DIM),
            lambda i, j: (j // tiles_per_group, i // nsb, j % tiles_per_group, i % nsb, 0))
        out_shape = jax.ShapeDtypeStruct((G, B, H, S, HEAD_DIM), BF16)
    else:
        out_spec = pl.BlockSpec((tm, tn), lambda i, j: (i, j))
        out_shape = jax.ShapeDtypeStruct((T, N), F32)
    return pl.pallas_call(
        functools.partial(_mm_kernel, a_heads=a_heads, out_heads=out_heads,
                          n_rope_tiles=n_rope_tiles),
        grid=(T // tm, N // tn),
        in_specs=in_specs,
        out_specs=out_spec,
        out_shape=out_shape,
        compiler_params=_cparams("parallel", "parallel"),
        name="proj",
    )(*args)


def _dil_attn_kernel(q1, q4, q16, k1c, k4c, k16c, k1p, k4p, k16p, v1c, v4c, v16c, v1p, v4p, v16p,
                     o_ref, *scratch, scale):
    acc_s, m_s, l_s = scratch[0:4], scratch[4:8], scratch[8:12]
    u_acc, u_m, u_l = scratch[12:15]
    qi = lax.broadcasted_iota(I32, (BLOCK, BLOCK), 0)
    kj = lax.broadcasted_iota(I32, (BLOCK, BLOCK), 1)
    prev_pen = jnp.where(pl.program_id(2) == 0, NEG_INF, 0.0).astype(F32)

    def unit(q, ka, kb, va, vb, first_local):
        sa = _dot_nt(q, ka) * scale
        sb = _dot_nt(q, kb) * scale
        sa = jnp.where(kj >= qi, sa, NEG_INF)
        if first_local:
            sa = sa + prev_pen
        sb = jnp.where(kj <= qi, sb, NEG_INF)
        m = jnp.maximum(jnp.max(sa, axis=1, keepdims=True), jnp.max(sb, axis=1, keepdims=True))
        pa = jnp.exp(sa - m)
        pb = jnp.exp(sb - m)
        l = jnp.sum(pa, axis=1, keepdims=True) + jnp.sum(pb, axis=1, keepdims=True)
        acc = _dot(pa.astype(BF16), va) + _dot(pb.astype(BF16), vb)
        return acc, jnp.broadcast_to(m, acc.shape), jnp.broadcast_to(l, acc.shape)

    def fold(slab, rows, acc_u, m_u, l_u):
        m_o = m_s[slab][rows, :]
        m_n = jnp.maximum(m_o, m_u)
        e_o = jnp.exp(m_o - m_n)
        e_u = jnp.exp(m_u - m_n)
        acc_s[slab][rows, :] = acc_s[slab][rows, :] * e_o + acc_u * e_u
        l_s[slab][rows, :] = l_s[slab][rows, :] * e_o + l_u * e_u
        m_s[slab][rows, :] = m_n

    for r in range(4):
        lanes = slice(r * HEAD_DIM, (r + 1) * HEAD_DIM)

        def put4(n, res, r=r):
            rows = pl.ds(pl.multiple_of(n * BLOCK, BLOCK), BLOCK)
            acc_s[r][rows, :], m_s[r][rows, :], l_s[r][rows, :] = res

        put4(0, unit(q4[0, 0, 0, 0:BLOCK, lanes], k4p[0, 0, 0, :, lanes], k4c[0, 0, 0, 0:BLOCK, lanes],
                     v4p[0, 0, 0, :, lanes], v4c[0, 0, 0, 0:BLOCK, lanes], True))

        def body4(n, carry, lanes=lanes, put4=put4):
            cur = pl.ds(pl.multiple_of(n * BLOCK, BLOCK), BLOCK)
            prv = pl.ds(pl.multiple_of((n - 1) * BLOCK, BLOCK), BLOCK)
            put4(n, unit(q4[0, 0, 0, cur, lanes], k4c[0, 0, 0, prv, lanes], k4c[0, 0, 0, cur, lanes],
                         v4c[0, 0, 0, prv, lanes], v4c[0, 0, 0, cur, lanes], False))
            return carry

        lax.fori_loop(1, SUPER // 4 // BLOCK, body4, 0)

    def fold1(n, res):
        u_acc[...], u_m[...], u_l[...] = res
        rows = pl.ds(pl.multiple_of(n * (BLOCK // 4), BLOCK // 4), BLOCK // 4)
        for r in range(4):
            pick = pl.ds(r, BLOCK // 4, stride=4)
            fold(r, rows, u_acc[pick, :], u_m[pick, :], u_l[pick, :])

    fold1(0, unit(q1[0, 0, 0, 0:BLOCK, :], k1p[0, 0, 0], k1c[0, 0, 0, 0:BLOCK, :],
                  v1p[0, 0, 0], v1c[0, 0, 0, 0:BLOCK, :], True))

    def body1(n, carry):
        cur = pl.ds(pl.multiple_of(n * BLOCK, BLOCK), BLOCK)
        prv = pl.ds(pl.multiple_of((n - 1) * BLOCK, BLOCK), BLOCK)
        fold1(n, unit(q1[0, 0, 0, cur, :], k1c[0, 0, 0, prv, :], k1c[0, 0, 0, cur, :],
                      v1c[0, 0, 0, prv, :], v1c[0, 0, 0, cur, :], False))
        return carry

    lax.fori_loop(1, SUPER // BLOCK, body1, 0)

    for r in range(16):
        lanes = slice(r * HEAD_DIM, (r + 1) * HEAD_DIM)
        res = unit(q16[0, 0, 0, :, lanes], k16p[0, 0, 0, :, lanes], k16c[0, 0, 0, :, lanes],
                   v16p[0, 0, 0, :, lanes], v16c[0, 0, 0, :, lanes], True)
        fold(r % 4, pl.ds(r // 4, BLOCK, stride=4), *res)

    for r in range(4):
        o_ref[0, 0, :, r * HEAD_DIM:(r + 1) * HEAD_DIM] = (acc_s[r][...] / l_s[r][...]).astype(BF16)


def _dilated_attention(qkv):
    _, B, H, S, Dh = qkv.shape
    assert Dh == HEAD_DIM and S % SUPER == 0
    assert tuple(d for _, d in DILATED_BRANCHES) == (1, 4, 16)
    assert all(w // d == BLOCK for w, d in DILATED_BRANCHES)
    views = {d: qkv.reshape(3, B, H, S // d, d * Dh) for d in (1, 4, 16)}

    def cur(which, d):
        return pl.BlockSpec((1, 1, 1, SUPER // d, d * Dh), lambda b, h, m: (which, b, h, m, 0))

    def prev(which, d):
        per = SUPER // d // BLOCK
        return pl.BlockSpec((1, 1, 1, BLOCK, d * Dh),
                            lambda b, h, m: (which, b, h, jnp.maximum(m * per - 1, 0), 0))

    in_specs, args = [], []
    for make, which in ((cur, 0), (cur, 1), (prev, 1), (cur, 2), (prev, 2)):
        for d in (1, 4, 16):
            in_specs.append(make(which, d))
            args.append(views[d])
    state = [pltpu.VMEM((SUPER // 4, Dh), F32)] * 12
    stage = [pltpu.VMEM((BLOCK, Dh), F32)] * 3
    out = pl.pallas_call(
        functools.partial(_dil_attn_kernel, scale=Dh ** -0.5),
        grid=(B, H, S // SUPER),
        in_specs=in_specs,
        out_specs=pl.BlockSpec((1, 1, SUPER // 4, 4 * Dh), lambda b, h, m: (b, h, m, 0)),
        out_shape=jax.ShapeDtypeStruct((B, H, S // 4, 4 * Dh), BF16),
        scratch_shapes=state + stage,
        compiler_params=_cparams("parallel", "parallel", "parallel"),
        name="dilated_attention",
    )(*args)
    return out.reshape(B, H, S, Dh)


def _sb_kernel(q_ref, k_ref, v_ref, o_ref, tri_ref, acc_ref, carry_ref, *, scale, tq, nq):
    row = lax.broadcasted_iota(I32, (tq, tq), 0)
    col = lax.broadcasted_iota(I32, (tq, tq), 1)
    tri_ref[...] = jnp.where(row > col, 1.0, 0.0).astype(BF16)

    def suffix_sum(x):
        hi = x.astype(BF16)
        lo = (x - hi.astype(F32)).astype(BF16)
        return _dot(hi, tri_ref[...]) + _dot(lo, tri_ref[...])

    def logs(q, k):
        z = _dot_nt(q, k) * scale
        t = jnp.log(1.0 + jnp.exp(-jnp.abs(z)))
        return jnp.minimum(z, 0.0) - t, -(jnp.maximum(z, 0.0) + t)

    def qblock(n, carry):
        rows = pl.ds(pl.multiple_of(n * tq, tq), tq)
        q = q_ref[0, 0, rows, :]
        causal = col < row
        log_b, log_1mb = logs(q, k_ref[0, 0, 0, rows, :])
        log_1mb = jnp.where(causal, log_1mb, 0.0)
        a = jnp.where(causal, jnp.exp(log_b + suffix_sum(log_1mb)), 0.0)
        acc_ref[...] = _dot(a.astype(BF16), v_ref[0, 0, 0, rows, :])
        c0 = jnp.broadcast_to(jnp.sum(log_1mb, axis=1, keepdims=True), (tq, tq))
        carry_ref[...] = c0

        def cond(st):
            j, top = st
            return jnp.logical_and(j >= 0, top > -SB_SKIP)

        def body(st):
            j, _ = st
            krows = pl.ds(pl.multiple_of(j * tq, tq), tq)
            log_b, log_1mb = logs(q, k_ref[0, 0, 0, krows, :])
            c = carry_ref[...]
            a = jnp.exp(log_b + suffix_sum(log_1mb) + c)
            acc_ref[...] += _dot(a.astype(BF16), v_ref[0, 0, 0, krows, :])
            c = c + jnp.broadcast_to(jnp.sum(log_1mb, axis=1, keepdims=True), (tq, tq))
            carry_ref[...] = c
            return j - 1, jnp.max(c)

        lax.while_loop(cond, body, (n - 1, jnp.max(c0)))
        o_ref[0, 0, rows, :] = acc_ref[...].astype(BF16)
        return carry

    lax.fori_loop(0, nq, qblock, 0)


def _stick_breaking_attention(q, kv):
    _, B, H, S, Dh = q.shape
    tq = _tile(SB_TQ, S)
    seq = lambda which: pl.BlockSpec((1, 1, 1, S, Dh), lambda b, h: (which, b, h, 0, 0))
    return pl.pallas_call(
        functools.partial(_sb_kernel, scale=Dh ** -0.5, tq=tq, nq=S // tq),
        grid=(B, H),
        in_specs=[pl.BlockSpec((1, 1, S, Dh), lambda b, h: (b, h, 0, 0)), seq(0), seq(1)],
        out_specs=pl.BlockSpec((1, 1, S, Dh), lambda b, h: (b, h, 0, 0)),
        out_shape=jax.ShapeDtypeStruct((B, H, S, Dh), BF16),
        scratch_shapes=[pltpu.VMEM((tq, tq), BF16), pltpu.VMEM((tq, Dh), F32),
                        pltpu.VMEM((tq, tq), F32)],
        compiler_params=_cparams("parallel", "parallel"),
        name="stick_breaking_attention",
    )(q.reshape(B, H, S, Dh), kv, kv)


def _router_kernel(h_ref, rw_ref, rb_ref, e_ref, r_ref, w_ref, cnt_ref, tri_ref, carry_ref,
                   *, n_exp, tm, nsteps):
    i = pl.program_id(0)
    gsz = n_exp // N_GROUPS

    @pl.when(i == 0)
    def _():
        row = lax.broadcasted_iota(I32, (tm, tm), 0)
        col = lax.broadcasted_iota(I32, (tm, tm), 1)
        tri_ref[...] = jnp.where(row < col, 1.0, 0.0).astype(BF16)
        carry_ref[...] = jnp.zeros_like(carry_ref)

    lo, hi = _unpack_pair(_load_slab_words(h_ref, tm))
    half = lo.shape[1]
    logits = _dot_nt(rw_ref[:, :half], lo) + _dot_nt(rw_ref[:, half:], hi)
    scores = jax.nn.sigmoid(logits)
    sel = scores + rb_ref[...]

    sub = lax.broadcasted_iota(I32, (gsz, tm), 0)
    groups, gscore = [], []
    for g in range(N_GROUPS):
        sg = sel[g * gsz:(g + 1) * gsz, :]
        m1 = jnp.max(sg, axis=0, keepdims=True)
        i1 = jnp.min(jnp.where(sg == m1, sub, gsz), axis=0, keepdims=True)
        m2 = jnp.max(jnp.where(sub == i1, NEG_INF, sg), axis=0, keepdims=True)
        groups.append(sg)
        gscore.append(m1 + m2)
    masked = []
    for g in range(N_GROUPS):
        beat = jnp.zeros((1, tm), F32)
        for o in range(N_GROUPS):
            if o == g:
                continue
            beat += jnp.where(gscore[o] > gscore[g], 1.0, 0.0)
            if o < g:
                beat += jnp.where(gscore[o] == gscore[g], 1.0, 0.0)
        masked.append(jnp.where(beat < TOPK_GROUPS, groups[g], NEG_INF))
    x = jnp.concatenate(masked, axis=0)

    eid = lax.broadcasted_iota(I32, (n_exp, tm), 0)
    chosen = jnp.zeros((n_exp, tm), F32)
    picks, vals = [], []
    for _ in range(TOP_K):
        mx = jnp.max(x, axis=0, keepdims=True)
        idx = jnp.min(jnp.where(x == mx, eid, n_exp), axis=0, keepdims=True)
        hit = eid == idx
        picks.append(idx)
        vals.append(jnp.sum(jnp.where(hit, scores, 0.0), axis=0, keepdims=True))
        chosen += jnp.where(hit, 1.0, 0.0)
        x = jnp.where(hit, NEG_INF, x)
    total = vals[0]
    for v in vals[1:]:
        total = total + v

    rank_all = _dot(chosen.astype(BF16), tri_ref[...]) + carry_ref[...]
    for k in range(TOP_K):
        e_ref[k:k + 1, :] = picks[k]
        w_ref[k:k + 1, :] = vals[k] / total * ROUTED_SCALE
        rk = jnp.sum(jnp.where(eid == picks[k], rank_all, 0.0), axis=0, keepdims=True)
        r_ref[k:k + 1, :] = rk.astype(I32)
    carry_ref[...] += jnp.broadcast_to(jnp.sum(chosen, axis=1, keepdims=True), (n_exp, tm))

    @pl.when(i == nsteps - 1)
    def _():
        cnt_ref[...] = carry_ref[:, 0:LANES].astype(I32)


def _router(h32, rw_t, rb):
    E, D = rw_t.shape
    ch = D // 2 // LANES
    T = h32.shape[0] // ch
    tm = _tile(ROUTER_TM, T)
    nsteps = T // tm
    tok = pl.BlockSpec((TOP_K, tm), lambda i: (0, i))
    return pl.pallas_call(
        functools.partial(_router_kernel, n_exp=E, tm=tm, nsteps=nsteps),
        grid=(nsteps,),
        in_specs=[pl.BlockSpec((tm * ch, LANES), lambda i: (i, 0)),
                  pl.BlockSpec((E, D), lambda i: (0, 0)),
                  pl.BlockSpec((E, 1), lambda i: (0, 0))],
        out_specs=[tok, tok, tok, pl.BlockSpec((E, LANES), lambda i: (0, 0))],
        out_shape=[jax.ShapeDtypeStruct((TOP_K, T), I32), jax.ShapeDtypeStruct((TOP_K, T), I32),
                   jax.ShapeDtypeStruct((TOP_K, T), F32), jax.ShapeDtypeStruct((E, LANES), I32)],
        scratch_shapes=[pltpu.VMEM((tm, tm), BF16), pltpu.VMEM((E, tm), F32)],
        compiler_params=_cparams("arbitrary"),
        name="moe_router",
    )(h32, rw_t, rb)


def _dispatch_kernel(last_ref, slots_ref, h_hbm, xs_hbm, zbuf, zsem, sem, *, n_exp, td, ch):
    i = pl.program_id(0)
    tile_rows = zbuf.shape[0]

    def zero_copy(e):
        dst = pl.ds(pl.multiple_of(last_ref[e], tile_rows), tile_rows)
        return pltpu.make_async_copy(zbuf, xs_hbm.at[dst], zsem)

    @pl.when(i == 0)
    def _():
        zbuf[...] = jnp.zeros_like(zbuf)

        def start(e, c):
            @pl.when(last_ref[e] >= 0)
            def _():
                zero_copy(e).start()
            return c

        def wait(e, c):
            @pl.when(last_ref[e] >= 0)
            def _():
                zero_copy(e).wait()
            return c

        lax.fori_loop(0, n_exp, start, 0)
        lax.fori_loop(0, n_exp, wait, 0)

    def row_copy(t, k):
        src = pl.ds(pl.multiple_of((i * td + t) * ch, ch), ch)
        dst = pl.ds(pl.multiple_of(slots_ref[0, k, t], ch), ch)
        return pltpu.make_async_copy(h_hbm.at[src], xs_hbm.at[dst], sem)

    def start_rows(t, c):
        for k in range(TOP_K):
            row_copy(t, k).start()
        return c

    def wait_rows(t, c):
        for k in range(TOP_K):
            row_copy(t, k).wait()
        return c

    lax.fori_loop(0, td, start_rows, 0)
    lax.fori_loop(0, td, wait_rows, 0)


def _dispatch(h32, slots3, last_rows, n_slots, tm_e, ch):
    nd, _, td = slots3.shape
    E = last_rows.shape[0]
    return pl.pallas_call(
        functools.partial(_dispatch_kernel, n_exp=E, td=td, ch=ch),
        grid_spec=pltpu.PrefetchScalarGridSpec(
            num_scalar_prefetch=1,
            grid=(nd,),
            in_specs=[pl.BlockSpec((1, TOP_K, td), lambda i, last: (i, 0, 0),
                                   memory_space=pltpu.SMEM),
                      pl.BlockSpec(memory_space=pl.ANY)],
            out_specs=pl.BlockSpec(memory_space=pl.ANY),
            scratch_shapes=[pltpu.VMEM((tm_e * ch, LANES), U32), pltpu.SemaphoreType.DMA,
                            pltpu.SemaphoreType.DMA]),
        out_shape=jax.ShapeDtypeStruct((n_slots * ch, LANES), U32),
        compiler_params=_cparams("arbitrary"),
        name="moe_dispatch",
    )(last_rows, slots3, h32)


def _mlp_rows(x_ref, wgu_ref, wd_ref, n_tok):
    f = wd_ref.shape[1]
    lo, hi = _unpack_pair(_load_slab_words(x_ref, n_tok))
    half = lo.shape[1]
    gu = _dot(lo, wgu_ref[0, :half, :]) + _dot(hi, wgu_ref[0, half:, :])
    a = (jax.nn.silu(gu[:, :f]) * gu[:, f:]).astype(BF16)
    return _dot(a, wd_ref[0])


def _expert_kernel(te_ref, nt_ref, x_ref, wgu_ref, wd_ref, y_ref, *, tm_e):
    @pl.when(pl.program_id(0) < nt_ref[0])
    def _():
        y = _mlp_rows(x_ref, wgu_ref, wd_ref, tm_e)
        half = y.shape[1] // 2
        _store_slab_words(y_ref, _pack_pair(y[:, :half], y[:, half:]))


def _shared_kernel(x_ref, wgu_ref, wd_ref, y_ref, *, tm):
    y_ref[...] = _mlp_rows(x_ref, wgu_ref, wd_ref, tm)


def _experts(xs, wgu, wd, tile_expert, n_tiles, tm_e, ch):
    E, D, F2 = wgu.shape
    F = F2 // 2
    n_slots = xs.shape[0] // ch
    rows = pl.BlockSpec((tm_e * ch, LANES), lambda i, te, nt: (jnp.minimum(i, nt[0] - 1), 0))
    return pl.pallas_call(
        functools.partial(_expert_kernel, tm_e=tm_e),
        grid_spec=pltpu.PrefetchScalarGridSpec(
            num_scalar_prefetch=2,
            grid=(n_slots // tm_e,),
            in_specs=[rows,
                      pl.BlockSpec((1, D, F2), lambda i, te, nt: (te[i], 0, 0)),
                      pl.BlockSpec((1, F, D), lambda i, te, nt: (te[i], 0, 0))],
            out_specs=rows),
        out_shape=jax.ShapeDtypeStruct(xs.shape, U32),
        compiler_params=_cparams("arbitrary"),
        name="moe_experts",
    )(tile_expert, n_tiles, xs, wgu, wd)


def _shared_expert(h32, wgu, wd, ch):
    _, D, F2 = wgu.shape
    T = h32.shape[0] // ch
    tm = _tile(512, T)
    return pl.pallas_call(
        functools.partial(_shared_kernel, tm=tm),
        grid=(T // tm,),
        in_specs=[pl.BlockSpec((tm * ch, LANES), lambda i: (i, 0)),
                  pl.BlockSpec((1, D, F2), lambda i: (0, 0, 0)),
                  pl.BlockSpec((1, F2 // 2, D), lambda i: (0, 0, 0))],
        out_specs=pl.BlockSpec((tm, D), lambda i: (i, 0)),
        out_shape=jax.ShapeDtypeStruct((T, D), F32),
        compiler_params=_cparams("parallel"),
        name="moe_shared_expert",
    )(h32, wgu, wd)


def _combine_kernel(slots_ref, w_ref, ysh_ref, y_hbm, o_ref, buf, sem, *, tc, ch):
    def row_copy(t, k):
        src = pl.ds(pl.multiple_of(slots_ref[0, k, t], ch), ch)
        dst = pl.ds(pl.multiple_of(t * ch, ch), ch)
        return pltpu.make_async_copy(y_hbm.at[src], buf.at[k, dst], sem)

    def start_rows(t, c):
        for k in range(TOP_K):
            row_copy(t, k).start()
        return c

    def wait_rows(t, c):
        for k in range(TOP_K):
            row_copy(t, k).wait()
        return c

    lax.fori_loop(0, tc, start_rows, 0)
    lax.fori_loop(0, tc, wait_rows, 0)

    half = ch * LANES
    gates = [w_ref[:, k:k + 1] for k in range(TOP_K)]
    for c in range(ch):
        lanes_lo = slice(c * LANES, (c + 1) * LANES)
        lanes_hi = slice(half + c * LANES, half + (c + 1) * LANES)
        acc_lo = ysh_ref[:, lanes_lo]
        acc_hi = ysh_ref[:, lanes_hi]
        for k in range(TOP_K):
            lo, hi = _unpack_pair_f32(buf.at[k][pl.ds(c, tc, stride=ch), :])
            acc_lo = acc_lo + gates[k] * lo
            acc_hi = acc_hi + gates[k] * hi
        o_ref[:, lanes_lo] = acc_lo
        o_ref[:, lanes_hi] = acc_hi


def _combine(slots3, w_tok, ysh, ys, ch):
    T, D = ysh.shape
    nc, _, tc = slots3.shape
    return pl.pallas_call(
        functools.partial(_combine_kernel, tc=tc, ch=ch),
        grid=(nc,),
        in_specs=[pl.BlockSpec((1, TOP_K, tc), lambda i: (i, 0, 0), memory_space=pltpu.SMEM),
                  pl.BlockSpec((tc, TOP_K), lambda i: (i, 0)),
                  pl.BlockSpec((tc, D), lambda i: (i, 0)),
                  pl.BlockSpec(memory_space=pl.ANY)],
        out_specs=pl.BlockSpec((tc, D), lambda i: (i, 0)),
        out_shape=jax.ShapeDtypeStruct((T, D), F32),
        scratch_shapes=[pltpu.VMEM((TOP_K, tc * ch, LANES), U32), pltpu.SemaphoreType.DMA],
        compiler_params=_cparams("arbitrary"),
        name="moe_combine",
    )(slots3, w_tok, ysh, ys)


def _moe(h32, rw_t, rb, wgu, wd, sh_gu, sh_d):
    E, D = rw_t.shape
    ch = D // 2 // LANES
    assert ch % 8 == 0, "a token slab must be whole (8,128) tiles"
    T = h32.shape[0] // ch
    tm_e = _tile(EXPERT_TM, T)
    eidx, rank, wts, counts = _router(h32, rw_t, rb)

    counts = counts[:, 0]
    tiles = (counts + tm_e - 1) // tm_e
    tile_end = jnp.cumsum(tiles)
    tile_start = tile_end - tiles
    n_tiles = tile_end[-1]
    slot_rows = (jnp.take(tile_start * tm_e, eidx) + rank) * ch
    max_tiles = T * TOP_K // tm_e + E
    tile_ids = jnp.minimum(jnp.arange(max_tiles, dtype=I32), n_tiles - 1)
    tile_expert = jnp.searchsorted(tile_end, tile_ids, side="right").astype(I32)
    last_rows = jnp.where(tiles > 0, (tile_end - 1) * (tm_e * ch), -1).astype(I32)

    def by_step(step):
        return slot_rows.reshape(TOP_K, T // step, step).transpose(1, 0, 2)

    xs = _dispatch(h32, by_step(_tile(DISPATCH_TD, T)), last_rows, max_tiles * tm_e, tm_e, ch)
    ys = _experts(xs, wgu, wd, tile_expert, n_tiles.reshape(1).astype(I32), tm_e, ch)
    ysh = _shared_expert(h32, sh_gu, sh_d, ch)
    return _combine(by_step(_tile(COMBINE_TC, T)), wts.T, ysh, ys, ch)


def kernel(x, c, positions, ada_w, ada_b, ln_g, ln_b, a_w_qkv, a_w_o, kv_ada_w, kv_ada_b, b_w_kv, b_w_q, b_w_o, router_w, router_b, w_gate, w_up, w_down, sh_gate, sh_up, sh_down):
    B, S, D = x.shape
    T = B * S
    assert D % HEAD_DIM == 0 and DEPTH == ada_w.shape[0] == 2
    xf = x.reshape(T, D)

    c8 = jnp.zeros((8, D), F32).at[:B].set(c)
    mods = _adaln(c8, ada_w, ada_b)[:, :B].reshape(DEPTH, B, 6, D)
    kv_mods = _adaln(c8, kv_ada_w[None], kv_ada_b[None])[0, :B].reshape(B, 2, D)
    sh_a, sc_a, g_a, sh_m, sc_m, g_m = (mods[:, :, r] for r in range(6))
    rope = _rope_tables(positions)

    def moe_weights(l):
        wgu = jnp.concatenate([w_gate[l], w_up[l]], axis=-1).astype(BF16)
        sh_gu = jnp.concatenate([sh_gate[l], sh_up[l]], axis=-1).astype(BF16)[None]
        return (router_w[l].T.astype(BF16), router_b[l].reshape(-1, 1), wgu,
                w_down[l].astype(BF16), sh_gu, sh_down[l].astype(BF16)[None])

    h = _modulate(xf, _mod_rows(None, [(sh_a[0], sc_a[0])]), S)
    qkv = _mm(h, a_w_qkv[0].astype(BF16), B=B, S=S, head_major_out=True, rope=rope,
              n_rope_cols=2 * D)
    o = _dilated_attention(qkv)
    y = _mm(o, a_w_o[0].astype(BF16), B=B, S=S)
    xf, h32 = _ln(xf, y, _mod_rows(g_a[0], [(sh_m[0], sc_m[0])]), ln_g[0, 0], ln_b[0, 0],
                  ["u32"], S)
    y = _moe(h32, *moe_weights(0))
    xf, h_kv, h_q = _ln(xf, y, _mod_rows(g_m[0], [(kv_mods[:, 0], kv_mods[:, 1]),
                                                  (sh_a[1], sc_a[1])]),
                        ln_g[0, 1], ln_b[0, 1], ["bf16", "bf16"], S)

    kv = _mm(h_kv, b_w_kv.astype(BF16), B=B, S=S, head_major_out=True)
    q = _mm(h_q, b_w_q[0].astype(BF16), B=B, S=S, head_major_out=True)
    o = _stick_breaking_attention(q, kv)
    y = _mm(o, b_w_o[0].astype(BF16), B=B, S=S)
    xf, h32 = _ln(xf, y, _mod_rows(g_a[1], [(sh_m[1], sc_m[1])]), ln_g[1, 0], ln_b[1, 0],
                  ["u32"], S)
    y = _moe(h32, *moe_weights(1))
    (xf,) = _ln(xf, y, _mod_rows(g_m[1], []), ln_g[1, 1], ln_b[1, 1], [], S)
    return xf.reshape(B, S, D)
```

```python
import functools
import math

import jax
import jax.numpy as jnp
from jax import lax
from jax.experimental import pallas as pl
from jax.experimental.pallas import tpu as pltpu

F32 = jnp.float32
BF16 = jnp.bfloat16
U32 = jnp.uint32
I32 = jnp.int32

HEAD_DIM = 128
ROT_DIM = HEAD_DIM // 4
ROPE_THETA = 500000.0
DILATED_BRANCHES = ((128, 1), (512, 4), (2048, 16))
BLOCK = 128
TOP_K = 8
N_GROUPS = 8
TOPK_GROUPS = 4
ROUTED_SCALE = 2.5
DEPTH = 2
DEEPNORM_ALPHA = (2 * DEPTH) ** 0.25
LN_EPS = 1e-5

LANES = 128
VMEM_LIMIT_BYTES = 56 * 1024 * 1024

MM_TM = 1024
MM_TN = 512
LN_TM = 256
ROUTER_TM = 512
EXPERT_TM = 256
DISPATCH_TD = 128
COMBINE_TC = 128
SUPER = 2048
DIL_GROUP = 8
SB_TQ = 256
SB_GROUP = 4
SB_SKIP = 100.0

NEG_INF = float("-inf")
HI_MASK = 0xFFFF0000


def _tile(pref, dim):
    t = min(pref, dim)
    assert dim % t == 0, (pref, dim)
    return t


def _cparams(*sem):
    return pltpu.CompilerParams(dimension_semantics=sem, vmem_limit_bytes=VMEM_LIMIT_BYTES)


def _dot(a, b):
    return jnp.dot(a, b, preferred_element_type=F32)


def _dot_nt(a, b):
    return lax.dot_general(a, b, (((1,), (1,)), ((), ())), preferred_element_type=F32)


def _unpack_pair(w):
    lo = lax.bitcast_convert_type(w << 16, F32).astype(BF16)
    hi = lax.bitcast_convert_type(w & jnp.uint32(HI_MASK), F32).astype(BF16)
    return lo, hi


def _unpack_pair_f32(w):
    lo = lax.bitcast_convert_type(w << 16, F32)
    hi = lax.bitcast_convert_type(w & jnp.uint32(HI_MASK), F32)
    return lo, hi


def _pack_pair(lo, hi):
    lo_b = lax.bitcast_convert_type(lo.astype(BF16).astype(F32), U32)
    hi_b = lax.bitcast_convert_type(hi.astype(BF16).astype(F32), U32)
    return (hi_b & jnp.uint32(HI_MASK)) | (lo_b >> 16)


def _load_slab_words(ref, n_tok):
    ch = ref.shape[0] // n_tok
    return jnp.concatenate([ref[pl.ds(c, n_tok, stride=ch), :] for c in range(ch)], axis=1)


def _store_slab_words(ref, words):
    n_tok = words.shape[0]
    ch = ref.shape[0] // n_tok
    for c in range(ch):
        ref[pl.ds(c, n_tok, stride=ch), :] = words[:, c * LANES:(c + 1) * LANES]


def _adaln_kernel(c_ref, w_ref, b_ref, o_ref, *, nk):
    k = pl.program_id(2)

    @pl.when(k == 0)
    def _():
        o_ref[...] = jnp.zeros_like(o_ref)

    cond = jax.nn.silu(c_ref[...])
    o_ref[0] += jnp.dot(cond, w_ref[0], preferred_element_type=F32,
                        precision=lax.Precision.HIGHEST)

    @pl.when(k == nk - 1)
    def _():
        o_ref[0] += b_ref[0]


def _adaln(c8, w, b):
    L, D, N = w.shape
    tk = _tile(512, D)
    tn = _tile(2048, N)
    nk = D // tk
    return pl.pallas_call(
        functools.partial(_adaln_kernel, nk=nk),
        grid=(L, N // tn, nk),
        in_specs=[pl.BlockSpec((8, tk), lambda l, j, k: (0, k)),
                  pl.BlockSpec((1, tk, tn), lambda l, j, k: (l, k, j)),
                  pl.BlockSpec((1, 1, tn), lambda l, j, k: (l, 0, j))],
        out_specs=pl.BlockSpec((1, 8, tn), lambda l, j, k: (l, 0, j)),
        out_shape=jax.ShapeDtypeStruct((L, 8, N), F32),
        compiler_params=_cparams("parallel", "parallel", "arbitrary"),
        name="adaln",
    )(c8, w, b.reshape(L, 1, N))


def _rope_kernel(pos_ref, freq_ref, sign_ref, cos_ref, sin_ref):
    ang = pos_ref[...] * freq_ref[...]
    cos_ref[...] = jnp.cos(ang)
    sin_ref[...] = jnp.sin(ang) * sign_ref[...]


def _rope_tables(positions):
    T = positions.size
    half = ROT_DIM // 2
    inv_freq = ROPE_THETA ** (-jnp.arange(0, ROT_DIM, 2, dtype=F32) / ROT_DIM)
    freq = jnp.concatenate([inv_freq, inv_freq, jnp.zeros((HEAD_DIM - ROT_DIM,), F32)])[None, :]
    sign = jnp.concatenate([-jnp.ones((half,), F32), jnp.ones((HEAD_DIM - half,), F32)])[None, :]
    pos = jnp.broadcast_to(positions.astype(F32).reshape(T, 1), (T, HEAD_DIM))
    tm = _tile(2048, T)
    row = pl.BlockSpec((tm, HEAD_DIM), lambda i: (i, 0))
    vec = pl.BlockSpec((1, HEAD_DIM), lambda i: (0, 0))
    return pl.pallas_call(
        _rope_kernel,
        grid=(T // tm,),
        in_specs=[row, vec, vec],
        out_specs=[row, row],
        out_shape=[jax.ShapeDtypeStruct((T, HEAD_DIM), F32)] * 2,
        compiler_params=_cparams("parallel"),
        name="rope_tables",
    )(pos, freq, sign)


def _emit_mod(xn, rows_ref, idx, fmt, out_ref):
    shift = rows_ref[0, 1 + 2 * idx:2 + 2 * idx, :]
    scale = rows_ref[0, 2 + 2 * idx:3 + 2 * idx, :]
    h = xn * (1.0 + scale) + shift
    if fmt == "bf16":
        out_ref[...] = h.astype(BF16)
    else:
        half = h.shape[1] // 2
        _store_slab_words(out_ref, _pack_pair(h[:, :half], h[:, half:]))


def _modulate_kernel(x_ref, rows_ref, o_ref):
    _emit_mod(x_ref[...], rows_ref, 0, "bf16", o_ref)


def _ln_kernel(x_ref, y_ref, rows_ref, g_ref, b_ref, xo_ref, *h_refs, fmts):
    gate = rows_ref[0, 0:1, :]
    z = DEEPNORM_ALPHA * x_ref[...] + (1.0 + gate) * y_ref[...]
    mu = jnp.mean(z, axis=-1, keepdims=True)
    zc = z - mu
    var = jnp.mean(zc * zc, axis=-1, keepdims=True)
    xn = zc * lax.rsqrt(var + LN_EPS) * g_ref[...] + b_ref[...]
    xo_ref[...] = xn
    for idx, (fmt, h_ref) in enumerate(zip(fmts, h_refs)):
        _emit_mod(xn, rows_ref, idx, fmt, h_ref)


def _mod_rows(gate, pairs):
    B, D = pairs[0][0].shape if pairs else gate.shape
    rows = [gate if gate is not None else jnp.zeros((B, D), F32)]
    for shift, scale in pairs:
        rows += [shift, scale]
    rows += [jnp.zeros((B, D), F32)] * (8 - len(rows))
    return jnp.stack(rows, axis=1)


def _out_struct(T, D, tm, fmt):
    if fmt == "bf16":
        return jax.ShapeDtypeStruct((T, D), BF16), (tm, D)
    ch = D // 2 // LANES
    return jax.ShapeDtypeStruct((T * ch, LANES), U32), (tm * ch, LANES)


def _modulate(x, rows, S):
    T, D = x.shape
    tm = _tile(LN_TM, S)
    nb = S // tm
    return pl.pallas_call(
        _modulate_kernel,
        grid=(T // tm,),
        in_specs=[pl.BlockSpec((tm, D), lambda i: (i, 0)),
                  pl.BlockSpec((1, 8, D), lambda i: (i // nb, 0, 0))],
        out_specs=pl.BlockSpec((tm, D), lambda i: (i, 0)),
        out_shape=jax.ShapeDtypeStruct((T, D), BF16),
        compiler_params=_cparams("parallel"),
        name="modulate",
    )(x, rows)


def _ln(x, y, rows, g, b, fmts, S):
    T, D = x.shape
    tm = _tile(LN_TM, S)
    nb = S // tm
    row = pl.BlockSpec((tm, D), lambda i: (i, 0))
    vec = pl.BlockSpec((1, D), lambda i: (0, 0))
    out_shape = [jax.ShapeDtypeStruct((T, D), F32)]
    out_specs = [row]
    for fmt in fmts:
        st, blk = _out_struct(T, D, tm, fmt)
        out_shape.append(st)
        out_specs.append(pl.BlockSpec(blk, lambda i: (i, 0)))
    return pl.pallas_call(
        functools.partial(_ln_kernel, fmts=tuple(fmts)),
        grid=(T // tm,),
        in_specs=[row, row, pl.BlockSpec((1, 8, D), lambda i: (i // nb, 0, 0)), vec, vec],
        out_specs=out_specs,
        out_shape=out_shape,
        compiler_params=_cparams("parallel"),
        name="deepnorm_ln",
    )(x, y, rows, g.reshape(1, D), b.reshape(1, D))


def _swap_halves(x):
    half = ROT_DIM // 2
    lane = lax.broadcasted_iota(I32, x.shape, 1)
    return jnp.where(lane < half, pltpu.roll(x, HEAD_DIM - half, 1), pltpu.roll(x, half, 1))


def _mm_kernel(*refs, a_heads, out_heads, n_rope_tiles):
    if n_rope_tiles:
        a_ref, w_ref, cos_ref, sin_ref, o_ref = refs
    else:
        a_ref, w_ref, o_ref = refs
    if a_heads:
        a = jnp.concatenate([a_ref[0, h] for h in range(a_heads)], axis=-1)
    else:
        a = a_ref[...]
    acc = _dot(a, w_ref[...])
    if not out_heads:
        o_ref[...] = acc
        return

    def write(rot):
        for hh in range(out_heads):
            xh = acc[:, hh * HEAD_DIM:(hh + 1) * HEAD_DIM]
            if rot:
                xh = xh * cos_ref[...] + _swap_halves(xh) * sin_ref[...]
            o_ref[0, 0, hh] = xh.astype(BF16)

    if not n_rope_tiles:
        write(False)
    else:
        j = pl.program_id(1)

        @pl.when(j < n_rope_tiles)
        def _():
            write(True)

        @pl.when(j >= n_rope_tiles)
        def _():
            write(False)


def _mm(a, w, *, B, S, head_major_out=False, rope=None, n_rope_cols=0):
    K, N = w.shape
    T = B * S
    H = K // HEAD_DIM
    tm = _tile(MM_TM, S)
    tn = _tile(MM_TN, N)
    nsb = S // tm
    a_heads = H if a.ndim == 4 else 0
    if a_heads:
        a_spec = pl.BlockSpec((1, H, tm, HEAD_DIM), lambda i, j: (i // nsb, 0, i % nsb, 0))
    else:
        a_spec = pl.BlockSpec((tm, K), lambda i, j: (i, 0))
    in_specs = [a_spec, pl.BlockSpec((K, tn), lambda i, j: (0, j))]
    args = [a, w]
    out_heads = tn // HEAD_DIM if head_major_out else 0
    n_rope_tiles = 0
    if head_major_out:
        D = H * HEAD_DIM
        G = N // D
        tiles_per_group = D // tn
        if rope is not None:
            assert n_rope_cols % tn == 0
            n_rope_tiles = n_rope_cols // tn
            tab = pl.BlockSpec((tm, HEAD_DIM), lambda i, j: (i, 0))
            in_specs += [tab, tab]
            args += list(rope)
        out_spec = pl.BlockSpec(
            (1, 1, out_heads, tm, HEAD_DIM),
            lambda i, j: (j // tiles_per_group, i // nsb, j % tiles_per_group, i % nsb, 0))
        out_shape = jax.ShapeDtypeStruct((G, B, H, S, HEAD_DIM), BF16)
    else:
        out_spec = pl.BlockSpec((tm, tn), lambda i, j: (i, j))
        out_shape = jax.ShapeDtypeStruct((T, N), F32)
    return pl.pallas_call(
        functools.partial(_mm_kernel, a_heads=a_heads, out_heads=out_heads,
                          n_rope_tiles=n_rope_tiles),
        grid=(T // tm, N // tn),
        in_specs=in_specs,
        out_specs=out_spec,
        out_shape=out_shape,
        compiler_params=_cparams("parallel", "parallel"),
        name="proj",
    )(*args)


def _dil_attn_kernel(q1, q4, q16, k1c, k4c, k16c, k1p, k4p, k16p, v1c, v4c, v16c, v1p, v4p, v16p,
                     o_ref, *scratch, scale):
    acc_s, m_s, l_s = scratch[0:4], scratch[4:8], scratch[8:12]
    bias_s, s_s, p_s, mu_s, st_acc, st_m, st_l = scratch[12:19]

    qi = lax.broadcasted_iota(I32, (BLOCK, 2 * BLOCK), 0)
    kc = lax.broadcasted_iota(I32, (BLOCK, 2 * BLOCK), 1)
    band = jnp.where(kc >= qi, jnp.where(kc <= qi + BLOCK, 0.0, NEG_INF), NEG_INF).astype(F32)
    bias_s[0] = band
    no_prev = jnp.where(pl.program_id(2) == 0, NEG_INF, 0.0).astype(F32)
    bias_s[1] = jnp.where(kc < BLOCK, band + no_prev, band)
    ones = jnp.ones((2 * BLOCK, HEAD_DIM), BF16)

    def run(units):
        for g, (q, kp, kc_, _, _, first, _) in enumerate(units):
            keys = jnp.concatenate([kp(), kc_()], axis=0)
            s_s[g] = _dot_nt(q(), keys) * scale + bias_s[1 if first else 0]
        for g in range(len(units)):
            s = s_s[g]
            m = jnp.max(jnp.maximum(s[:, :BLOCK], s[:, BLOCK:]), axis=1, keepdims=True)
            p_s[g] = jnp.exp(s - m).astype(BF16)
            mu_s[g] = jnp.broadcast_to(m, (BLOCK, HEAD_DIM))
        for g, (_, _, _, vp, vc, _, sink) in enumerate(units):
            p = p_s[g]
            vals = jnp.concatenate([vp(), vc()], axis=0)
            sink(g, _dot(p, vals), mu_s[g], _dot(p, ones))

    def fold(slab, rows, acc_u, m_u, l_u):
        m_o = m_s[slab][rows, :]
        m_n = jnp.maximum(m_o, m_u)
        e_o = jnp.exp(m_o - m_n)
        e_u = jnp.exp(m_u - m_n)
        acc_s[slab][rows, :] = acc_s[slab][rows, :] * e_o + acc_u * e_u
        l_s[slab][rows, :] = l_s[slab][rows, :] * e_o + l_u * e_u
        m_s[slab][rows, :] = m_n

    def tile(ref, rows, lanes):
        return lambda: ref[0, 0, 0, rows, lanes]

    def banded(q, kc_, kp, vc, vp, n, lanes, sink):
        cur = slice(n * BLOCK, (n + 1) * BLOCK)
        if n == 0:
            k_prev, v_prev = tile(kp, slice(None), lanes), tile(vp, slice(None), lanes)
        else:
            prv = slice((n - 1) * BLOCK, n * BLOCK)
            k_prev, v_prev = tile(kc_, prv, lanes), tile(vc, prv, lanes)
        return (tile(q, cur, lanes), k_prev, tile(kc_, cur, lanes), v_prev, tile(vc, cur, lanes),
                n == 0, sink)

    units = []
    for n in range(SUPER // 4 // BLOCK):
        for r in range(4):
            def init(g, acc, m, l, r=r, n=n):
                rows = slice(n * BLOCK, (n + 1) * BLOCK)
                acc_s[r][rows, :], m_s[r][rows, :], l_s[r][rows, :] = acc, m, l

            units.append(banded(q4, k4c, k4p, v4c, v4p, n,
                                slice(r * HEAD_DIM, (r + 1) * HEAD_DIM), init))

    for n in range(SUPER // BLOCK):
        def fold1(g, acc, m, l, n=n):
            st_acc[g], st_m[g], st_l[g] = acc, m, l
            rows = slice(n * (BLOCK // 4), (n + 1) * (BLOCK // 4))
            for r in range(4):
                pick = pl.ds(r, BLOCK // 4, stride=4)
                fold(r, rows, st_acc[g, pick, :], st_m[g, pick, :], st_l[g, pick, :])

        units.append(banded(q1, k1c, k1p, v1c, v1p, n, slice(None), fold1))

    for r in range(16):
        def fold16(g, acc, m, l, r=r):
            fold(r % 4, pl.ds(r // 4, BLOCK, stride=4), acc, m, l)

        units.append(banded(q16, k16c, k16p, v16c, v16p, 0,
                            slice(r * HEAD_DIM, (r + 1) * HEAD_DIM), fold16))

    for start in range(0, len(units), DIL_GROUP):
        run(units[start:start + DIL_GROUP])

    for r in range(4):
        o_ref[0, 0, :, r * HEAD_DIM:(r + 1) * HEAD_DIM] = (acc_s[r][...] / l_s[r][...]).astype(BF16)


def _dilated_attention(qkv):
    _, B, H, S, Dh = qkv.shape
    assert Dh == HEAD_DIM and S % SUPER == 0
    assert tuple(d for _, d in DILATED_BRANCHES) == (1, 4, 16)
    assert all(w // d == BLOCK for w, d in DILATED_BRANCHES)
    views = {d: qkv.reshape(3, B, H, S // d, d * Dh) for d in (1, 4, 16)}

    def cur(which, d):
        return pl.BlockSpec((1, 1, 1, SUPER // d, d * Dh), lambda b, h, m: (which, b, h, m, 0))

    def prev(which, d):
        per = SUPER // d // BLOCK
        return pl.BlockSpec((1, 1, 1, BLOCK, d * Dh),
                            lambda b, h, m: (which, b, h, jnp.maximum(m * per - 1, 0), 0))

    in_specs, args = [], []
    for make, which in ((cur, 0), (cur, 1), (prev, 1), (cur, 2), (prev, 2)):
        for d in (1, 4, 16):
            in_specs.append(make(which, d))
            args.append(views[d])
    state = [pltpu.VMEM((SUPER // 4, Dh), F32)] * 12
    stage = [pltpu.VMEM((2, BLOCK, 2 * BLOCK), F32),
             pltpu.VMEM((DIL_GROUP, BLOCK, 2 * BLOCK), F32),
             pltpu.VMEM((DIL_GROUP, BLOCK, 2 * BLOCK), BF16),
             pltpu.VMEM((DIL_GROUP, BLOCK, Dh), F32)] + [
             pltpu.VMEM((DIL_GROUP, BLOCK, Dh), F32)] * 3
    out = pl.pallas_call(
        functools.partial(_dil_attn_kernel, scale=Dh ** -0.5),
        grid=(B, H, S // SUPER),
        in_specs=in_specs,
        out_specs=pl.BlockSpec((1, 1, SUPER // 4, 4 * Dh), lambda b, h, m: (b, h, m, 0)),
        out_shape=jax.ShapeDtypeStruct((B, H, S // 4, 4 * Dh), BF16),
        scratch_shapes=state + stage,
        compiler_params=_cparams("parallel", "parallel", "parallel"),
        name="dilated_attention",
    )(*args)
    return out.reshape(B, H, S, Dh)


def _sb_kernel(q_ref, k_ref, v_ref, o_ref, tri_ref, acc_ref, carry_ref, *, scale, tq, nq):
    row = lax.broadcasted_iota(I32, (tq, tq), 0)
    col = lax.broadcasted_iota(I32, (tq, tq), 1)
    tri_ref[...] = jnp.where(row > col, 1.0, 0.0).astype(BF16)

    def suffix_sum(x):
        hi = x.astype(BF16)
        lo = (x - hi.astype(F32)).astype(BF16)
        return _dot(hi, tri_ref[...]) + _dot(lo, tri_ref[...])

    def logs(q, k):
        z = _dot_nt(q, k) * scale
        t = jnp.log(1.0 + jnp.exp(-jnp.abs(z)))
        return jnp.minimum(z, 0.0) - t, -(jnp.maximum(z, 0.0) + t)

    def qgroup(it, carry):
        n0 = it * SB_GROUP

        def block_rows(n):
            return pl.ds(pl.multiple_of(n * tq, tq), tq)

        causal = col < row
        tops = []
        for g in range(SB_GROUP):
            rows = block_rows(n0 + g)
            log_b, log_1mb = logs(q_ref[0, 0, rows, :], k_ref[0, 0, 0, rows, :])
            log_1mb = jnp.where(causal, log_1mb, 0.0)
            a = jnp.where(causal, jnp.exp(log_b + suffix_sum(log_1mb)), 0.0)
            acc_ref[g] = _dot(a.astype(BF16), v_ref[0, 0, 0, rows, :])
            c0 = jnp.broadcast_to(jnp.sum(log_1mb, axis=1, keepdims=True), (tq, tq))
            carry_ref[g] = c0
            tops.append(jnp.where(n0 + g >= 1, jnp.max(c0), NEG_INF))

        def cond(st):
            _, top = st
            return top > -SB_SKIP

        def body(st):
            i, _ = st
            tops = []
            for g in range(SB_GROUP):
                j = n0 + g - i
                live = j >= 0
                krows = block_rows(jnp.maximum(j, 0))
                log_b, log_1mb = logs(q_ref[0, 0, block_rows(n0 + g), :], k_ref[0, 0, 0, krows, :])
                c = carry_ref[g]
                a = jnp.where(live, jnp.exp(log_b + suffix_sum(log_1mb) + c), 0.0)
                acc_ref[g] += _dot(a.astype(BF16), v_ref[0, 0, 0, krows, :])
                c = c + jnp.broadcast_to(jnp.sum(log_1mb, axis=1, keepdims=True), (tq, tq))
                carry_ref[g] = c
                tops.append(jnp.where(j >= 1, jnp.max(c), NEG_INF))
            return i + 1, functools.reduce(jnp.maximum, tops)

        lax.while_loop(cond, body, (1, functools.reduce(jnp.maximum, tops)))
        for g in range(SB_GROUP):
            o_ref[0, 0, block_rows(n0 + g), :] = acc_ref[g].astype(BF16)
        return carry

    lax.fori_loop(0, nq // SB_GROUP, qgroup, 0)


def _stick_breaking_attention(q, kv):
    _, B, H, S, Dh = q.shape
    tq = _tile(SB_TQ, S)
    seq = lambda which: pl.BlockSpec((1, 1, 1, S, Dh), lambda b, h: (which, b, h, 0, 0))
    return pl.pallas_call(
        functools.partial(_sb_kernel, scale=Dh ** -0.5, tq=tq, nq=S // tq),
        grid=(B, H),
        in_specs=[pl.BlockSpec((1, 1, S, Dh), lambda b, h: (b, h, 0, 0)), seq(0), seq(1)],
        out_specs=pl.BlockSpec((1, 1, S, Dh), lambda b, h: (b, h, 0, 0)),
        out_shape=jax.ShapeDtypeStruct((B, H, S, Dh), BF16),
        scratch_shapes=[pltpu.VMEM((tq, tq), BF16), pltpu.VMEM((SB_GROUP, tq, Dh), F32),
                        pltpu.VMEM((SB_GROUP, tq, tq), F32)],
        compiler_params=_cparams("parallel", "parallel"),
        name="stick_breaking_attention",
    )(q.reshape(B, H, S, Dh), kv, kv)


def _router_kernel(h_ref, rw_ref, rb_ref, e_ref, r_ref, w_ref, cnt_ref, tri_ref, carry_ref,
                   *, n_exp, tm, nsteps):
    i = pl.program_id(0)
    gsz = n_exp // N_GROUPS

    @pl.when(i == 0)
    def _():
        row = lax.broadcasted_iota(I32, (tm, tm), 0)
        col = lax.broadcasted_iota(I32, (tm, tm), 1)
        tri_ref[...] = jnp.where(row < col, 1.0, 0.0).astype(BF16)
        carry_ref[...] = jnp.zeros_like(carry_ref)

    lo, hi = _unpack_pair(_load_slab_words(h_ref, tm))
    half = lo.shape[1]
    logits = _dot_nt(rw_ref[:, :half], lo) + _dot_nt(rw_ref[:, half:], hi)
    scores = jax.nn.sigmoid(logits)
    sel = scores + rb_ref[...]

    sub = lax.broadcasted_iota(I32, (gsz, tm), 0)
    groups, gscore = [], []
    for g in range(N_GROUPS):
        sg = sel[g * gsz:(g + 1) * gsz, :]
        m1 = jnp.max(sg, axis=0, keepdims=True)
        i1 = jnp.min(jnp.where(sg == m1, sub, gsz), axis=0, keepdims=True)
        m2 = jnp.max(jnp.where(sub == i1, NEG_INF, sg), axis=0, keepdims=True)
        groups.append(sg)
        gscore.append(m1 + m2)
    masked = []
    for g in range(N_GROUPS):
        beat = jnp.zeros((1, tm), F32)
        for o in range(N_GROUPS):
            if o == g:
                continue
            beat += jnp.where(gscore[o] > gscore[g], 1.0, 0.0)
            if o < g:
                beat += jnp.where(gscore[o] == gscore[g], 1.0, 0.0)
        masked.append(jnp.where(beat < TOPK_GROUPS, groups[g], NEG_INF))
    x = jnp.concatenate(masked, axis=0)

    eid = lax.broadcasted_iota(I32, (n_exp, tm), 0)
    chosen = jnp.zeros((n_exp, tm), F32)
    picks, vals = [], []
    for _ in range(TOP_K):
        mx = jnp.max(x, axis=0, keepdims=True)
        idx = jnp.min(jnp.where(x == mx, eid, n_exp), axis=0, keepdims=True)
        hit = eid == idx
        picks.append(idx)
        vals.append(jnp.sum(jnp.where(hit, scores, 0.0), axis=0, keepdims=True))
        chosen += jnp.where(hit, 1.0, 0.0)
        x = jnp.where(hit, NEG_INF, x)
    total = vals[0]
    for v in vals[1:]:
        total = total + v

    rank_all = _dot(chosen.astype(BF16), tri_ref[...]) + carry_ref[...]
    for k in range(TOP_K):
        e_ref[k:k + 1, :] = picks[k]
        w_ref[k:k + 1, :] = vals[k] / total * ROUTED_SCALE
        rk = jnp.sum(jnp.where(eid == picks[k], rank_all, 0.0), axis=0, keepdims=True)
        r_ref[k:k + 1, :] = rk.astype(I32)
    carry_ref[...] += jnp.broadcast_to(jnp.sum(chosen, axis=1, keepdims=True), (n_exp, tm))

    @pl.when(i == nsteps - 1)
    def _():
        cnt_ref[...] = carry_ref[:, 0:LANES].astype(I32)


def _router(h32, rw_t, rb):
    E, D = rw_t.shape
    ch = D // 2 // LANES
    T = h32.shape[0] // ch
    tm = _tile(ROUTER_TM, T)
    nsteps = T // tm
    tok = pl.BlockSpec((TOP_K, tm), lambda i: (0, i))
    return pl.pallas_call(
        functools.partial(_router_kernel, n_exp=E, tm=tm, nsteps=nsteps),
        grid=(nsteps,),
        in_specs=[pl.BlockSpec((tm * ch, LANES), lambda i: (i, 0)),
                  pl.BlockSpec((E, D), lambda i: (0, 0)),
                  pl.BlockSpec((E, 1), lambda i: (0, 0))],
        out_specs=[tok, tok, tok, pl.BlockSpec((E, LANES), lambda i: (0, 0))],
        out_shape=[jax.ShapeDtypeStruct((TOP_K, T), I32), jax.ShapeDtypeStruct((TOP_K, T), I32),
                   jax.ShapeDtypeStruct((TOP_K, T), F32), jax.ShapeDtypeStruct((E, LANES), I32)],
        scratch_shapes=[pltpu.VMEM((tm, tm), BF16), pltpu.VMEM((E, tm), F32)],
        compiler_params=_cparams("arbitrary"),
        name="moe_router",
    )(h32, rw_t, rb)


def _dispatch_kernel(last_ref, slots_ref, h_ref, xs_hbm, zbuf, zsem, sem, *, n_exp, td, ch):
    i = pl.program_id(0)
    tile_rows = zbuf.shape[0]

    def zero_copy(e):
        dst = pl.ds(pl.multiple_of(last_ref[e], tile_rows), tile_rows)
        return pltpu.make_async_copy(zbuf, xs_hbm.at[dst], zsem)

    @pl.when(i == 0)
    def _():
        zbuf[...] = jnp.zeros_like(zbuf)

        def start(e, c):
            @pl.when(last_ref[e] >= 0)
            def _():
                zero_copy(e).start()
            return c

        def wait(e, c):
            @pl.when(last_ref[e] >= 0)
            def _():
                zero_copy(e).wait()
            return c

        lax.fori_loop(0, n_exp, start, 0)
        lax.fori_loop(0, n_exp, wait, 0)

    def row_copy(t, k):
        src = pl.ds(pl.multiple_of(t * ch, ch), ch)
        dst = pl.ds(pl.multiple_of(slots_ref[k, t], ch), ch)
        return pltpu.make_async_copy(h_ref.at[src], xs_hbm.at[dst], sem)

    def start_rows(t, c):
        for k in range(TOP_K):
            row_copy(t, k).start()
        return c

    def wait_rows(t, c):
        for k in range(TOP_K):
            row_copy(t, k).wait()
        return c

    lax.fori_loop(0, td, start_rows, 0)
    lax.fori_loop(0, td, wait_rows, 0)


def _dispatch(h32, slot_rows, last_rows, n_slots, tm_e, ch):
    T = slot_rows.shape[1]
    td = _tile(DISPATCH_TD, T)
    E = last_rows.shape[0]
    return pl.pallas_call(
        functools.partial(_dispatch_kernel, n_exp=E, td=td, ch=ch),
        grid_spec=pltpu.PrefetchScalarGridSpec(
            num_scalar_prefetch=1,
            grid=(T // td,),
            in_specs=[pl.BlockSpec((TOP_K, td), lambda i, last: (0, i), memory_space=pltpu.SMEM),
                      pl.BlockSpec((td * ch, LANES), lambda i, last: (i, 0))],
            out_specs=pl.BlockSpec(memory_space=pl.ANY),
            scratch_shapes=[pltpu.VMEM((tm_e * ch, LANES), U32), pltpu.SemaphoreType.DMA,
                            pltpu.SemaphoreType.DMA]),
        out_shape=jax.ShapeDtypeStruct((n_slots * ch, LANES), U32),
        compiler_params=_cparams("arbitrary"),
        name="moe_dispatch",
    )(last_rows, slot_rows, h32)


def _mlp_rows(x_ref, wgu_ref, wd_ref, n_tok):
    f = wd_ref.shape[0]
    lo, hi = _unpack_pair(_load_slab_words(x_ref, n_tok))
    half = lo.shape[1]
    gu = _dot(lo, wgu_ref[:half, :]) + _dot(hi, wgu_ref[half:, :])
    a = (jax.nn.silu(gu[:, :f]) * gu[:, f:]).astype(BF16)
    return _dot(a, wd_ref[...])


def _expert_kernel(te_ref, nt_ref, x_ref, wg_ref, wu_ref, wd_ref, y_ref, wgu_s, wd_s, *, tm_e):
    i = pl.program_id(0)

    @pl.when(i < nt_ref[0])
    def _():
        @pl.when(jnp.logical_or(i == 0, te_ref[i] != te_ref[jnp.maximum(i - 1, 0)]))
        def _():
            f = wd_s.shape[0]
            wgu_s[:, :f] = wg_ref[0, 0].astype(BF16)
            wgu_s[:, f:] = wu_ref[0, 0].astype(BF16)
            wd_s[...] = wd_ref[0, 0].astype(BF16)

        y = _mlp_rows(x_ref, wgu_s, wd_s, tm_e)
        half = y.shape[1] // 2
        _store_slab_words(y_ref, _pack_pair(y[:, :half], y[:, half:]))


def _shared_kernel(x_ref, wgu_ref, wd_ref, y_ref, *, tm):
    y_ref[...] = _mlp_rows(x_ref, wgu_ref, wd_ref, tm)


def _experts(xs, layer, w_gate, w_up, w_down, tile_expert, n_tiles, tm_e, ch):
    _, E, D, F = w_gate.shape
    n_slots = xs.shape[0] // ch
    rows = pl.BlockSpec((tm_e * ch, LANES), lambda i, te, nt: (jnp.minimum(i, nt[0] - 1), 0))
    w_in = pl.BlockSpec((1, 1, D, F), lambda i, te, nt: (layer, te[i], 0, 0))
    return pl.pallas_call(
        functools.partial(_expert_kernel, tm_e=tm_e),
        grid_spec=pltpu.PrefetchScalarGridSpec(
            num_scalar_prefetch=2,
            grid=(n_slots // tm_e,),
            in_specs=[rows, w_in, w_in,
                      pl.BlockSpec((1, 1, F, D), lambda i, te, nt: (layer, te[i], 0, 0))],
            out_specs=rows,
            scratch_shapes=[pltpu.VMEM((D, 2 * F), BF16), pltpu.VMEM((F, D), BF16)]),
        out_shape=jax.ShapeDtypeStruct(xs.shape, U32),
        compiler_params=_cparams("arbitrary"),
        name="moe_experts",
    )(tile_expert, n_tiles, xs, w_gate, w_up, w_down)


def _shared_expert(h32, wgu, wd, ch):
    D, F2 = wgu.shape
    T = h32.shape[0] // ch
    tm = _tile(512, T)
    return pl.pallas_call(
        functools.partial(_shared_kernel, tm=tm),
        grid=(T // tm,),
        in_specs=[pl.BlockSpec((tm * ch, LANES), lambda i: (i, 0)),
                  pl.BlockSpec((D, F2), lambda i: (0, 0)),
                  pl.BlockSpec((F2 // 2, D), lambda i: (0, 0))],
        out_specs=pl.BlockSpec((tm, D), lambda i: (i, 0)),
        out_shape=jax.ShapeDtypeStruct((T, D), F32),
        compiler_params=_cparams("parallel"),
        name="moe_shared_expert",
    )(h32, wgu, wd)


def _combine_kernel(slots_ref, w_ref, ysh_ref, y_hbm, o_ref, buf, sem, *, tc, ch):
    def row_copy(t, k):
        src = pl.ds(pl.multiple_of(slots_ref[k, t], ch), ch)
        dst = pl.ds(pl.multiple_of(t * ch, ch), ch)
        return pltpu.make_async_copy(y_hbm.at[src], buf.at[k, dst], sem)

    def start_rows(t, c):
        for k in range(TOP_K):
            row_copy(t, k).start()
        return c

    def wait_rows(t, c):
        for k in range(TOP_K):
            row_copy(t, k).wait()
        return c

    lax.fori_loop(0, tc, start_rows, 0)
    lax.fori_loop(0, tc, wait_rows, 0)

    half = ch * LANES
    w_cols = w_ref[...].T
    gates = [w_cols[:, k:k + 1] for k in range(TOP_K)]
    for c in range(ch):
        lanes_lo = slice(c * LANES, (c + 1) * LANES)
        lanes_hi = slice(half + c * LANES, half + (c + 1) * LANES)
        acc_lo = ysh_ref[:, lanes_lo]
        acc_hi = ysh_ref[:, lanes_hi]
        for k in range(TOP_K):
            lo, hi = _unpack_pair_f32(buf.at[k][pl.ds(c, tc, stride=ch), :])
            acc_lo = acc_lo + gates[k] * lo
            acc_hi = acc_hi + gates[k] * hi
        o_ref[:, lanes_lo] = acc_lo
        o_ref[:, lanes_hi] = acc_hi


def _combine(slot_rows, wts, ysh, ys, ch):
    T, D = ysh.shape
    tc = _tile(COMBINE_TC, T)
    return pl.pallas_call(
        functools.partial(_combine_kernel, tc=tc, ch=ch),
        grid=(T // tc,),
        in_specs=[pl.BlockSpec((TOP_K, tc), lambda i: (0, i), memory_space=pltpu.SMEM),
                  pl.BlockSpec((TOP_K, tc), lambda i: (0, i)),
                  pl.BlockSpec((tc, D), lambda i: (i, 0)),
                  pl.BlockSpec(memory_space=pl.ANY)],
        out_specs=pl.BlockSpec((tc, D), lambda i: (i, 0)),
        out_shape=jax.ShapeDtypeStruct((T, D), F32),
        scratch_shapes=[pltpu.VMEM((TOP_K, tc * ch, LANES), U32), pltpu.SemaphoreType.DMA],
        compiler_params=_cparams("arbitrary"),
        name="moe_combine",
    )(slot_rows, wts, ysh, ys)


def _moe(h32, layer, rw_t, rb, w_gate, w_up, w_down, sh_gu, sh_d):
    E, D = rw_t.shape
    ch = D // 2 // LANES
    assert ch % 8 == 0, "a token slab must be whole (8,128) tiles"
    T = h32.shape[0] // ch
    tm_e = _tile(EXPERT_TM, T)
    eidx, rank, wts, counts = _router(h32, rw_t, rb)

    counts = counts[:, 0]
    tiles = (counts + tm_e - 1) // tm_e
    tile_end = jnp.cumsum(tiles)
    tile_start = tile_end - tiles
    n_tiles = tile_end[-1]
    experts = jnp.arange(E, dtype=I32)
    first_row = jnp.sum(jnp.where(eidx[:, :, None] == experts, tile_start * tm_e, 0), axis=-1)
    slot_rows = (first_row + rank) * ch
    max_tiles = T * TOP_K // tm_e + E
    tile_ids = jnp.minimum(jnp.arange(max_tiles, dtype=I32), n_tiles - 1)
    tile_expert = jnp.sum((tile_ids[:, None] >= tile_end[None, :]).astype(I32), axis=-1)
    last_rows = jnp.where(tiles > 0, (tile_end - 1) * (tm_e * ch), -1).astype(I32)

    xs = _dispatch(h32, slot_rows, last_rows, max_tiles * tm_e, tm_e, ch)
    ys = _experts(xs, layer, w_gate, w_up, w_down, tile_expert,
                  n_tiles.reshape(1).astype(I32), tm_e, ch)
    ysh = _shared_expert(h32, sh_gu, sh_d, ch)
    return _combine(slot_rows, wts, ysh, ys, ch)


def kernel(x, c, positions, ada_w, ada_b, ln_g, ln_b, a_w_qkv, a_w_o, kv_ada_w, kv_ada_b, b_w_kv, b_w_q, b_w_o, router_w, router_b, w_gate, w_up, w_down, sh_gate, sh_up, sh_down):
    B, S, D = x.shape
    T = B * S
    assert D % HEAD_DIM == 0 and DEPTH == ada_w.shape[0] == 2
    xf = x.reshape(T, D)

    c8 = jnp.zeros((8, D), F32).at[:B].set(c)
    mods = _adaln(c8, ada_w, ada_b)[:, :B].reshape(DEPTH, B, 6, D)
    kv_mods = _adaln(c8, kv_ada_w[None], kv_ada_b[None])[0, :B].reshape(B, 2, D)
    sh_a, sc_a, g_a, sh_m, sc_m, g_m = (mods[:, :, r] for r in range(6))
    rope = _rope_tables(positions)

    def moe_weights(l):
        sh_gu = jnp.concatenate([sh_gate[l], sh_up[l]], axis=-1).astype(BF16)
        return (l, router_w[l].T.astype(BF16), router_b[l].reshape(-1, 1), w_gate, w_up, w_down,
                sh_gu, sh_down[l].astype(BF16))

    h = _modulate(xf, _mod_rows(None, [(sh_a[0], sc_a[0])]), S)
    qkv = _mm(h, a_w_qkv[0].astype(BF16), B=B, S=S, head_major_out=True, rope=rope,
              n_rope_cols=2 * D)
    o = _dilated_attention(qkv)
    y = _mm(o, a_w_o[0].astype(BF16), B=B, S=S)
    xf, h32 = _ln(xf, y, _mod_rows(g_a[0], [(sh_m[0], sc_m[0])]), ln_g[0, 0], ln_b[0, 0],
                  ["u32"], S)
    y = _moe(h32, *moe_weights(0))
    xf, h_kv, h_q = _ln(xf, y, _mod_rows(g_m[0], [(kv_mods[:, 0], kv_mods[:, 1]),
                                                  (sh_a[1], sc_a[1])]),
                        ln_g[0, 1], ln_b[0, 1], ["bf16", "bf16"], S)

    kv = _mm(h_kv, b_w_kv.astype(BF16), B=B, S=S, head_major_out=True)
    q = _mm(h_q, b_w_q[0].astype(BF16), B=B, S=S, head_major_out=True)
    o = _stick_breaking_attention(q, kv)
    y = _mm(o, b_w_o[0].astype(BF16), B=B, S=S)
    xf, h32 = _ln(xf, y, _mod_rows(g_a[1], [(sh_m[1], sc_m[1])]), ln_g[1, 0], ln_b[1, 0],
                  ["u32"], S)
    y = _moe(h32, *moe_weights(1))
    (xf,) = _ln(xf, y, _mod_rows(g_m[1], []), ln_g[1, 1], ln_b[1, 1], [], S)
    return xf.reshape(B, S, D)
```

```python
import functools
import math

import jax
import jax.numpy as jnp
from jax import lax
from jax.experimental import pallas as pl
from jax.experimental.pallas import tpu as pltpu

F32 = jnp.float32
BF16 = jnp.bfloat16
U32 = jnp.uint32
I32 = jnp.int32

HEAD_DIM = 128
ROT_DIM = HEAD_DIM // 4
ROPE_THETA = 500000.0
DILATED_BRANCHES = ((128, 1), (512, 4), (2048, 16))
DILATIONS = tuple(d for _, d in DILATED_BRANCHES)
BLOCK = 128
TOP_K = 8
N_GROUPS = 8
TOPK_GROUPS = 4
ROUTED_SCALE = 2.5
DEPTH = 2
DEEPNORM_ALPHA = (2 * DEPTH) ** 0.25
LN_EPS = 1e-5

LANES = 128
VMEM_LIMIT_BYTES = 56 * 1024 * 1024

MM_TM = 1024
MM_TN = 512
LN_TM = 256
ROUTER_TM = 512
EXPERT_TM = 256
DISPATCH_TD = 128
COMBINE_TC = 128
SUPER = 2048
DIL_GROUP = 8
SB_TQ = 256
SB_GROUP = 4
SB_SKIP = 127.0
LOG2_E = 1.4426950408889634

NEG_INF = float("-inf")
HI_MASK = 0xFFFF0000


def _tile(pref, dim):
    t = min(pref, dim)
    assert dim % t == 0, (pref, dim)
    return t


def _cparams(*sem):
    return pltpu.CompilerParams(dimension_semantics=sem, vmem_limit_bytes=VMEM_LIMIT_BYTES)


def _dot(a, b):
    return jnp.dot(a, b, preferred_element_type=F32)


def _dot_nt(a, b):
    return lax.dot_general(a, b, (((1,), (1,)), ((), ())), preferred_element_type=F32)


def _unpack_pair(w):
    lo = lax.bitcast_convert_type(w << 16, F32).astype(BF16)
    hi = lax.bitcast_convert_type(w & jnp.uint32(HI_MASK), F32).astype(BF16)
    return lo, hi


def _unpack_pair_f32(w):
    lo = lax.bitcast_convert_type(w << 16, F32)
    hi = lax.bitcast_convert_type(w & jnp.uint32(HI_MASK), F32)
    return lo, hi


def _pack_pair(lo, hi):
    lo_b = lax.bitcast_convert_type(lo.astype(BF16).astype(F32), U32)
    hi_b = lax.bitcast_convert_type(hi.astype(BF16).astype(F32), U32)
    return (hi_b & jnp.uint32(HI_MASK)) | (lo_b >> 16)


def _load_slab_words(ref, n_tok):
    ch = ref.shape[0] // n_tok
    return jnp.concatenate([ref[pl.ds(c, n_tok, stride=ch), :] for c in range(ch)], axis=1)


def _store_slab_words(ref, words):
    n_tok = words.shape[0]
    ch = ref.shape[0] // n_tok
    for c in range(ch):
        ref[pl.ds(c, n_tok, stride=ch), :] = words[:, c * LANES:(c + 1) * LANES]


def _adaln_kernel(c_ref, w_ref, b_ref, o_ref, *, nk):
    k = pl.program_id(2)

    @pl.when(k == 0)
    def _():
        o_ref[...] = jnp.zeros_like(o_ref)

    cond = jax.nn.silu(c_ref[...])
    o_ref[0] += jnp.dot(cond, w_ref[0], preferred_element_type=F32,
                        precision=lax.Precision.HIGHEST)

    @pl.when(k == nk - 1)
    def _():
        o_ref[0] += b_ref[0]


def _adaln(c8, w, b):
    L, D, N = w.shape
    tk = _tile(512, D)
    tn = _tile(2048, N)
    nk = D // tk
    return pl.pallas_call(
        functools.partial(_adaln_kernel, nk=nk),
        grid=(L, N // tn, nk),
        in_specs=[pl.BlockSpec((8, tk), lambda l, j, k: (0, k)),
                  pl.BlockSpec((1, tk, tn), lambda l, j, k: (l, k, j)),
                  pl.BlockSpec((1, 1, tn), lambda l, j, k: (l, 0, j))],
        out_specs=pl.BlockSpec((1, 8, tn), lambda l, j, k: (l, 0, j)),
        out_shape=jax.ShapeDtypeStruct((L, 8, N), F32),
        compiler_params=_cparams("parallel", "parallel", "arbitrary"),
        name="adaln",
    )(c8, w, b.reshape(L, 1, N))


def _rope_kernel(pos_ref, freq_ref, sign_ref, cos_ref, sin_ref):
    ang = pos_ref[...] * freq_ref[...]
    cos_ref[...] = jnp.cos(ang)
    sin_ref[...] = jnp.sin(ang) * sign_ref[...]


def _rope_tables(positions):
    T = positions.size
    half = ROT_DIM // 2
    inv_freq = ROPE_THETA ** (-jnp.arange(0, ROT_DIM, 2, dtype=F32) / ROT_DIM)
    freq = jnp.concatenate([inv_freq, inv_freq, jnp.zeros((HEAD_DIM - ROT_DIM,), F32)])[None, :]
    sign = jnp.concatenate([-jnp.ones((half,), F32), jnp.ones((HEAD_DIM - half,), F32)])[None, :]
    pos = jnp.broadcast_to(positions.astype(F32).reshape(T, 1), (T, HEAD_DIM))
    tm = _tile(2048, T)
    row = pl.BlockSpec((tm, HEAD_DIM), lambda i: (i, 0))
    vec = pl.BlockSpec((1, HEAD_DIM), lambda i: (0, 0))
    return pl.pallas_call(
        _rope_kernel,
        grid=(T // tm,),
        in_specs=[row, vec, vec],
        out_specs=[row, row],
        out_shape=[jax.ShapeDtypeStruct((T, HEAD_DIM), F32)] * 2,
        compiler_params=_cparams("parallel"),
        name="rope_tables",
    )(pos, freq, sign)


def _emit_mod(xn, rows_ref, idx, fmt, out_ref):
    shift = rows_ref[0, 1 + 2 * idx:2 + 2 * idx, :]
    scale = rows_ref[0, 2 + 2 * idx:3 + 2 * idx, :]
    h = xn * (1.0 + scale) + shift
    if fmt == "bf16":
        out_ref[...] = h.astype(BF16)
    else:
        half = h.shape[1] // 2
        _store_slab_words(out_ref, _pack_pair(h[:, :half], h[:, half:]))


def _modulate_kernel(x_ref, rows_ref, o_ref):
    _emit_mod(x_ref[...], rows_ref, 0, "bf16", o_ref)


def _ln_kernel(x_ref, y_ref, rows_ref, g_ref, b_ref, xo_ref, *h_refs, fmts):
    gate = rows_ref[0, 0:1, :]
    z = DEEPNORM_ALPHA * x_ref[...] + (1.0 + gate) * y_ref[...]
    mu = jnp.mean(z, axis=-1, keepdims=True)
    zc = z - mu
    var = jnp.mean(zc * zc, axis=-1, keepdims=True)
    xn = zc * lax.rsqrt(var + LN_EPS) * g_ref[...] + b_ref[...]
    xo_ref[...] = xn
    for idx, (fmt, h_ref) in enumerate(zip(fmts, h_refs)):
        _emit_mod(xn, rows_ref, idx, fmt, h_ref)


def _mod_rows(gate, pairs):
    B, D = pairs[0][0].shape if pairs else gate.shape
    rows = [gate if gate is not None else jnp.zeros((B, D), F32)]
    for shift, scale in pairs:
        rows += [shift, scale]
    rows += [jnp.zeros((B, D), F32)] * (8 - len(rows))
    return jnp.stack(rows, axis=1)


def _out_struct(T, D, tm, fmt):
    if fmt == "bf16":
        return jax.ShapeDtypeStruct((T, D), BF16), (tm, D)
    ch = D // 2 // LANES
    return jax.ShapeDtypeStruct((T * ch, LANES), U32), (tm * ch, LANES)


def _modulate(x, rows, S):
    T, D = x.shape
    tm = _tile(LN_TM, S)
    nb = S // tm
    return pl.pallas_call(
        _modulate_kernel,
        grid=(T // tm,),
        in_specs=[pl.BlockSpec((tm, D), lambda i: (i, 0)),
                  pl.BlockSpec((1, 8, D), lambda i: (i // nb, 0, 0))],
        out_specs=pl.BlockSpec((tm, D), lambda i: (i, 0)),
        out_shape=jax.ShapeDtypeStruct((T, D), BF16),
        compiler_params=_cparams("parallel"),
        name="modulate",
    )(x, rows)


def _ln(x, y, rows, g, b, fmts, S):
    T, D = x.shape
    tm = _tile(LN_TM, S)
    nb = S // tm
    row = pl.BlockSpec((tm, D), lambda i: (i, 0))
    vec = pl.BlockSpec((1, D), lambda i: (0, 0))
    out_shape = [jax.ShapeDtypeStruct((T, D), F32)]
    out_specs = [row]
    for fmt in fmts:
        st, blk = _out_struct(T, D, tm, fmt)
        out_shape.append(st)
        out_specs.append(pl.BlockSpec(blk, lambda i: (i, 0)))
    return pl.pallas_call(
        functools.partial(_ln_kernel, fmts=tuple(fmts)),
        grid=(T // tm,),
        in_specs=[row, row, pl.BlockSpec((1, 8, D), lambda i: (i // nb, 0, 0)), vec, vec],
        out_specs=out_specs,
        out_shape=out_shape,
        compiler_params=_cparams("parallel"),
        name="deepnorm_ln",
    )(x, y, rows, g.reshape(1, D), b.reshape(1, D))


def _swap_halves(x):
    half = ROT_DIM // 2
    lane = lax.broadcasted_iota(I32, x.shape, 1)
    return jnp.where(lane < half, pltpu.roll(x, HEAD_DIM - half, 1), pltpu.roll(x, half, 1))


def _mm_kernel(*refs, a_heads, out_heads, n_rope_tiles, dilations):
    a_ref, w_ref = refs[:2]
    refs = refs[2:]
    if n_rope_tiles:
        cos_ref, sin_ref = refs[:2]
        refs = refs[2:]
    o_ref = refs[0]
    view_refs = refs[1:1 + len(dilations)]
    stage = refs[1 + len(dilations)] if dilations else None
    if a_heads:
        a = jnp.concatenate([a_ref[0, h] for h in range(a_heads)], axis=-1)
    else:
        a = a_ref[...]
    acc = _dot(a, w_ref[...])
    if not out_heads:
        o_ref[...] = acc
        return
    tm = acc.shape[0]

    def write(rot):
        for hh in range(out_heads):
            xh = acc[:, hh * HEAD_DIM:(hh + 1) * HEAD_DIM]
            if rot:
                xh = xh * cos_ref[...] + _swap_halves(xh) * sin_ref[...]
            o_ref[0, 0, hh] = xh.astype(BF16)
            if dilations:
                stage[hh] = xh
                for d, v_ref in zip(dilations, view_refs):
                    for r in range(d):
                        v_ref[0, 0, hh, :, r * HEAD_DIM:(r + 1) * HEAD_DIM] = (
                            stage[hh, pl.ds(r, tm // d, stride=d), :].astype(BF16))

    if not n_rope_tiles:
        write(False)
    else:
        j = pl.program_id(1)

        @pl.when(j < n_rope_tiles)
        def _():
            write(True)

        @pl.when(j >= n_rope_tiles)
        def _():
            write(False)


def _mm(a, w, *, B, S, head_major_out=False, rope=None, n_rope_cols=0, dilations=()):
    K, N = w.shape
    T = B * S
    H = K // HEAD_DIM
    tm = _tile(MM_TM, S)
    tn = _tile(MM_TN, N)
    nsb = S // tm
    a_heads = H if a.ndim == 4 else 0
    if a_heads:
        a_spec = pl.BlockSpec((1, H, tm, HEAD_DIM), lambda i, j: (i // nsb, 0, i % nsb, 0))
    else:
        a_spec = pl.BlockSpec((tm, K), lambda i, j: (i, 0))
    in_specs = [a_spec, pl.BlockSpec((K, tn), lambda i, j: (0, j))]
    args = [a, w]
    out_heads = tn // HEAD_DIM if head_major_out else 0
    n_rope_tiles = 0
    if head_major_out:
        D = H * HEAD_DIM
        G = N // D
        tiles_per_group = D // tn
        if rope is not None:
            assert n_rope_cols % tn == 0
            n_rope_tiles = n_rope_cols // tn
            tab = pl.BlockSpec((tm, HEAD_DIM), lambda i, j: (i, 0))
            in_specs += [tab, tab]
            args += list(rope)
        head_index = lambda i, j: (j // tiles_per_group, i // nsb, j % tiles_per_group, i % nsb, 0)
        out_spec = [pl.BlockSpec((1, 1, out_heads, tm // d, d * HEAD_DIM), head_index)
                    for d in (1,) + tuple(dilations)]
        out_shape = [jax.ShapeDtypeStruct((G, B, H, S // d, d * HEAD_DIM), BF16)
                     for d in (1,) + tuple(dilations)]
    else:
        out_spec = pl.BlockSpec((tm, tn), lambda i, j: (i, j))
        out_shape = jax.ShapeDtypeStruct((T, N), F32)
    scratch = [pltpu.VMEM((out_heads, tm, HEAD_DIM), F32)] if dilations else []
    out = pl.pallas_call(
        functools.partial(_mm_kernel, a_heads=a_heads, out_heads=out_heads,
                          n_rope_tiles=n_rope_tiles, dilations=tuple(dilations)),
        grid=(T // tm, N // tn),
        in_specs=in_specs,
        out_specs=out_spec,
        out_shape=out_shape,
        scratch_shapes=scratch,
        compiler_params=_cparams("parallel", "parallel"),
        name="proj",
    )(*args)
    if head_major_out and not dilations:
        return out[0]
    return out


def _dil_attn_kernel(q1, q4, q16, k1c, k4c, k16c, k1p, k4p, k16p, v1c, v4c, v16c, v1p, v4p, v16p,
                     o_ref, *scratch, scale):
    acc_s, m_s, l_s = scratch[0:4], scratch[4:8], scratch[8:12]
    bias_s, s_s, p_s, mu_s, st_acc, st_m, st_l, out_s = scratch[12:20]

    qi = lax.broadcasted_iota(I32, (BLOCK, 2 * BLOCK), 0)
    kc = lax.broadcasted_iota(I32, (BLOCK, 2 * BLOCK), 1)
    band = jnp.where(kc >= qi, jnp.where(kc <= qi + BLOCK, 0.0, NEG_INF), NEG_INF).astype(F32)
    bias_s[0] = band
    no_prev = jnp.where(pl.program_id(2) == 0, NEG_INF, 0.0).astype(F32)
    bias_s[1] = jnp.where(kc < BLOCK, band + no_prev, band)
    ones = jnp.ones((2 * BLOCK, HEAD_DIM), BF16)

    def run(units):
        for g, (q, kp, kc_, _, _, first, _) in enumerate(units):
            keys = jnp.concatenate([kp(), kc_()], axis=0)
            s_s[g] = _dot_nt(q(), keys) * scale + bias_s[1 if first else 0]
        for g in range(len(units)):
            s = s_s[g]
            m = jnp.max(jnp.maximum(s[:, :BLOCK], s[:, BLOCK:]), axis=1, keepdims=True)
            p_s[g] = jnp.exp(s - m).astype(BF16)
            mu_s[g] = jnp.broadcast_to(m, (BLOCK, HEAD_DIM))
        for g, (_, _, _, vp, vc, _, sink) in enumerate(units):
            p = p_s[g]
            vals = jnp.concatenate([vp(), vc()], axis=0)
            sink(g, _dot(p, vals), mu_s[g], _dot(p, ones))

    def fold(slab, rows, acc_u, m_u, l_u):
        m_o = m_s[slab][rows, :]
        m_n = jnp.maximum(m_o, m_u)
        e_o = jnp.exp(m_o - m_n)
        e_u = jnp.exp(m_u - m_n)
        acc_s[slab][rows, :] = acc_s[slab][rows, :] * e_o + acc_u * e_u
        l_s[slab][rows, :] = l_s[slab][rows, :] * e_o + l_u * e_u
        m_s[slab][rows, :] = m_n

    def tile(ref, rows, lanes):
        return lambda: ref[0, 0, 0, rows, lanes]

    def banded(q, kc_, kp, vc, vp, n, lanes, sink):
        cur = slice(n * BLOCK, (n + 1) * BLOCK)
        if n == 0:
            k_prev, v_prev = tile(kp, slice(None), lanes), tile(vp, slice(None), lanes)
        else:
            prv = slice((n - 1) * BLOCK, n * BLOCK)
            k_prev, v_prev = tile(kc_, prv, lanes), tile(vc, prv, lanes)
        return (tile(q, cur, lanes), k_prev, tile(kc_, cur, lanes), v_prev, tile(vc, cur, lanes),
                n == 0, sink)

    units = []
    for n in range(SUPER // 4 // BLOCK):
        for r in range(4):
            def init(g, acc, m, l, r=r, n=n):
                rows = slice(n * BLOCK, (n + 1) * BLOCK)
                acc_s[r][rows, :], m_s[r][rows, :], l_s[r][rows, :] = acc, m, l

            units.append(banded(q4, k4c, k4p, v4c, v4p, n,
                                slice(r * HEAD_DIM, (r + 1) * HEAD_DIM), init))

    for n in range(SUPER // BLOCK):
        def fold1(g, acc, m, l, n=n):
            st_acc[g], st_m[g], st_l[g] = acc, m, l
            rows = slice(n * (BLOCK // 4), (n + 1) * (BLOCK // 4))
            for r in range(4):
                pick = pl.ds(r, BLOCK // 4, stride=4)
                fold(r, rows, st_acc[g, pick, :], st_m[g, pick, :], st_l[g, pick, :])

        units.append(banded(q1, k1c, k1p, v1c, v1p, n, slice(None), fold1))

    for r in range(16):
        def fold16(g, acc, m, l, r=r):
            fold(r % 4, pl.ds(r // 4, BLOCK, stride=4), acc, m, l)

        units.append(banded(q16, k16c, k16p, v16c, v16p, 0,
                            slice(r * HEAD_DIM, (r + 1) * HEAD_DIM), fold16))

    for start in range(0, len(units), DIL_GROUP):
        run(units[start:start + DIL_GROUP])

    for r in range(4):
        out_s[pl.ds(r, SUPER // 4, stride=4), :] = acc_s[r][...] / l_s[r][...]
    o_ref[0, 0] = out_s[...].astype(BF16)


def _dilated_attention(qkv1, qkv4, qkv16):
    _, B, H, S, Dh = qkv1.shape
    assert Dh == HEAD_DIM and S % SUPER == 0
    assert tuple(d for _, d in DILATED_BRANCHES) == DILATIONS
    assert all(w // d == BLOCK for w, d in DILATED_BRANCHES)
    views = {1: qkv1, 4: qkv4, 16: qkv16}

    def cur(which, d):
        return pl.BlockSpec((1, 1, 1, SUPER // d, d * Dh), lambda b, h, m: (which, b, h, m, 0))

    def prev(which, d):
        per = SUPER // d // BLOCK
        return pl.BlockSpec((1, 1, 1, BLOCK, d * Dh),
                            lambda b, h, m: (which, b, h, jnp.maximum(m * per - 1, 0), 0))

    in_specs, args = [], []
    for make, which in ((cur, 0), (cur, 1), (prev, 1), (cur, 2), (prev, 2)):
        for d in (1, 4, 16):
            in_specs.append(make(which, d))
            args.append(views[d])
    state = [pltpu.VMEM((SUPER // 4, Dh), F32)] * 12
    stage = [pltpu.VMEM((2, BLOCK, 2 * BLOCK), F32),
             pltpu.VMEM((DIL_GROUP, BLOCK, 2 * BLOCK), F32),
             pltpu.VMEM((DIL_GROUP, BLOCK, 2 * BLOCK), BF16),
             pltpu.VMEM((DIL_GROUP, BLOCK, Dh), F32)] + [
             pltpu.VMEM((DIL_GROUP, BLOCK, Dh), F32)] * 3 + [
             pltpu.VMEM((SUPER, Dh), F32)]
    return pl.pallas_call(
        functools.partial(_dil_attn_kernel, scale=Dh ** -0.5),
        grid=(B, H, S // SUPER),
        in_specs=in_specs,
        out_specs=pl.BlockSpec((1, 1, SUPER, Dh), lambda b, h, m: (b, h, m, 0)),
        out_shape=jax.ShapeDtypeStruct((B, H, S, Dh), BF16),
        scratch_shapes=state + stage,
        compiler_params=_cparams("parallel", "parallel", "parallel"),
        name="dilated_attention",
    )(*args)


def _sb_kernel(q_ref, k_ref, v_ref, o_ref, tri_ref, acc_ref, carry_ref, *, scale, tq, nq):
    row = lax.broadcasted_iota(I32, (tq, tq), 0)
    col = lax.broadcasted_iota(I32, (tq, tq), 1)
    tri_ref[...] = jnp.where(row > col, 1.0, 0.0).astype(BF16)

    def suffix_sum(x):
        hi = x.astype(BF16)
        lo = (x - hi.astype(F32)).astype(BF16)
        return _dot(hi, tri_ref[...]) + _dot(lo, tri_ref[...])

    def logs(q, k):
        z2 = _dot_nt(q, k) * (scale * LOG2_E)
        t = jnp.log2(1.0 + jnp.exp2(-jnp.abs(z2)))
        return jnp.minimum(z2, 0.0) - t, -(jnp.maximum(z2, 0.0) + t)

    def qgroup(it, carry):
        n0 = it * SB_GROUP

        def block_rows(n):
            return pl.ds(pl.multiple_of(n * tq, tq), tq)

        causal = col < row
        tops = []
        for g in range(SB_GROUP):
            rows = block_rows(n0 + g)
            log_b, log_1mb = logs(q_ref[0, 0, rows, :], k_ref[0, 0, 0, rows, :])
            log_1mb = jnp.where(causal, log_1mb, 0.0)
            a = jnp.where(causal, jnp.exp2(log_b + suffix_sum(log_1mb)), 0.0)
            acc_ref[g] = _dot(a.astype(BF16), v_ref[0, 0, 0, rows, :])
            c0 = jnp.broadcast_to(jnp.sum(log_1mb, axis=1, keepdims=True), (tq, tq))
            carry_ref[g] = c0
            tops.append(jnp.where(n0 + g >= 1, jnp.max(c0), NEG_INF))

        def cond(st):
            _, top = st
            return top > -SB_SKIP

        def body(st):
            i, _ = st
            tops = []
            for g in range(SB_GROUP):
                j = n0 + g - i
                live = j >= 0
                krows = block_rows(jnp.maximum(j, 0))
                log_b, log_1mb = logs(q_ref[0, 0, block_rows(n0 + g), :], k_ref[0, 0, 0, krows, :])
                c = carry_ref[g]
                a = jnp.where(live, jnp.exp2(log_b + suffix_sum(log_1mb) + c), 0.0)
                acc_ref[g] += _dot(a.astype(BF16), v_ref[0, 0, 0, krows, :])
                c = c + jnp.broadcast_to(jnp.sum(log_1mb, axis=1, keepdims=True), (tq, tq))
                carry_ref[g] = c
                tops.append(jnp.where(j >= 1, jnp.max(c), NEG_INF))
            return i + 1, functools.reduce(jnp.maximum, tops)

        lax.while_loop(cond, body, (1, functools.reduce(jnp.maximum, tops)))
        for g in range(SB_GROUP):
            o_ref[0, 0, block_rows(n0 + g), :] = acc_ref[g].astype(BF16)
        return carry

    lax.fori_loop(0, nq // SB_GROUP, qgroup, 0)


def _stick_breaking_attention(q, kv):
    _, B, H, S, Dh = q.shape
    tq = _tile(SB_TQ, S)
    seq = lambda which: pl.BlockSpec((1, 1, 1, S, Dh), lambda b, h: (which, b, h, 0, 0))
    return pl.pallas_call(
        functools.partial(_sb_kernel, scale=Dh ** -0.5, tq=tq, nq=S // tq),
        grid=(B, H),
        in_specs=[pl.BlockSpec((1, 1, S, Dh), lambda b, h: (b, h, 0, 0)), seq(0), seq(1)],
        out_specs=pl.BlockSpec((1, 1, S, Dh), lambda b, h: (b, h, 0, 0)),
        out_shape=jax.ShapeDtypeStruct((B, H, S, Dh), BF16),
        scratch_shapes=[pltpu.VMEM((tq, tq), BF16), pltpu.VMEM((SB_GROUP, tq, Dh), F32),
                        pltpu.VMEM((SB_GROUP, tq, tq), F32)],
        compiler_params=_cparams("parallel", "parallel"),
        name="stick_breaking_attention",
    )(q.reshape(B, H, S, Dh), kv, kv)


def _router_kernel(h_ref, rw_ref, rb_ref, e_ref, r_ref, w_ref, cnt_ref, tri_ref, carry_ref,
                   *, n_exp, tm, nsteps):
    i = pl.program_id(0)
    gsz = n_exp // N_GROUPS

    @pl.when(i == 0)
    def _():
        row = lax.broadcasted_iota(I32, (tm, tm), 0)
        col = lax.broadcasted_iota(I32, (tm, tm), 1)
        tri_ref[...] = jnp.where(row < col, 1.0, 0.0).astype(BF16)
        carry_ref[...] = jnp.zeros_like(carry_ref)

    lo, hi = _unpack_pair(_load_slab_words(h_ref, tm))
    half = lo.shape[1]
    logits = _dot_nt(rw_ref[:, :half], lo) + _dot_nt(rw_ref[:, half:], hi)
    scores = jax.nn.sigmoid(logits)
    sel = scores + rb_ref[...]

    sub = lax.broadcasted_iota(I32, (gsz, tm), 0)
    groups, gscore = [], []
    for g in range(N_GROUPS):
        sg = sel[g * gsz:(g + 1) * gsz, :]
        m1 = jnp.max(sg, axis=0, keepdims=True)
        i1 = jnp.min(jnp.where(sg == m1, sub, gsz), axis=0, keepdims=True)
        m2 = jnp.max(jnp.where(sub == i1, NEG_INF, sg), axis=0, keepdims=True)
        groups.append(sg)
        gscore.append(m1 + m2)
    masked = []
    for g in range(N_GROUPS):
        beat = jnp.zeros((1, tm), F32)
        for o in range(N_GROUPS):
            if o == g:
                continue
            beat += jnp.where(gscore[o] > gscore[g], 1.0, 0.0)
            if o < g:
                beat += jnp.where(gscore[o] == gscore[g], 1.0, 0.0)
        masked.append(jnp.where(beat < TOPK_GROUPS, groups[g], NEG_INF))
    x = jnp.concatenate(masked, axis=0)

    eid = lax.broadcasted_iota(I32, (n_exp, tm), 0)
    chosen = jnp.zeros((n_exp, tm), F32)
    picks, vals = [], []
    for _ in range(TOP_K):
        mx = jnp.max(x, axis=0, keepdims=True)
        idx = jnp.min(jnp.where(x == mx, eid, n_exp), axis=0, keepdims=True)
        hit = eid == idx
        picks.append(idx)
        vals.append(jnp.sum(jnp.where(hit, scores, 0.0), axis=0, keepdims=True))
        chosen += jnp.where(hit, 1.0, 0.0)
        x = jnp.where(hit, NEG_INF, x)
    total = vals[0]
    for v in vals[1:]:
        total = total + v

    rank_all = _dot(chosen.astype(BF16), tri_ref[...]) + carry_ref[...]
    for k in range(TOP_K):
        e_ref[k:k + 1, :] = picks[k]
        w_ref[k:k + 1, :] = vals[k] / total * ROUTED_SCALE
        rk = jnp.sum(jnp.where(eid == picks[k], rank_all, 0.0), axis=0, keepdims=True)
        r_ref[k:k + 1, :] = rk.astype(I32)
    carry_ref[...] += jnp.broadcast_to(jnp.sum(chosen, axis=1, keepdims=True), (n_exp, tm))

    @pl.when(i == nsteps - 1)
    def _():
        cnt_ref[...] = carry_ref[:, 0:LANES].astype(I32)


def _router(h32, rw_t, rb):
    E, D = rw_t.shape
    ch = D // 2 // LANES
    T = h32.shape[0] // ch
    tm = _tile(ROUTER_TM, T)
    nsteps = T // tm
    tok = pl.BlockSpec((TOP_K, tm), lambda i: (0, i))
    return pl.pallas_call(
        functools.partial(_router_kernel, n_exp=E, tm=tm, nsteps=nsteps),
        grid=(nsteps,),
        in_specs=[pl.BlockSpec((tm * ch, LANES), lambda i: (i, 0)),
                  pl.BlockSpec((E, D), lambda i: (0, 0)),
                  pl.BlockSpec((E, 1), lambda i: (0, 0))],
        out_specs=[tok, tok, tok, pl.BlockSpec((E, LANES), lambda i: (0, 0))],
        out_shape=[jax.ShapeDtypeStruct((TOP_K, T), I32), jax.ShapeDtypeStruct((TOP_K, T), I32),
                   jax.ShapeDtypeStruct((TOP_K, T), F32), jax.ShapeDtypeStruct((E, LANES), I32)],
        scratch_shapes=[pltpu.VMEM((tm, tm), BF16), pltpu.VMEM((E, tm), F32)],
        compiler_params=_cparams("arbitrary"),
        name="moe_router",
    )(h32, rw_t, rb)


def _dispatch_kernel(last_ref, slots_ref, h_ref, xs_hbm, zbuf, zsem, sem, *, n_exp, td, ch):
    i = pl.program_id(0)
    tile_rows = zbuf.shape[0]

    def zero_copy(e):
        dst = pl.ds(pl.multiple_of(last_ref[e], tile_rows), tile_rows)
        return pltpu.make_async_copy(zbuf, xs_hbm.at[dst], zsem)

    @pl.when(i == 0)
    def _():
        zbuf[...] = jnp.zeros_like(zbuf)

        def start(e, c):
            @pl.when(last_ref[e] >= 0)
            def _():
                zero_copy(e).start()
            return c

        def wait(e, c):
            @pl.when(last_ref[e] >= 0)
            def _():
                zero_copy(e).wait()
            return c

        lax.fori_loop(0, n_exp, start, 0)
        lax.fori_loop(0, n_exp, wait, 0)

    def row_copy(t, k):
        src = pl.ds(pl.multiple_of(t * ch, ch), ch)
        dst = pl.ds(pl.multiple_of(slots_ref[k, t], ch), ch)
        return pltpu.make_async_copy(h_ref.at[src], xs_hbm.at[dst], sem)

    def start_rows(t, c):
        for k in range(TOP_K):
            row_copy(t, k).start(priority=k % 2)
        return c

    def wait_rows(t, c):
        for k in range(TOP_K):
            row_copy(t, k).wait()
        return c

    lax.fori_loop(0, td, start_rows, 0)
    lax.fori_loop(0, td, wait_rows, 0)


def _dispatch(h32, slot_rows, last_rows, n_slots, tm_e, ch):
    T = slot_rows.shape[1]
    td = _tile(DISPATCH_TD, T)
    E = last_rows.shape[0]
    return pl.pallas_call(
        functools.partial(_dispatch_kernel, n_exp=E, td=td, ch=ch),
        grid_spec=pltpu.PrefetchScalarGridSpec(
            num_scalar_prefetch=1,
            grid=(T // td,),
            in_specs=[pl.BlockSpec((TOP_K, td), lambda i, last: (0, i), memory_space=pltpu.SMEM),
                      pl.BlockSpec((td * ch, LANES), lambda i, last: (i, 0))],
            out_specs=pl.BlockSpec(memory_space=pl.ANY),
            scratch_shapes=[pltpu.VMEM((tm_e * ch, LANES), U32), pltpu.SemaphoreType.DMA,
                            pltpu.SemaphoreType.DMA]),
        out_shape=jax.ShapeDtypeStruct((n_slots * ch, LANES), U32),
        compiler_params=_cparams("arbitrary"),
        name="moe_dispatch",
    )(last_rows, slot_rows, h32)


def _mlp_rows(x_ref, wgu_ref, wd_ref, n_tok):
    f = wd_ref.shape[0]
    lo, hi = _unpack_pair(_load_slab_words(x_ref, n_tok))
    half = lo.shape[1]
    gu = _dot(lo, wgu_ref[:half, :]) + _dot(hi, wgu_ref[half:, :])
    a = (jax.nn.silu(gu[:, :f]) * gu[:, f:]).astype(BF16)
    return _dot(a, wd_ref[...])


def _expert_kernel(te_ref, nt_ref, x_ref, wg_ref, wu_ref, wd_ref, y_ref, wgu_s, wd_s, *, tm_e):
    i = pl.program_id(0)

    @pl.when(i < nt_ref[0])
    def _():
        @pl.when(jnp.logical_or(i == 0, te_ref[i] != te_ref[jnp.maximum(i - 1, 0)]))
        def _():
            f = wd_s.shape[0]
            wgu_s[:, :f] = wg_ref[0, 0].astype(BF16)
            wgu_s[:, f:] = wu_ref[0, 0].astype(BF16)
            wd_s[...] = wd_ref[0, 0].astype(BF16)

        y = _mlp_rows(x_ref, wgu_s, wd_s, tm_e)
        half = y.shape[1] // 2
        _store_slab_words(y_ref, _pack_pair(y[:, :half], y[:, half:]))


def _shared_kernel(x_ref, wgu_ref, wd_ref, y_ref, *, tm):
    y_ref[...] = _mlp_rows(x_ref, wgu_ref, wd_ref, tm)


def _experts(xs, layer, w_gate, w_up, w_down, tile_expert, n_tiles, tm_e, ch):
    _, E, D, F = w_gate.shape
    n_slots = xs.shape[0] // ch
    rows = pl.BlockSpec((tm_e * ch, LANES), lambda i, te, nt: (jnp.minimum(i, nt[0] - 1), 0))
    w_in = pl.BlockSpec((1, 1, D, F), lambda i, te, nt: (layer, te[i], 0, 0))
    return pl.pallas_call(
        functools.partial(_expert_kernel, tm_e=tm_e),
        grid_spec=pltpu.PrefetchScalarGridSpec(
            num_scalar_prefetch=2,
            grid=(n_slots // tm_e,),
            in_specs=[rows, w_in, w_in,
                      pl.BlockSpec((1, 1, F, D), lambda i, te, nt: (layer, te[i], 0, 0))],
            out_specs=rows,
            scratch_shapes=[pltpu.VMEM((D, 2 * F), BF16), pltpu.VMEM((F, D), BF16)]),
        out_shape=jax.ShapeDtypeStruct(xs.shape, U32),
        compiler_params=_cparams("arbitrary"),
        name="moe_experts",
    )(tile_expert, n_tiles, xs, w_gate, w_up, w_down)


def _shared_expert(h32, wgu, wd, ch):
    D, F2 = wgu.shape
    T = h32.shape[0] // ch
    tm = _tile(512, T)
    return pl.pallas_call(
        functools.partial(_shared_kernel, tm=tm),
        grid=(T // tm,),
        in_specs=[pl.BlockSpec((tm * ch, LANES), lambda i: (i, 0)),
                  pl.BlockSpec((D, F2), lambda i: (0, 0)),
                  pl.BlockSpec((F2 // 2, D), lambda i: (0, 0))],
        out_specs=pl.BlockSpec((tm, D), lambda i: (i, 0)),
        out_shape=jax.ShapeDtypeStruct((T, D), F32),
        compiler_params=_cparams("parallel"),
        name="moe_shared_expert",
    )(h32, wgu, wd)


def _combine_kernel(slots_ref, next_slots_ref, w_ref, ysh_ref, y_hbm, o_ref, bufs, sems,
                    *, tc, ch, nsteps):
    i = pl.program_id(0)
    slot = i % 2

    def row_copy(table, which, t, k):
        src = pl.ds(pl.multiple_of(table[k, t], ch), ch)
        dst = pl.ds(pl.multiple_of(t * ch, ch), ch)
        return pltpu.make_async_copy(y_hbm.at[src], bufs.at[which, k, dst], sems.at[which])

    def gather(table, which):
        def start_rows(t, c):
            for k in range(TOP_K):
                row_copy(table, which, t, k).start(priority=k % 2)
            return c

        lax.fori_loop(0, tc, start_rows, 0)

    @pl.when(i == 0)
    def _():
        gather(slots_ref, 0)

    @pl.when(i + 1 < nsteps)
    def _():
        gather(next_slots_ref, 1 - slot)

    def wait_rows(t, c):
        for k in range(TOP_K):
            row_copy(slots_ref, slot, t, k).wait()
        return c

    lax.fori_loop(0, tc, wait_rows, 0)

    buf = bufs.at[slot]
    half = ch * LANES
    w_cols = w_ref[...].T
    gates = [w_cols[:, k:k + 1] for k in range(TOP_K)]
    for c in range(ch):
        lanes_lo = slice(c * LANES, (c + 1) * LANES)
        lanes_hi = slice(half + c * LANES, half + (c + 1) * LANES)
        acc_lo = ysh_ref[:, lanes_lo]
        acc_hi = ysh_ref[:, lanes_hi]
        for k in range(TOP_K):
            lo, hi = _unpack_pair_f32(buf.at[k][pl.ds(c, tc, stride=ch), :])
            acc_lo = acc_lo + gates[k] * lo
            acc_hi = acc_hi + gates[k] * hi
        o_ref[:, lanes_lo] = acc_lo
        o_ref[:, lanes_hi] = acc_hi


def _combine(slot_rows, wts, ysh, ys, ch):
    T, D = ysh.shape
    tc = _tile(COMBINE_TC, T)
    nsteps = T // tc
    return pl.pallas_call(
        functools.partial(_combine_kernel, tc=tc, ch=ch, nsteps=nsteps),
        grid=(nsteps,),
        in_specs=[pl.BlockSpec((TOP_K, tc), lambda i: (0, i), memory_space=pltpu.SMEM),
                  pl.BlockSpec((TOP_K, tc), lambda i: (0, jnp.minimum(i + 1, nsteps - 1)),
                               memory_space=pltpu.SMEM),
                  pl.BlockSpec((TOP_K, tc), lambda i: (0, i)),
                  pl.BlockSpec((tc, D), lambda i: (i, 0)),
                  pl.BlockSpec(memory_space=pl.ANY)],
        out_specs=pl.BlockSpec((tc, D), lambda i: (i, 0)),
        out_shape=jax.ShapeDtypeStruct((T, D), F32),
        scratch_shapes=[pltpu.VMEM((2, TOP_K, tc * ch, LANES), U32),
                        pltpu.SemaphoreType.DMA((2,))],
        compiler_params=_cparams("arbitrary"),
        name="moe_combine",
    )(slot_rows, slot_rows, wts, ysh, ys)


def _moe(h32, layer, rw_t, rb, w_gate, w_up, w_down, sh_gu, sh_d):
    E, D = rw_t.shape
    ch = D // 2 // LANES
    assert ch % 8 == 0, "a token slab must be whole (8,128) tiles"
    T = h32.shape[0] // ch
    tm_e = _tile(EXPERT_TM, T)
    eidx, rank, wts, counts = _router(h32, rw_t, rb)

    counts = counts[:, 0]
    tiles = (counts + tm_e - 1) // tm_e
    tile_end = jnp.cumsum(tiles)
    tile_start = tile_end - tiles
    n_tiles = tile_end[-1]
    experts = jnp.arange(E, dtype=I32)
    first_row = jnp.sum(jnp.where(eidx[:, :, None] == experts, tile_start * tm_e, 0), axis=-1)
    slot_rows = (first_row + rank) * ch
    max_tiles = T * TOP_K // tm_e + E
    tile_ids = jnp.minimum(jnp.arange(max_tiles, dtype=I32), n_tiles - 1)
    tile_expert = jnp.sum((tile_ids[:, None] >= tile_end[None, :]).astype(I32), axis=-1)
    last_rows = jnp.where(tiles > 0, (tile_end - 1) * (tm_e * ch), -1).astype(I32)

    xs = _dispatch(h32, slot_rows, last_rows, max_tiles * tm_e, tm_e, ch)
    ys = _experts(xs, layer, w_gate, w_up, w_down, tile_expert,
                  n_tiles.reshape(1).astype(I32), tm_e, ch)
    ysh = _shared_expert(h32, sh_gu, sh_d, ch)
    return _combine(slot_rows, wts, ysh, ys, ch)


def kernel(x, c, positions, ada_w, ada_b, ln_g, ln_b, a_w_qkv, a_w_o, kv_ada_w, kv_ada_b, b_w_kv, b_w_q, b_w_o, router_w, router_b, w_gate, w_up, w_down, sh_gate, sh_up, sh_down):
    B, S, D = x.shape
    T = B * S
    assert D % HEAD_DIM == 0 and DEPTH == ada_w.shape[0] == 2
    xf = x.reshape(T, D)

    c8 = jnp.zeros((8, D), F32).at[:B].set(c)
    mods = _adaln(c8, ada_w, ada_b)[:, :B].reshape(DEPTH, B, 6, D)
    kv_mods = _adaln(c8, kv_ada_w[None], kv_ada_b[None])[0, :B].reshape(B, 2, D)
    sh_a, sc_a, g_a, sh_m, sc_m, g_m = (mods[:, :, r] for r in range(6))
    rope = _rope_tables(positions)

    def moe_weights(l):
        sh_gu = jnp.concatenate([sh_gate[l], sh_up[l]], axis=-1).astype(BF16)
        return (l, router_w[l].T.astype(BF16), router_b[l].reshape(-1, 1), w_gate, w_up, w_down,
                sh_gu, sh_down[l].astype(BF16))

    h = _modulate(xf, _mod_rows(None, [(sh_a[0], sc_a[0])]), S)
    qkv_views = _mm(h, a_w_qkv[0].astype(BF16), B=B, S=S, head_major_out=True, rope=rope,
                    n_rope_cols=2 * D, dilations=DILATIONS[1:])
    o = _dilated_attention(*qkv_views)
    y = _mm(o, a_w_o[0].astype(BF16), B=B, S=S)
    xf, h32 = _ln(xf, y, _mod_rows(g_a[0], [(sh_m[0], sc_m[0])]), ln_g[0, 0], ln_b[0, 0],
                  ["u32"], S)
    y = _moe(h32, *moe_weights(0))
    xf, h_kv, h_q = _ln(xf, y, _mod_rows(g_m[0], [(kv_mods[:, 0], kv_mods[:, 1]),
                                                  (sh_a[1], sc_a[1])]),
                        ln_g[0, 1], ln_b[0, 1], ["bf16", "bf16"], S)

    kv = _mm(h_kv, b_w_kv.astype(BF16), B=B, S=S, head_major_out=True)
    q = _mm(h_q, b_w_q[0].astype(BF16), B=B, S=S, head_major_out=True)
    o = _stick_breaking_attention(q, kv)
    y = _mm(o, b_w_o[0].astype(BF16), B=B, S=S)
    xf, h32 = _ln(xf, y, _mod_rows(g_a[1], [(sh_m[1], sc_m[1])]), ln_g[1, 0], ln_b[1, 0],
                  ["u32"], S)
    y = _moe(h32, *moe_weights(1))
    (xf,) = _ln(xf, y, _mod_rows(g_m[1], []), ln_g[1, 1], ln_b[1, 1], [], S)
    return xf.reshape(B, S, D)
```

```python
import functools
import math

import jax
import jax.numpy as jnp
from jax import lax
from jax.experimental import pallas as pl
from jax.experimental.pallas import tpu as pltpu

F32 = jnp.float32
BF16 = jnp.bfloat16
U32 = jnp.uint32
I32 = jnp.int32

HEAD_DIM = 128
ROT_DIM = HEAD_DIM // 4
ROPE_THETA = 500000.0
DILATED_BRANCHES = ((128, 1), (512, 4), (2048, 16))
DILATIONS = tuple(d for _, d in DILATED_BRANCHES)
BLOCK = 128
TOP_K = 8
N_GROUPS = 8
TOPK_GROUPS = 4
ROUTED_SCALE = 2.5
DEPTH = 2
DEEPNORM_ALPHA = (2 * DEPTH) ** 0.25
LN_EPS = 1e-5

LANES = 128
VMEM_LIMIT_BYTES = 56 * 1024 * 1024

MM_TM = 1024
MM_TN = 512
LN_TM = 256
ROUTER_TM = 512
EXPERT_TM = 256
DISPATCH_TD = 128
COMBINE_TC = 128
SUPER = 2048
DIL_GROUP = 8
SB_TQ = 128
SB_GROUP = 8
SB_SKIP = 127.0
LOG2_E = 1.4426950408889634

NEG_INF = float("-inf")
HI_MASK = 0xFFFF0000


def _tile(pref, dim):
    t = min(pref, dim)
    assert dim % t == 0, (pref, dim)
    return t


def _cparams(*sem):
    return pltpu.CompilerParams(dimension_semantics=sem, vmem_limit_bytes=VMEM_LIMIT_BYTES)


def _dot(a, b):
    return jnp.dot(a, b, preferred_element_type=F32)


def _dot_nt(a, b):
    return lax.dot_general(a, b, (((1,), (1,)), ((), ())), preferred_element_type=F32)


def _unpack_pair(w):
    lo = lax.bitcast_convert_type(w << 16, F32).astype(BF16)
    hi = lax.bitcast_convert_type(w & jnp.uint32(HI_MASK), F32).astype(BF16)
    return lo, hi


def _unpack_pair_f32(w):
    lo = lax.bitcast_convert_type(w << 16, F32)
    hi = lax.bitcast_convert_type(w & jnp.uint32(HI_MASK), F32)
    return lo, hi


def _pack_pair(lo, hi):
    lo_b = lax.bitcast_convert_type(lo.astype(BF16).astype(F32), U32)
    hi_b = lax.bitcast_convert_type(hi.astype(BF16).astype(F32), U32)
    return (hi_b & jnp.uint32(HI_MASK)) | (lo_b >> 16)


def _load_slab_words(ref, n_tok):
    ch = ref.shape[0] // n_tok
    return jnp.concatenate([ref[pl.ds(c, n_tok, stride=ch), :] for c in range(ch)], axis=1)


def _store_slab_words(ref, words):
    n_tok = words.shape[0]
    ch = ref.shape[0] // n_tok
    for c in range(ch):
        ref[pl.ds(c, n_tok, stride=ch), :] = words[:, c * LANES:(c + 1) * LANES]


def _adaln_kernel(c_ref, w_ref, b_ref, o_ref, *, nk):
    k = pl.program_id(2)

    @pl.when(k == 0)
    def _():
        o_ref[...] = jnp.zeros_like(o_ref)

    cond = jax.nn.silu(c_ref[...])
    o_ref[0] += jnp.dot(cond, w_ref[0], preferred_element_type=F32,
                        precision=lax.Precision.HIGHEST)

    @pl.when(k == nk - 1)
    def _():
        o_ref[0] += b_ref[0]


def _adaln(c8, w, b):
    L, D, N = w.shape
    tk = _tile(1024, D)
    tn = _tile(2048, N)
    nk = D // tk
    return pl.pallas_call(
        functools.partial(_adaln_kernel, nk=nk),
        grid=(L, N // tn, nk),
        in_specs=[pl.BlockSpec((8, tk), lambda l, j, k: (0, k)),
                  pl.BlockSpec((1, tk, tn), lambda l, j, k: (l, k, j)),
                  pl.BlockSpec((1, 1, tn), lambda l, j, k: (l, 0, j))],
        out_specs=pl.BlockSpec((1, 8, tn), lambda l, j, k: (l, 0, j)),
        out_shape=jax.ShapeDtypeStruct((L, 8, N), F32),
        compiler_params=_cparams("parallel", "parallel", "arbitrary"),
        name="adaln",
    )(c8, w, b.reshape(L, 1, N))


def _rope_kernel(pos_ref, freq_ref, sign_ref, cos_ref, sin_ref):
    ang = pos_ref[...] * freq_ref[...]
    cos_ref[...] = jnp.cos(ang)
    sin_ref[...] = jnp.sin(ang) * sign_ref[...]


def _rope_tables(positions):
    T = positions.size
    half = ROT_DIM // 2
    inv_freq = ROPE_THETA ** (-jnp.arange(0, ROT_DIM, 2, dtype=F32) / ROT_DIM)
    freq = jnp.concatenate([inv_freq, inv_freq, jnp.zeros((HEAD_DIM - ROT_DIM,), F32)])[None, :]
    sign = jnp.concatenate([-jnp.ones((half,), F32), jnp.ones((HEAD_DIM - half,), F32)])[None, :]
    pos = jnp.broadcast_to(positions.astype(F32).reshape(T, 1), (T, HEAD_DIM))
    tm = _tile(2048, T)
    row = pl.BlockSpec((tm, HEAD_DIM), lambda i: (i, 0))
    vec = pl.BlockSpec((1, HEAD_DIM), lambda i: (0, 0))
    return pl.pallas_call(
        _rope_kernel,
        grid=(T // tm,),
        in_specs=[row, vec, vec],
        out_specs=[row, row],
        out_shape=[jax.ShapeDtypeStruct((T, HEAD_DIM), F32)] * 2,
        compiler_params=_cparams("parallel"),
        name="rope_tables",
    )(pos, freq, sign)


def _emit_mod(xn, rows_ref, idx, fmt, out_ref):
    shift = rows_ref[0, 1 + 2 * idx:2 + 2 * idx, :]
    scale = rows_ref[0, 2 + 2 * idx:3 + 2 * idx, :]
    h = xn * (1.0 + scale) + shift
    if fmt == "bf16":
        out_ref[...] = h.astype(BF16)
    else:
        half = h.shape[1] // 2
        _store_slab_words(out_ref, _pack_pair(h[:, :half], h[:, half:]))


def _modulate_kernel(x_ref, rows_ref, o_ref):
    _emit_mod(x_ref[...], rows_ref, 0, "bf16", o_ref)


def _ln_kernel(x_ref, y_ref, rows_ref, g_ref, b_ref, xo_ref, *h_refs, fmts):
    gate = rows_ref[0, 0:1, :]
    z = DEEPNORM_ALPHA * x_ref[...] + (1.0 + gate) * y_ref[...]
    mu = jnp.mean(z, axis=-1, keepdims=True)
    zc = z - mu
    var = jnp.mean(zc * zc, axis=-1, keepdims=True)
    xn = zc * lax.rsqrt(var + LN_EPS) * g_ref[...] + b_ref[...]
    xo_ref[...] = xn
    for idx, (fmt, h_ref) in enumerate(zip(fmts, h_refs)):
        _emit_mod(xn, rows_ref, idx, fmt, h_ref)


def _mod_rows(gate, pairs):
    B, D = pairs[0][0].shape if pairs else gate.shape
    rows = [gate if gate is not None else jnp.zeros((B, D), F32)]
    for shift, scale in pairs:
        rows += [shift, scale]
    rows += [jnp.zeros((B, D), F32)] * (8 - len(rows))
    return jnp.stack(rows, axis=1)


def _out_struct(T, D, tm, fmt):
    if fmt == "bf16":
        return jax.ShapeDtypeStruct((T, D), BF16), (tm, D)
    ch = D // 2 // LANES
    return jax.ShapeDtypeStruct((T * ch, LANES), U32), (tm * ch, LANES)


def _modulate(x, rows, S):
    T, D = x.shape
    tm = _tile(LN_TM, S)
    nb = S // tm
    return pl.pallas_call(
        _modulate_kernel,
        grid=(T // tm,),
        in_specs=[pl.BlockSpec((tm, D), lambda i: (i, 0)),
                  pl.BlockSpec((1, 8, D), lambda i: (i // nb, 0, 0))],
        out_specs=pl.BlockSpec((tm, D), lambda i: (i, 0)),
        out_shape=jax.ShapeDtypeStruct((T, D), BF16),
        compiler_params=_cparams("parallel"),
        name="modulate",
    )(x, rows)


def _ln(x, y, rows, g, b, fmts, S):
    T, D = x.shape
    tm = _tile(LN_TM, S)
    nb = S // tm
    row = pl.BlockSpec((tm, D), lambda i: (i, 0))
    vec = pl.BlockSpec((1, D), lambda i: (0, 0))
    out_shape = [jax.ShapeDtypeStruct((T, D), F32)]
    out_specs = [row]
    for fmt in fmts:
        st, blk = _out_struct(T, D, tm, fmt)
        out_shape.append(st)
        out_specs.append(pl.BlockSpec(blk, lambda i: (i, 0)))
    return pl.pallas_call(
        functools.partial(_ln_kernel, fmts=tuple(fmts)),
        grid=(T // tm,),
        in_specs=[row, row, pl.BlockSpec((1, 8, D), lambda i: (i // nb, 0, 0)), vec, vec],
        out_specs=out_specs,
        out_shape=out_shape,
        compiler_params=_cparams("parallel"),
        name="deepnorm_ln",
    )(x, y, rows, g.reshape(1, D), b.reshape(1, D))


def _swap_halves(x):
    half = ROT_DIM // 2
    lane = lax.broadcasted_iota(I32, x.shape, 1)
    return jnp.where(lane < half, pltpu.roll(x, HEAD_DIM - half, 1), pltpu.roll(x, half, 1))


def _mm_kernel(*refs, a_heads, out_heads, n_rope_tiles, dilations):
    a_ref, w_ref = refs[:2]
    refs = refs[2:]
    if n_rope_tiles:
        cos_ref, sin_ref = refs[:2]
        refs = refs[2:]
    o_ref = refs[0]
    view_refs = refs[1:1 + len(dilations)]
    stage = refs[1 + len(dilations)] if dilations else None
    if a_heads:
        a = jnp.concatenate([a_ref[0, h] for h in range(a_heads)], axis=-1)
    else:
        a = a_ref[...]
    if not out_heads:
        o_ref[...] = _dot(a, w_ref[...])
        return
    tm = a.shape[0]
    heads_per_dot = min(out_heads, MM_TN // HEAD_DIM)

    def write(rot):
        for h0 in range(0, out_heads, heads_per_dot):
            acc = _dot(a, w_ref[:, h0 * HEAD_DIM:(h0 + heads_per_dot) * HEAD_DIM])
            for hh in range(h0, h0 + heads_per_dot):
                xh = acc[:, (hh - h0) * HEAD_DIM:(hh - h0 + 1) * HEAD_DIM]
                if rot:
                    xh = xh * cos_ref[...] + _swap_halves(xh) * sin_ref[...]
                o_ref[0, 0, hh] = xh.astype(BF16)
                if dilations:
                    stage[hh] = xh
                    for d, v_ref in zip(dilations, view_refs):
                        for r in range(d):
                            v_ref[0, 0, hh, :, r * HEAD_DIM:(r + 1) * HEAD_DIM] = (
                                stage[hh, pl.ds(r, tm // d, stride=d), :].astype(BF16))

    if not n_rope_tiles:
        write(False)
    else:
        j = pl.program_id(1)

        @pl.when(j < n_rope_tiles)
        def _():
            write(True)

        @pl.when(j >= n_rope_tiles)
        def _():
            write(False)


def _mm(a, w, *, B, S, head_major_out=False, rope=None, n_rope_cols=0, dilations=()):
    K, N = w.shape
    T = B * S
    H = K // HEAD_DIM
    tm = _tile(MM_TM // 2 if dilations else MM_TM, S)
    tn = _tile(2 * MM_TN if dilations else MM_TN, N)
    nsb = S // tm
    a_heads = H if a.ndim == 4 else 0
    if a_heads:
        a_spec = pl.BlockSpec((1, H, tm, HEAD_DIM), lambda i, j: (i // nsb, 0, i % nsb, 0))
    else:
        a_spec = pl.BlockSpec((tm, K), lambda i, j: (i, 0))
    in_specs = [a_spec, pl.BlockSpec((K, tn), lambda i, j: (0, j))]
    args = [a, w]
    out_heads = tn // HEAD_DIM if head_major_out else 0
    n_rope_tiles = 0
    if head_major_out:
        D = H * HEAD_DIM
        G = N // D
        tiles_per_group = D // tn
        if rope is not None:
            assert n_rope_cols % tn == 0
            n_rope_tiles = n_rope_cols // tn
            tab = pl.BlockSpec((tm, HEAD_DIM), lambda i, j: (i, 0))
            in_specs += [tab, tab]
            args += list(rope)
        head_index = lambda i, j: (j // tiles_per_group, i // nsb, j % tiles_per_group, i % nsb, 0)
        out_spec = [pl.BlockSpec((1, 1, out_heads, tm // d, d * HEAD_DIM), head_index)
                    for d in (1,) + tuple(dilations)]
        out_shape = [jax.ShapeDtypeStruct((G, B, H, S // d, d * HEAD_DIM), BF16)
                     for d in (1,) + tuple(dilations)]
    else:
        out_spec = pl.BlockSpec((tm, tn), lambda i, j: (i, j))
        out_shape = jax.ShapeDtypeStruct((T, N), F32)
    scratch = [pltpu.VMEM((out_heads, tm, HEAD_DIM), F32)] if dilations else []
    out = pl.pallas_call(
        functools.partial(_mm_kernel, a_heads=a_heads, out_heads=out_heads,
                          n_rope_tiles=n_rope_tiles, dilations=tuple(dilations)),
        grid=(T // tm, N // tn),
        in_specs=in_specs,
        out_specs=out_spec,
        out_shape=out_shape,
        scratch_shapes=scratch,
        compiler_params=_cparams("parallel", "parallel"),
        name="proj",
    )(*args)
    if head_major_out and not dilations:
        return out[0]
    return out


def _dil_attn_kernel(q1, q4, q16, k1c, k4c, k16c, k1p, k4p, k16p, v1c, v4c, v16c, v1p, v4p, v16p,
                     o_ref, *scratch, scale):
    acc_s, m_s, l_s = scratch[0:4], scratch[4:8], scratch[8:12]
    bias_s, s_s, p_s, mu_s, st_acc, st_m, st_l, out_s = scratch[12:20]

    qi = lax.broadcasted_iota(I32, (BLOCK, 2 * BLOCK), 0)
    kc = lax.broadcasted_iota(I32, (BLOCK, 2 * BLOCK), 1)
    band = jnp.where(kc >= qi, jnp.where(kc <= qi + BLOCK, 0.0, NEG_INF), NEG_INF).astype(F32)
    bias_s[0] = band
    no_prev = jnp.where(pl.program_id(2) == 0, NEG_INF, 0.0).astype(F32)
    bias_s[1] = jnp.where(kc < BLOCK, band + no_prev, band)
    ones = jnp.ones((2 * BLOCK, HEAD_DIM), BF16)

    def run(units):
        for g, (q, kp, kc_, _, _, first, _) in enumerate(units):
            keys = jnp.concatenate([kp(), kc_()], axis=0)
            s_s[g] = _dot_nt(q(), keys) * scale + bias_s[1 if first else 0]
        for g in range(len(units)):
            s = s_s[g]
            m = jnp.max(jnp.maximum(s[:, :BLOCK], s[:, BLOCK:]), axis=1, keepdims=True)
            p_s[g] = jnp.exp(s - m).astype(BF16)
            mu_s[g] = jnp.broadcast_to(m, (BLOCK, HEAD_DIM))
        for g, (_, _, _, vp, vc, _, sink) in enumerate(units):
            p = p_s[g]
            vals = jnp.concatenate([vp(), vc()], axis=0)
            sink(g, _dot(p, vals), mu_s[g], _dot(p, ones))

    def fold(slab, rows, acc_u, m_u, l_u):
        m_o = m_s[slab][rows, :]
        m_n = jnp.maximum(m_o, m_u)
        e_o = jnp.exp(m_o - m_n)
        e_u = jnp.exp(m_u - m_n)
        acc_s[slab][rows, :] = acc_s[slab][rows, :] * e_o + acc_u * e_u
        l_s[slab][rows, :] = l_s[slab][rows, :] * e_o + l_u * e_u
        m_s[slab][rows, :] = m_n

    def tile(ref, rows, lanes):
        return lambda: ref[0, 0, 0, rows, lanes]

    def banded(q, kc_, kp, vc, vp, n, lanes, sink):
        cur = slice(n * BLOCK, (n + 1) * BLOCK)
        if n == 0:
            k_prev, v_prev = tile(kp, slice(None), lanes), tile(vp, slice(None), lanes)
        else:
            prv = slice((n - 1) * BLOCK, n * BLOCK)
            k_prev, v_prev = tile(kc_, prv, lanes), tile(vc, prv, lanes)
        return (tile(q, cur, lanes), k_prev, tile(kc_, cur, lanes), v_prev, tile(vc, cur, lanes),
                n == 0, sink)

    units = []
    for n in range(SUPER // 4 // BLOCK):
        for r in range(4):
            def init(g, acc, m, l, r=r, n=n):
                rows = slice(n * BLOCK, (n + 1) * BLOCK)
                acc_s[r][rows, :], m_s[r][rows, :], l_s[r][rows, :] = acc, m, l

            units.append(banded(q4, k4c, k4p, v4c, v4p, n,
                                slice(r * HEAD_DIM, (r + 1) * HEAD_DIM), init))

    for n in range(SUPER // BLOCK):
        def fold1(g, acc, m, l, n=n):
            st_acc[g], st_m[g], st_l[g] = acc, m, l
            rows = slice(n * (BLOCK // 4), (n + 1) * (BLOCK // 4))
            for r in range(4):
                pick = pl.ds(r, BLOCK // 4, stride=4)
                fold(r, rows, st_acc[g, pick, :], st_m[g, pick, :], st_l[g, pick, :])

        units.append(banded(q1, k1c, k1p, v1c, v1p, n, slice(None), fold1))

    for r in range(16):
        def fold16(g, acc, m, l, r=r):
            fold(r % 4, pl.ds(r // 4, BLOCK, stride=4), acc, m, l)

        units.append(banded(q16, k16c, k16p, v16c, v16p, 0,
                            slice(r * HEAD_DIM, (r + 1) * HEAD_DIM), fold16))

    for start in range(0, len(units), DIL_GROUP):
        run(units[start:start + DIL_GROUP])

    for r in range(4):
        out_s[pl.ds(r, SUPER // 4, stride=4), :] = acc_s[r][...] / l_s[r][...]
    o_ref[0, 0] = out_s[...].astype(BF16)


def _dilated_attention(qkv1, qkv4, qkv16):
    _, B, H, S, Dh = qkv1.shape
    assert Dh == HEAD_DIM and S % SUPER == 0
    assert tuple(d for _, d in DILATED_BRANCHES) == DILATIONS
    assert all(w // d == BLOCK for w, d in DILATED_BRANCHES)
    views = {1: qkv1, 4: qkv4, 16: qkv16}

    def cur(which, d):
        return pl.BlockSpec((1, 1, 1, SUPER // d, d * Dh), lambda b, h, m: (which, b, h, m, 0))

    def prev(which, d):
        per = SUPER // d // BLOCK
        return pl.BlockSpec((1, 1, 1, BLOCK, d * Dh),
                            lambda b, h, m: (which, b, h, jnp.maximum(m * per - 1, 0), 0))

    in_specs, args = [], []
    for make, which in ((cur, 0), (cur, 1), (prev, 1), (cur, 2), (prev, 2)):
        for d in (1, 4, 16):
            in_specs.append(make(which, d))
            args.append(views[d])
    state = [pltpu.VMEM((SUPER // 4, Dh), F32)] * 12
    stage = [pltpu.VMEM((2, BLOCK, 2 * BLOCK), F32),
             pltpu.VMEM((DIL_GROUP, BLOCK, 2 * BLOCK), F32),
             pltpu.VMEM((DIL_GROUP, BLOCK, 2 * BLOCK), BF16),
             pltpu.VMEM((DIL_GROUP, BLOCK, Dh), F32)] + [
             pltpu.VMEM((DIL_GROUP, BLOCK, Dh), F32)] * 3 + [
             pltpu.VMEM((SUPER, Dh), F32)]
    return pl.pallas_call(
        functools.partial(_dil_attn_kernel, scale=Dh ** -0.5),
        grid=(B, H, S // SUPER),
        in_specs=in_specs,
        out_specs=pl.BlockSpec((1, 1, SUPER, Dh), lambda b, h, m: (b, h, m, 0)),
        out_shape=jax.ShapeDtypeStruct((B, H, S, Dh), BF16),
        scratch_shapes=state + stage,
        compiler_params=_cparams("parallel", "parallel", "parallel"),
        name="dilated_attention",
    )(*args)


def _sb_kernel(q_ref, k_ref, v_ref, o_ref, tri_ref, acc_ref, carry_ref, *, scale, tq, nq):
    row = lax.broadcasted_iota(I32, (tq, tq), 0)
    col = lax.broadcasted_iota(I32, (tq, tq), 1)
    tri_ref[...] = jnp.where(row > col, 1.0, 0.0).astype(BF16)

    def suffix_sum(x):
        hi = x.astype(BF16)
        lo = (x - hi.astype(F32)).astype(BF16)
        return _dot(hi, tri_ref[...]) + _dot(lo, tri_ref[...])

    def logs(q, k):
        z2 = _dot_nt(q, k) * (scale * LOG2_E)
        t = jnp.log2(1.0 + jnp.exp2(-jnp.abs(z2)))
        return jnp.minimum(z2, 0.0) - t, -(jnp.maximum(z2, 0.0) + t)

    def qgroup(it, carry):
        n0 = it * SB_GROUP

        def block_rows(n):
            return pl.ds(pl.multiple_of(n * tq, tq), tq)

        causal = col < row
        tops = []
        for g in range(SB_GROUP):
            rows = block_rows(n0 + g)
            log_b, log_1mb = logs(q_ref[0, 0, rows, :], k_ref[0, 0, 0, rows, :])
            log_1mb = jnp.where(causal, log_1mb, 0.0)
            a = jnp.where(causal, jnp.exp2(log_b + suffix_sum(log_1mb)), 0.0)
            acc_ref[g] = _dot(a.astype(BF16), v_ref[0, 0, 0, rows, :])
            c0 = jnp.broadcast_to(jnp.sum(log_1mb, axis=1, keepdims=True), (tq, tq))
            carry_ref[g] = c0
            tops.append(jnp.where(n0 + g >= 1, jnp.max(c0), NEG_INF))

        def cond(st):
            _, top = st
            return top > -SB_SKIP

        def body(st):
            i, _ = st
            tops = []
            for g in range(SB_GROUP):
                j = n0 + g - i
                live = j >= 0
                krows = block_rows(jnp.maximum(j, 0))
                log_b, log_1mb = logs(q_ref[0, 0, block_rows(n0 + g), :], k_ref[0, 0, 0, krows, :])
                c = carry_ref[g]
                a = jnp.where(live, jnp.exp2(log_b + suffix_sum(log_1mb) + c), 0.0)
                acc_ref[g] += _dot(a.astype(BF16), v_ref[0, 0, 0, krows, :])
                c = c + jnp.broadcast_to(jnp.sum(log_1mb, axis=1, keepdims=True), (tq, tq))
                carry_ref[g] = c
                tops.append(jnp.where(j >= 1, jnp.max(c), NEG_INF))
            return i + 1, functools.reduce(jnp.maximum, tops)

        lax.while_loop(cond, body, (1, functools.reduce(jnp.maximum, tops)))
        for g in range(SB_GROUP):
            o_ref[0, 0, block_rows(n0 + g), :] = acc_ref[g].astype(BF16)
        return carry

    lax.fori_loop(0, nq // SB_GROUP, qgroup, 0)


def _stick_breaking_attention(q, kv):
    _, B, H, S, Dh = q.shape
    tq = _tile(SB_TQ, S)
    seq = lambda which: pl.BlockSpec((1, 1, 1, S, Dh), lambda b, h: (which, b, h, 0, 0))
    return pl.pallas_call(
        functools.partial(_sb_kernel, scale=Dh ** -0.5, tq=tq, nq=S // tq),
        grid=(B, H),
        in_specs=[pl.BlockSpec((1, 1, S, Dh), lambda b, h: (b, h, 0, 0)), seq(0), seq(1)],
        out_specs=pl.BlockSpec((1, 1, S, Dh), lambda b, h: (b, h, 0, 0)),
        out_shape=jax.ShapeDtypeStruct((B, H, S, Dh), BF16),
        scratch_shapes=[pltpu.VMEM((tq, tq), BF16), pltpu.VMEM((SB_GROUP, tq, Dh), F32),
                        pltpu.VMEM((SB_GROUP, tq, tq), F32)],
        compiler_params=_cparams("parallel", "parallel"),
        name="stick_breaking_attention",
    )(q.reshape(B, H, S, Dh), kv, kv)


def _router_kernel(h_ref, rw_ref, rb_ref, e_ref, r_ref, w_ref, cnt_ref, tri_ref, carry_ref,
                   *, n_exp, tm, nsteps):
    i = pl.program_id(0)
    gsz = n_exp // N_GROUPS

    @pl.when(i == 0)
    def _():
        row = lax.broadcasted_iota(I32, (tm, tm), 0)
        col = lax.broadcasted_iota(I32, (tm, tm), 1)
        tri_ref[...] = jnp.where(row < col, 1.0, 0.0).astype(BF16)
        carry_ref[...] = jnp.zeros_like(carry_ref)

    lo, hi = _unpack_pair(_load_slab_words(h_ref, tm))
    half = lo.shape[1]
    logits = _dot_nt(rw_ref[:, :half], lo) + _dot_nt(rw_ref[:, half:], hi)
    scores = jax.nn.sigmoid(logits)
    sel = scores + rb_ref[...]

    sub = lax.broadcasted_iota(I32, (gsz, tm), 0)
    groups, gscore = [], []
    for g in range(N_GROUPS):
        sg = sel[g * gsz:(g + 1) * gsz, :]
        m1 = jnp.max(sg, axis=0, keepdims=True)
        i1 = jnp.min(jnp.where(sg == m1, sub, gsz), axis=0, keepdims=True)
        m2 = jnp.max(jnp.where(sub == i1, NEG_INF, sg), axis=0, keepdims=True)
        groups.append(sg)
        gscore.append(m1 + m2)
    masked = []
    for g in range(N_GROUPS):
        beat = jnp.zeros((1, tm), F32)
        for o in range(N_GROUPS):
            if o == g:
                continue
            beat += jnp.where(gscore[o] > gscore[g], 1.0, 0.0)
            if o < g:
                beat += jnp.where(gscore[o] == gscore[g], 1.0, 0.0)
        masked.append(jnp.where(beat < TOPK_GROUPS, groups[g], NEG_INF))
    x = jnp.concatenate(masked, axis=0)

    eid = lax.broadcasted_iota(I32, (n_exp, tm), 0)
    chosen = jnp.zeros((n_exp, tm), F32)
    picks, vals = [], []
    for _ in range(TOP_K):
        mx = jnp.max(x, axis=0, keepdims=True)
        idx = jnp.min(jnp.where(x == mx, eid, n_exp), axis=0, keepdims=True)
        hit = eid == idx
        picks.append(idx)
        vals.append(jnp.sum(jnp.where(hit, scores, 0.0), axis=0, keepdims=True))
        chosen += jnp.where(hit, 1.0, 0.0)
        x = jnp.where(hit, NEG_INF, x)
    total = vals[0]
    for v in vals[1:]:
        total = total + v

    rank_all = _dot(chosen.astype(BF16), tri_ref[...]) + carry_ref[...]
    for k in range(TOP_K):
        e_ref[k:k + 1, :] = picks[k]
        w_ref[k:k + 1, :] = vals[k] / total * ROUTED_SCALE
        rk = jnp.sum(jnp.where(eid == picks[k], rank_all, 0.0), axis=0, keepdims=True)
        r_ref[k:k + 1, :] = rk.astype(I32)
    carry_ref[...] += jnp.broadcast_to(jnp.sum(chosen, axis=1, keepdims=True), (n_exp, tm))

    @pl.when(i == nsteps - 1)
    def _():
        cnt_ref[...] = carry_ref[:, 0:LANES].astype(I32)


def _router(h32, rw_t, rb):
    E, D = rw_t.shape
    ch = D // 2 // LANES
    T = h32.shape[0] // ch
    tm = _tile(ROUTER_TM, T)
    nsteps = T // tm
    tok = pl.BlockSpec((TOP_K, tm), lambda i: (0, i))
    return pl.pallas_call(
        functools.partial(_router_kernel, n_exp=E, tm=tm, nsteps=nsteps),
        grid=(nsteps,),
        in_specs=[pl.BlockSpec((tm * ch, LANES), lambda i: (i, 0)),
                  pl.BlockSpec((E, D), lambda i: (0, 0)),
                  pl.BlockSpec((E, 1), lambda i: (0, 0))],
        out_specs=[tok, tok, tok, pl.BlockSpec((E, LANES), lambda i: (0, 0))],
        out_shape=[jax.ShapeDtypeStruct((TOP_K, T), I32), jax.ShapeDtypeStruct((TOP_K, T), I32),
                   jax.ShapeDtypeStruct((TOP_K, T), F32), jax.ShapeDtypeStruct((E, LANES), I32)],
        scratch_shapes=[pltpu.VMEM((tm, tm), BF16), pltpu.VMEM((E, tm), F32)],
        compiler_params=_cparams("arbitrary"),
        name="moe_router",
    )(h32, rw_t, rb)


def _dispatch_kernel(last_ref, slots_ref, h_ref, xs_hbm, zbuf, zsem, sem, *, n_exp, td, ch):
    i = pl.program_id(0)
    tile_rows = zbuf.shape[0]

    def zero_copy(e):
        dst = pl.ds(pl.multiple_of(last_ref[e], tile_rows), tile_rows)
        return pltpu.make_async_copy(zbuf, xs_hbm.at[dst], zsem)

    @pl.when(i == 0)
    def _():
        zbuf[...] = jnp.zeros_like(zbuf)

        def start(e, c):
            @pl.when(last_ref[e] >= 0)
            def _():
                zero_copy(e).start()
            return c

        def wait(e, c):
            @pl.when(last_ref[e] >= 0)
            def _():
                zero_copy(e).wait()
            return c

        lax.fori_loop(0, n_exp, start, 0)
        lax.fori_loop(0, n_exp, wait, 0)

    def row_copy(t, k):
        src = pl.ds(pl.multiple_of(t * ch, ch), ch)
        dst = pl.ds(pl.multiple_of(slots_ref[k, t], ch), ch)
        return pltpu.make_async_copy(h_ref.at[src], xs_hbm.at[dst], sem)

    def start_rows(t, c):
        for k in range(TOP_K):
            row_copy(t, k).start(priority=k % 2)
        return c

    def wait_rows(t, c):
        for k in range(TOP_K):
            row_copy(t, k).wait()
        return c

    lax.fori_loop(0, td, start_rows, 0)
    lax.fori_loop(0, td, wait_rows, 0)


def _dispatch(h32, slot_rows, last_rows, n_slots, tm_e, ch):
    T = slot_rows.shape[1]
    td = _tile(DISPATCH_TD, T)
    E = last_rows.shape[0]
    return pl.pallas_call(
        functools.partial(_dispatch_kernel, n_exp=E, td=td, ch=ch),
        grid_spec=pltpu.PrefetchScalarGridSpec(
            num_scalar_prefetch=1,
            grid=(T // td,),
            in_specs=[pl.BlockSpec((TOP_K, td), lambda i, last: (0, i), memory_space=pltpu.SMEM),
                      pl.BlockSpec((td * ch, LANES), lambda i, last: (i, 0))],
            out_specs=pl.BlockSpec(memory_space=pl.ANY),
            scratch_shapes=[pltpu.VMEM((tm_e * ch, LANES), U32), pltpu.SemaphoreType.DMA,
                            pltpu.SemaphoreType.DMA]),
        out_shape=jax.ShapeDtypeStruct((n_slots * ch, LANES), U32),
        compiler_params=_cparams("arbitrary"),
        name="moe_dispatch",
    )(last_rows, slot_rows, h32)


def _mlp_hidden(x_ref, wgu_ref, n_tok):
    f = wgu_ref.shape[1] // 2
    lo, hi = _unpack_pair(_load_slab_words(x_ref, n_tok))
    half = lo.shape[1]
    gu = _dot(lo, wgu_ref[:half, :]) + _dot(hi, wgu_ref[half:, :])
    return (jax.nn.silu(gu[:, :f]) * gu[:, f:]).astype(BF16)


def _expert_kernel(te_ref, nt_ref, x_ref, wg_ref, wu_ref, wd_ref, y_ref, wgu_s, wd_s, *, tm_e):
    i = pl.program_id(0)
    f, d_model = wd_s.shape
    half = d_model // 2
    ch = half // LANES

    @pl.when(i < nt_ref[0])
    def _():
        @pl.when(jnp.logical_or(i == 0, te_ref[i] != te_ref[jnp.maximum(i - 1, 0)]))
        def _():
            wgu_s[:, :f] = wg_ref[0, 0].astype(BF16)
            wgu_s[:, f:] = wu_ref[0, 0].astype(BF16)
            for c in range(ch):
                lo_cols = slice(c * LANES, (c + 1) * LANES)
                hi_cols = slice(half + c * LANES, half + (c + 1) * LANES)
                wd_s[:, 2 * c * LANES:(2 * c + 1) * LANES] = wd_ref[0, 0, :, lo_cols].astype(BF16)
                wd_s[:, (2 * c + 1) * LANES:(2 * c + 2) * LANES] = wd_ref[0, 0, :, hi_cols].astype(BF16)

        a = _mlp_hidden(x_ref, wgu_s, tm_e)
        for c in range(ch):
            yc = _dot(a, wd_s[:, 2 * c * LANES:(2 * c + 2) * LANES])
            y_ref[pl.ds(c, tm_e, stride=ch), :] = _pack_pair(yc[:, :LANES], yc[:, LANES:])


def _shared_kernel(x_ref, wgu_ref, wd_ref, y_ref, *, tm):
    y_ref[...] = _dot(_mlp_hidden(x_ref, wgu_ref, tm), wd_ref[...])


def _experts(xs, layer, w_gate, w_up, w_down, tile_expert, n_tiles, tm_e, ch):
    _, E, D, F = w_gate.shape
    n_slots = xs.shape[0] // ch
    rows = pl.BlockSpec((tm_e * ch, LANES), lambda i, te, nt: (jnp.minimum(i, nt[0] - 1), 0))
    w_in = pl.BlockSpec((1, 1, D, F), lambda i, te, nt: (layer, te[i], 0, 0))
    return pl.pallas_call(
        functools.partial(_expert_kernel, tm_e=tm_e),
        grid_spec=pltpu.PrefetchScalarGridSpec(
            num_scalar_prefetch=2,
            grid=(n_slots // tm_e,),
            in_specs=[rows, w_in, w_in,
                      pl.BlockSpec((1, 1, F, D), lambda i, te, nt: (layer, te[i], 0, 0))],
            out_specs=rows,
            scratch_shapes=[pltpu.VMEM((D, 2 * F), BF16), pltpu.VMEM((F, D), BF16)]),
        out_shape=jax.ShapeDtypeStruct(xs.shape, U32),
        compiler_params=_cparams("arbitrary"),
        name="moe_experts",
    )(tile_expert, n_tiles, xs, w_gate, w_up, w_down)


def _shared_expert(h32, wgu, wd, ch):
    D, F2 = wgu.shape
    T = h32.shape[0] // ch
    tm = _tile(512, T)
    return pl.pallas_call(
        functools.partial(_shared_kernel, tm=tm),
        grid=(T // tm,),
        in_specs=[pl.BlockSpec((tm * ch, LANES), lambda i: (i, 0)),
                  pl.BlockSpec((D, F2), lambda i: (0, 0)),
                  pl.BlockSpec((F2 // 2, D), lambda i: (0, 0))],
        out_specs=pl.BlockSpec((tm, D), lambda i: (i, 0)),
        out_shape=jax.ShapeDtypeStruct((T, D), F32),
        compiler_params=_cparams("parallel"),
        name="moe_shared_expert",
    )(h32, wgu, wd)


def _combine_kernel(slots_ref, next_slots_ref, w_ref, ysh_ref, y_hbm, o_ref, bufs, gate_s, sems,
                    *, tc, ch, nsteps):
    i = pl.program_id(0)
    slot = i % 2

    def row_copy(table, which, t, k):
        src = pl.ds(pl.multiple_of(table[k, t], ch), ch)
        dst = pl.ds(pl.multiple_of(t * ch, ch), ch)
        return pltpu.make_async_copy(y_hbm.at[src], bufs.at[which, k, dst], sems.at[which])

    def gather_token(table, which, t):
        for k in range(TOP_K):
            row_copy(table, which, t, k).start(priority=k % 2)

    def wait_all(table, which):
        def wait_rows(t, c):
            for k in range(TOP_K):
                row_copy(table, which, t, k).wait()
            return c

        lax.fori_loop(0, tc, wait_rows, 0)

    @pl.when(i == 0)
    def _():
        def start_rows(t, c):
            gather_token(slots_ref, 0, t)
            return c

        lax.fori_loop(0, tc, start_rows, 0)

    wait_all(slots_ref, slot)

    half = ch * LANES
    w_cols = w_ref[...].T
    for k in range(TOP_K):
        gate_s[k] = jnp.broadcast_to(w_cols[:, k:k + 1], (tc, LANES))
    per_chunk = tc // ch
    for c in range(ch):
        lanes_lo = slice(c * LANES, (c + 1) * LANES)
        lanes_hi = slice(half + c * LANES, half + (c + 1) * LANES)
        acc_lo = ysh_ref[:, lanes_lo]
        acc_hi = ysh_ref[:, lanes_hi]
        for k in range(TOP_K):
            lo, hi = _unpack_pair_f32(bufs.at[slot, k][pl.ds(c, tc, stride=ch), :])
            acc_lo = acc_lo + gate_s[k] * lo
            acc_hi = acc_hi + gate_s[k] * hi
        o_ref[:, lanes_lo] = acc_lo
        o_ref[:, lanes_hi] = acc_hi
        for t in range(c * per_chunk, (c + 1) * per_chunk):
            gather_token(next_slots_ref, 1 - slot, t)

    @pl.when(i == nsteps - 1)
    def _():
        wait_all(next_slots_ref, 1 - slot)


def _combine(slot_rows, wts, ysh, ys, ch):
    T, D = ysh.shape
    tc = _tile(COMBINE_TC, T)
    nsteps = T // tc
    return pl.pallas_call(
        functools.partial(_combine_kernel, tc=tc, ch=ch, nsteps=nsteps),
        grid=(nsteps,),
        in_specs=[pl.BlockSpec((TOP_K, tc), lambda i: (0, i), memory_space=pltpu.SMEM),
                  pl.BlockSpec((TOP_K, tc), lambda i: (0, jnp.minimum(i + 1, nsteps - 1)),
                               memory_space=pltpu.SMEM),
                  pl.BlockSpec((TOP_K, tc), lambda i: (0, i)),
                  pl.BlockSpec((tc, D), lambda i: (i, 0)),
                  pl.BlockSpec(memory_space=pl.ANY)],
        out_specs=pl.BlockSpec((tc, D), lambda i: (i, 0)),
        out_shape=jax.ShapeDtypeStruct((T, D), F32),
        scratch_shapes=[pltpu.VMEM((2, TOP_K, tc * ch, LANES), U32),
                        pltpu.VMEM((TOP_K, tc, LANES), F32),
                        pltpu.SemaphoreType.DMA((2,))],
        compiler_params=_cparams("arbitrary"),
        name="moe_combine",
    )(slot_rows, slot_rows, wts, ysh, ys)


def _moe(h32, layer, rw_t, rb, w_gate, w_up, w_down, sh_gu, sh_d):
    E, D = rw_t.shape
    ch = D // 2 // LANES
    assert ch % 8 == 0, "a token slab must be whole (8,128) tiles"
    T = h32.shape[0] // ch
    tm_e = _tile(EXPERT_TM, T)
    eidx, rank, wts, counts = _router(h32, rw_t, rb)

    counts = counts[:, 0]
    tiles = (counts + tm_e - 1) // tm_e
    tile_end = jnp.cumsum(tiles)
    tile_start = tile_end - tiles
    n_tiles = tile_end[-1]
    experts = jnp.arange(E, dtype=I32)
    first_row = jnp.sum(jnp.where(eidx[:, :, None] == experts, tile_start * tm_e, 0), axis=-1)
    slot_rows = (first_row + rank) * ch
    max_tiles = T * TOP_K // tm_e + E
    tile_ids = jnp.minimum(jnp.arange(max_tiles, dtype=I32), n_tiles - 1)
    tile_expert = jnp.sum((tile_ids[:, None] >= tile_end[None, :]).astype(I32), axis=-1)
    last_rows = jnp.where(tiles > 0, (tile_end - 1) * (tm_e * ch), -1).astype(I32)

    xs = _dispatch(h32, slot_rows, last_rows, max_tiles * tm_e, tm_e, ch)
    ys = _experts(xs, layer, w_gate, w_up, w_down, tile_expert,
                  n_tiles.reshape(1).astype(I32), tm_e, ch)
    ysh = _shared_expert(h32, sh_gu, sh_d, ch)
    return _combine(slot_rows, wts, ysh, ys, ch)


def kernel(x, c, positions, ada_w, ada_b, ln_g, ln_b, a_w_qkv, a_w_o, kv_ada_w, kv_ada_b, b_w_kv, b_w_q, b_w_o, router_w, router_b, w_gate, w_up, w_down, sh_gate, sh_up, sh_down):
    B, S, D = x.shape
    T = B * S
    assert D % HEAD_DIM == 0 and DEPTH == ada_w.shape[0] == 2
    xf = x.reshape(T, D)

    c8 = jnp.zeros((8, D), F32).at[:B].set(c)
    mods = _adaln(c8, ada_w, ada_b)[:, :B].reshape(DEPTH, B, 6, D)
    kv_mods = _adaln(c8, kv_ada_w[None], kv_ada_b[None])[0, :B].reshape(B, 2, D)
    sh_a, sc_a, g_a, sh_m, sc_m, g_m = (mods[:, :, r] for r in range(6))
    rope = _rope_tables(positions)

    def moe_weights(l):
        sh_gu = jnp.concatenate([sh_gate[l], sh_up[l]], axis=-1).astype(BF16)
        return (l, router_w[l].T.astype(BF16), router_b[l].reshape(-1, 1), w_gate, w_up, w_down,
                sh_gu, sh_down[l].astype(BF16))

    h = _modulate(xf, _mod_rows(None, [(sh_a[0], sc_a[0])]), S)
    qkv_views = _mm(h, a_w_qkv[0].astype(BF16), B=B, S=S, head_major_out=True, rope=rope,
                    n_rope_cols=2 * D, dilations=DILATIONS[1:])
    o = _dilated_attention(*qkv_views)
    y = _mm(o, a_w_o[0].astype(BF16), B=B, S=S)
    xf, h32 = _ln(xf, y, _mod_rows(g_a[0], [(sh_m[0], sc_m[0])]), ln_g[0, 0], ln_b[0, 0],
                  ["u32"], S)
    y = _moe(h32, *moe_weights(0))
    xf, h_kv, h_q = _ln(xf, y, _mod_rows(g_m[0], [(kv_mods[:, 0], kv_mods[:, 1]),
                                                  (sh_a[1], sc_a[1])]),
                        ln_g[0, 1], ln_b[0, 1], ["bf16", "bf16"], S)

    kv = _mm(h_kv, b_w_kv.astype(BF16), B=B, S=S, head_major_out=True)
    q = _mm(h_q, b_w_q[0].astype(BF16), B=B, S=S, head_major_out=True)
    o = _stick_breaking_attention(q, kv)
    y = _mm(o, b_w_o[0].astype(BF16), B=B, S=S)
    xf, h32 = _ln(xf, y, _mod_rows(g_a[1], [(sh_m[1], sc_m[1])]), ln_g[1, 0], ln_b[1, 0],
                  ["u32"], S)
    y = _moe(h32, *moe_weights(1))
    (xf,) = _ln(xf, y, _mod_rows(g_m[1], []), ln_g[1, 1], ln_b[1, 1], [], S)
    return xf.reshape(B, S, D)
```

```python
import functools
import math

import jax
import jax.numpy as jnp
from jax import lax
from jax.experimental import pallas as pl
from jax.experimental.pallas import tpu as pltpu

F32 = jnp.float32
BF16 = jnp.bfloat16
U32 = jnp.uint32
I32 = jnp.int32

HEAD_DIM = 128
ROT_DIM = HEAD_DIM // 4
ROPE_THETA = 500000.0
DILATED_BRANCHES = ((128, 1), (512, 4), (2048, 16))
DILATIONS = tuple(d for _, d in DILATED_BRANCHES)
BLOCK = 128
TOP_K = 8
N_GROUPS = 8
TOPK_GROUPS = 4
ROUTED_SCALE = 2.5
DEPTH = 2
DEEPNORM_ALPHA = (2 * DEPTH) ** 0.25
LN_EPS = 1e-5

LANES = 128
VMEM_LIMIT_BYTES = 56 * 1024 * 1024

MM_TM = 1024
MM_TN = 512
LN_TM = 256
ROUTER_TM = 512
EXPERT_TM = 256
DISPATCH_TD = 128
COMBINE_TC = 128
SUPER = 2048
DIL_GROUP = 8
SB_TQ = 128
SB_GROUP = 8
SB_SKIP = 127.0
LOG2_E = 1.4426950408889634

NEG_INF = float("-inf")
HI_MASK = 0xFFFF0000


def _tile(pref, dim):
    t = min(pref, dim)
    assert dim % t == 0, (pref, dim)
    return t


def _cparams(*sem):
    return pltpu.CompilerParams(dimension_semantics=sem, vmem_limit_bytes=VMEM_LIMIT_BYTES)


def _dot(a, b):
    return jnp.dot(a, b, preferred_element_type=F32)


def _dot_nt(a, b):
    return lax.dot_general(a, b, (((1,), (1,)), ((), ())), preferred_element_type=F32)


def _unpack_pair(w):
    lo = lax.bitcast_convert_type(w << 16, F32).astype(BF16)
    hi = lax.bitcast_convert_type(w & jnp.uint32(HI_MASK), F32).astype(BF16)
    return lo, hi


def _unpack_pair_f32(w):
    lo = lax.bitcast_convert_type(w << 16, F32)
    hi = lax.bitcast_convert_type(w & jnp.uint32(HI_MASK), F32)
    return lo, hi


def _pack_pair(lo, hi):
    lo_b = lax.bitcast_convert_type(lo.astype(BF16).astype(F32), U32)
    hi_b = lax.bitcast_convert_type(hi.astype(BF16).astype(F32), U32)
    return (hi_b & jnp.uint32(HI_MASK)) | (lo_b >> 16)


def _load_slab_words(ref, n_tok):
    ch = ref.shape[0] // n_tok
    return jnp.concatenate([ref[pl.ds(c, n_tok, stride=ch), :] for c in range(ch)], axis=1)


def _store_slab_words(ref, words):
    n_tok = words.shape[0]
    ch = ref.shape[0] // n_tok
    for c in range(ch):
        ref[pl.ds(c, n_tok, stride=ch), :] = words[:, c * LANES:(c + 1) * LANES]


def _adaln_kernel(c_ref, w_ref, b_ref, o_ref, *, nk):
    k = pl.program_id(2)

    @pl.when(k == 0)
    def _():
        o_ref[...] = jnp.zeros_like(o_ref)

    cond = jax.nn.silu(c_ref[...])
    o_ref[0] += jnp.dot(cond, w_ref[0], preferred_element_type=F32,
                        precision=lax.Precision.HIGHEST)

    @pl.when(k == nk - 1)
    def _():
        o_ref[0] += b_ref[0]


def _adaln(c8, w, b):
    L, D, N = w.shape
    tk = _tile(1024, D)
    tn = _tile(2048, N)
    nk = D // tk
    return pl.pallas_call(
        functools.partial(_adaln_kernel, nk=nk),
        grid=(L, N // tn, nk),
        in_specs=[pl.BlockSpec((8, tk), lambda l, j, k: (0, k)),
                  pl.BlockSpec((1, tk, tn), lambda l, j, k: (l, k, j)),
                  pl.BlockSpec((1, 1, tn), lambda l, j, k: (l, 0, j))],
        out_specs=pl.BlockSpec((1, 8, tn), lambda l, j, k: (l, 0, j)),
        out_shape=jax.ShapeDtypeStruct((L, 8, N), F32),
        compiler_params=_cparams("parallel", "parallel", "arbitrary"),
        name="adaln",
    )(c8, w, b.reshape(L, 1, N))


def _rope_kernel(pos_ref, freq_ref, sign_ref, cos_ref, sin_ref):
    ang = pos_ref[...] * freq_ref[...]
    cos_ref[...] = jnp.cos(ang)
    sin_ref[...] = jnp.sin(ang) * sign_ref[...]


def _rope_tables(positions):
    T = positions.size
    half = ROT_DIM // 2
    inv_freq = ROPE_THETA ** (-jnp.arange(0, ROT_DIM, 2, dtype=F32) / ROT_DIM)
    freq = jnp.concatenate([inv_freq, inv_freq, jnp.zeros((HEAD_DIM - ROT_DIM,), F32)])[None, :]
    sign = jnp.concatenate([-jnp.ones((half,), F32), jnp.ones((HEAD_DIM - half,), F32)])[None, :]
    pos = jnp.broadcast_to(positions.astype(F32).reshape(T, 1), (T, HEAD_DIM))
    tm = _tile(2048, T)
    row = pl.BlockSpec((tm, HEAD_DIM), lambda i: (i, 0))
    vec = pl.BlockSpec((1, HEAD_DIM), lambda i: (0, 0))
    return pl.pallas_call(
        _rope_kernel,
        grid=(T // tm,),
        in_specs=[row, vec, vec],
        out_specs=[row, row],
        out_shape=[jax.ShapeDtypeStruct((T, HEAD_DIM), F32)] * 2,
        compiler_params=_cparams("parallel"),
        name="rope_tables",
    )(pos, freq, sign)


def _emit_mod(xn, rows_ref, idx, fmt, out_ref):
    shift = rows_ref[0, 1 + 2 * idx:2 + 2 * idx, :]
    scale = rows_ref[0, 2 + 2 * idx:3 + 2 * idx, :]
    h = xn * (1.0 + scale) + shift
    if fmt == "bf16":
        out_ref[...] = h.astype(BF16)
    else:
        half = h.shape[1] // 2
        _store_slab_words(out_ref, _pack_pair(h[:, :half], h[:, half:]))


def _modulate_kernel(x_ref, rows_ref, o_ref):
    _emit_mod(x_ref[...], rows_ref, 0, "bf16", o_ref)


def _ln_kernel(x_ref, y_ref, rows_ref, g_ref, b_ref, xo_ref, *h_refs, fmts):
    gate = rows_ref[0, 0:1, :]
    z = DEEPNORM_ALPHA * x_ref[...] + (1.0 + gate) * y_ref[...]
    mu = jnp.mean(z, axis=-1, keepdims=True)
    zc = z - mu
    var = jnp.mean(zc * zc, axis=-1, keepdims=True)
    xn = zc * lax.rsqrt(var + LN_EPS) * g_ref[...] + b_ref[...]
    xo_ref[...] = xn
    for idx, (fmt, h_ref) in enumerate(zip(fmts, h_refs)):
        _emit_mod(xn, rows_ref, idx, fmt, h_ref)


def _mod_rows(gate, pairs):
    B, D = pairs[0][0].shape if pairs else gate.shape
    rows = [gate if gate is not None else jnp.zeros((B, D), F32)]
    for shift, scale in pairs:
        rows += [shift, scale]
    rows += [jnp.zeros((B, D), F32)] * (8 - len(rows))
    return jnp.stack(rows, axis=1)


def _out_struct(T, D, tm, fmt):
    if fmt == "bf16":
        return jax.ShapeDtypeStruct((T, D), BF16), (tm, D)
    ch = D // 2 // LANES
    return jax.ShapeDtypeStruct((T * ch, LANES), U32), (tm * ch, LANES)


def _modulate(x, rows, S):
    T, D = x.shape
    tm = _tile(LN_TM, S)
    nb = S // tm
    return pl.pallas_call(
        _modulate_kernel,
        grid=(T // tm,),
        in_specs=[pl.BlockSpec((tm, D), lambda i: (i, 0)),
                  pl.BlockSpec((1, 8, D), lambda i: (i // nb, 0, 0))],
        out_specs=pl.BlockSpec((tm, D), lambda i: (i, 0)),
        out_shape=jax.ShapeDtypeStruct((T, D), BF16),
        compiler_params=_cparams("parallel"),
        name="modulate",
    )(x, rows)


def _ln(x, y, rows, g, b, fmts, S):
    T, D = x.shape
    tm = _tile(LN_TM, S)
    nb = S // tm
    row = pl.BlockSpec((tm, D), lambda i: (i, 0))
    vec = pl.BlockSpec((1, D), lambda i: (0, 0))
    out_shape = [jax.ShapeDtypeStruct((T, D), F32)]
    out_specs = [row]
    for fmt in fmts:
        st, blk = _out_struct(T, D, tm, fmt)
        out_shape.append(st)
        out_specs.append(pl.BlockSpec(blk, lambda i: (i, 0)))
    return pl.pallas_call(
        functools.partial(_ln_kernel, fmts=tuple(fmts)),
        grid=(T // tm,),
        in_specs=[row, row, pl.BlockSpec((1, 8, D), lambda i: (i // nb, 0, 0)), vec, vec],
        out_specs=out_specs,
        out_shape=out_shape,
        compiler_params=_cparams("parallel"),
        name="deepnorm_ln",
    )(x, y, rows, g.reshape(1, D), b.reshape(1, D))


def _swap_halves(x):
    half = ROT_DIM // 2
    lane = lax.broadcasted_iota(I32, x.shape, 1)
    return jnp.where(lane < half, pltpu.roll(x, HEAD_DIM - half, 1), pltpu.roll(x, half, 1))


def _mm_kernel(*refs, a_heads, out_heads, n_rope_tiles, dilations, col_axis):
    a_ref, w_ref = refs[:2]
    refs = refs[2:]
    if n_rope_tiles:
        cos_ref, sin_ref = refs[:2]
        refs = refs[2:]
    o_ref = refs[0]
    view_refs = refs[1:1 + len(dilations)]
    stage = refs[1 + len(dilations)] if dilations else None
    if a_heads:
        a = jnp.concatenate([a_ref[0, h] for h in range(a_heads)], axis=-1)
    else:
        a = a_ref[...]
    if not out_heads:
        o_ref[...] = _dot(a, w_ref[...])
        return
    tm = a.shape[0]
    heads_per_dot = min(out_heads, MM_TN // HEAD_DIM)

    def write(rot):
        for h0 in range(0, out_heads, heads_per_dot):
            acc = _dot(a, w_ref[:, h0 * HEAD_DIM:(h0 + heads_per_dot) * HEAD_DIM])
            for hh in range(h0, h0 + heads_per_dot):
                xh = acc[:, (hh - h0) * HEAD_DIM:(hh - h0 + 1) * HEAD_DIM]
                if rot:
                    xh = xh * cos_ref[...] + _swap_halves(xh) * sin_ref[...]
                o_ref[0, 0, hh] = xh.astype(BF16)
                if dilations:
                    stage[hh] = xh
                    for d, v_ref in zip(dilations, view_refs):
                        for r in range(d):
                            v_ref[0, 0, hh, :, r * HEAD_DIM:(r + 1) * HEAD_DIM] = (
                                stage[hh, pl.ds(r, tm // d, stride=d), :].astype(BF16))

    if not n_rope_tiles:
        write(False)
    else:
        j = pl.program_id(col_axis)

        @pl.when(j < n_rope_tiles)
        def _():
            write(True)

        @pl.when(j >= n_rope_tiles)
        def _():
            write(False)


def _mm(a, w, *, B, S, head_major_out=False, rope=None, n_rope_cols=0, dilations=()):
    K, N = w.shape
    T = B * S
    H = K // HEAD_DIM
    tm = _tile(MM_TM // 2 if dilations else MM_TM, S)
    tn = _tile(2 * MM_TN if dilations else MM_TN, N)
    nsb = S // tm
    cols_outer = bool(dilations)
    ix = (lambda f: (lambda j, i: f(i, j))) if cols_outer else (lambda f: f)
    a_heads = H if a.ndim == 4 else 0
    if a_heads:
        a_spec = pl.BlockSpec((1, H, tm, HEAD_DIM), ix(lambda i, j: (i // nsb, 0, i % nsb, 0)))
    else:
        a_spec = pl.BlockSpec((tm, K), ix(lambda i, j: (i, 0)))
    in_specs = [a_spec, pl.BlockSpec((K, tn), ix(lambda i, j: (0, j)))]
    args = [a, w]
    out_heads = tn // HEAD_DIM if head_major_out else 0
    n_rope_tiles = 0
    if head_major_out:
        D = H * HEAD_DIM
        G = N // D
        tiles_per_group = D // tn
        if rope is not None:
            assert n_rope_cols % tn == 0
            n_rope_tiles = n_rope_cols // tn
            tab = pl.BlockSpec((tm, HEAD_DIM), ix(lambda i, j: (i, 0)))
            in_specs += [tab, tab]
            args += list(rope)
        head_index = ix(lambda i, j: (j // tiles_per_group, i // nsb, j % tiles_per_group,
                                      i % nsb, 0))
        out_spec = [pl.BlockSpec((1, 1, out_heads, tm // d, d * HEAD_DIM), head_index)
                    for d in (1,) + tuple(dilations)]
        out_shape = [jax.ShapeDtypeStruct((G, B, H, S // d, d * HEAD_DIM), BF16)
                     for d in (1,) + tuple(dilations)]
    else:
        out_spec = pl.BlockSpec((tm, tn), ix(lambda i, j: (i, j)))
        out_shape = jax.ShapeDtypeStruct((T, N), F32)
    scratch = [pltpu.VMEM((out_heads, tm, HEAD_DIM), F32)] if dilations else []
    out = pl.pallas_call(
        functools.partial(_mm_kernel, a_heads=a_heads, out_heads=out_heads,
                          n_rope_tiles=n_rope_tiles, dilations=tuple(dilations),
                          col_axis=0 if cols_outer else 1),
        grid=(N // tn, T // tm) if cols_outer else (T // tm, N // tn),
        in_specs=in_specs,
        out_specs=out_spec,
        out_shape=out_shape,
        scratch_shapes=scratch,
        compiler_params=_cparams("parallel", "parallel"),
        name="proj",
    )(*args)
    if head_major_out and not dilations:
        return out[0]
    return out


def _dil_attn_kernel(q1, q4, q16, k1c, k4c, k16c, k1p, k4p, k16p, v1c, v4c, v16c, v1p, v4p, v16p,
                     o_ref, *scratch, scale):
    acc_s, m_s, l_s = scratch[0:4], scratch[4:8], scratch[8:12]
    bias_s, s_s, p_s, mu_s, st_acc, st_m, st_l, out_s = scratch[12:20]

    qi = lax.broadcasted_iota(I32, (BLOCK, 2 * BLOCK), 0)
    kc = lax.broadcasted_iota(I32, (BLOCK, 2 * BLOCK), 1)
    band = jnp.where(kc >= qi, jnp.where(kc <= qi + BLOCK, 0.0, NEG_INF), NEG_INF).astype(F32)
    bias_s[0] = band
    no_prev = jnp.where(pl.program_id(2) == 0, NEG_INF, 0.0).astype(F32)
    bias_s[1] = jnp.where(kc < BLOCK, band + no_prev, band)
    ones = jnp.ones((2 * BLOCK, HEAD_DIM), BF16)

    def run(units):
        for g, (q, kp, kc_, _, _, first, _) in enumerate(units):
            keys = jnp.concatenate([kp(), kc_()], axis=0)
            s_s[g] = _dot_nt(q(), keys) * scale + bias_s[1 if first else 0]
        for g in range(len(units)):
            s = s_s[g]
            m = jnp.max(jnp.maximum(s[:, :BLOCK], s[:, BLOCK:]), axis=1, keepdims=True)
            p_s[g] = jnp.exp(s - m).astype(BF16)
            mu_s[g] = jnp.broadcast_to(m, (BLOCK, HEAD_DIM))
        for g, (_, _, _, vp, vc, _, sink) in enumerate(units):
            p = p_s[g]
            vals = jnp.concatenate([vp(), vc()], axis=0)
            sink(g, _dot(p, vals), mu_s[g], _dot(p, ones))

    def fold(slab, rows, acc_u, m_u, l_u):
        m_o = m_s[slab][rows, :]
        m_n = jnp.maximum(m_o, m_u)
        e_o = jnp.exp(m_o - m_n)
        e_u = jnp.exp(m_u - m_n)
        acc_s[slab][rows, :] = acc_s[slab][rows, :] * e_o + acc_u * e_u
        l_s[slab][rows, :] = l_s[slab][rows, :] * e_o + l_u * e_u
        m_s[slab][rows, :] = m_n

    def tile(ref, rows, lanes):
        return lambda: ref[0, 0, 0, rows, lanes]

    def banded(q, kc_, kp, vc, vp, n, lanes, sink):
        cur = slice(n * BLOCK, (n + 1) * BLOCK)
        if n == 0:
            k_prev, v_prev = tile(kp, slice(None), lanes), tile(vp, slice(None), lanes)
        else:
            prv = slice((n - 1) * BLOCK, n * BLOCK)
            k_prev, v_prev = tile(kc_, prv, lanes), tile(vc, prv, lanes)
        return (tile(q, cur, lanes), k_prev, tile(kc_, cur, lanes), v_prev, tile(vc, cur, lanes),
                n == 0, sink)

    units = []
    for n in range(SUPER // 4 // BLOCK):
        for r in range(4):
            def init(g, acc, m, l, r=r, n=n):
                rows = slice(n * BLOCK, (n + 1) * BLOCK)
                acc_s[r][rows, :], m_s[r][rows, :], l_s[r][rows, :] = acc, m, l

            units.append(banded(q4, k4c, k4p, v4c, v4p, n,
                                slice(r * HEAD_DIM, (r + 1) * HEAD_DIM), init))

    for n in range(SUPER // BLOCK):
        def fold1(g, acc, m, l, n=n):
            st_acc[g], st_m[g], st_l[g] = acc, m, l
            rows = slice(n * (BLOCK // 4), (n + 1) * (BLOCK // 4))
            for r in range(4):
                pick = pl.ds(r, BLOCK // 4, stride=4)
                fold(r, rows, st_acc[g, pick, :], st_m[g, pick, :], st_l[g, pick, :])

        units.append(banded(q1, k1c, k1p, v1c, v1p, n, slice(None), fold1))

    for r in range(16):
        def fold16(g, acc, m, l, r=r):
            fold(r % 4, pl.ds(r // 4, BLOCK, stride=4), acc, m, l)

        units.append(banded(q16, k16c, k16p, v16c, v16p, 0,
                            slice(r * HEAD_DIM, (r + 1) * HEAD_DIM), fold16))

    for start in range(0, len(units), DIL_GROUP):
        run(units[start:start + DIL_GROUP])

    for r in range(4):
        out_s[pl.ds(r, SUPER // 4, stride=4), :] = acc_s[r][...] / l_s[r][...]
    o_ref[0, 0] = out_s[...].astype(BF16)


def _dilated_attention(qkv1, qkv4, qkv16):
    _, B, H, S, Dh = qkv1.shape
    assert Dh == HEAD_DIM and S % SUPER == 0
    assert tuple(d for _, d in DILATED_BRANCHES) == DILATIONS
    assert all(w // d == BLOCK for w, d in DILATED_BRANCHES)
    views = {1: qkv1, 4: qkv4, 16: qkv16}

    def cur(which, d):
        return pl.BlockSpec((1, 1, 1, SUPER // d, d * Dh), lambda b, h, m: (which, b, h, m, 0))

    def prev(which, d):
        per = SUPER // d // BLOCK
        return pl.BlockSpec((1, 1, 1, BLOCK, d * Dh),
                            lambda b, h, m: (which, b, h, jnp.maximum(m * per - 1, 0), 0))

    in_specs, args = [], []
    for make, which in ((cur, 0), (cur, 1), (prev, 1), (cur, 2), (prev, 2)):
        for d in (1, 4, 16):
            in_specs.append(make(which, d))
            args.append(views[d])
    state = [pltpu.VMEM((SUPER // 4, Dh), F32)] * 12
    stage = [pltpu.VMEM((2, BLOCK, 2 * BLOCK), F32),
             pltpu.VMEM((DIL_GROUP, BLOCK, 2 * BLOCK), F32),
             pltpu.VMEM((DIL_GROUP, BLOCK, 2 * BLOCK), BF16),
             pltpu.VMEM((DIL_GROUP, BLOCK, Dh), F32)] + [
             pltpu.VMEM((DIL_GROUP, BLOCK, Dh), F32)] * 3 + [
             pltpu.VMEM((SUPER, Dh), F32)]
    return pl.pallas_call(
        functools.partial(_dil_attn_kernel, scale=Dh ** -0.5),
        grid=(B, H, S // SUPER),
        in_specs=in_specs,
        out_specs=pl.BlockSpec((1, 1, SUPER, Dh), lambda b, h, m: (b, h, m, 0)),
        out_shape=jax.ShapeDtypeStruct((B, H, S, Dh), BF16),
        scratch_shapes=state + stage,
        compiler_params=_cparams("parallel", "parallel", "parallel"),
        name="dilated_attention",
    )(*args)


def _sb_kernel(q_ref, k_ref, v_ref, o_ref, tri_ref, acc_ref, carry_ref, logb_s, rsum_s,
               hi_s, lo_s, a_s, *, scale, tq, nq):
    row = lax.broadcasted_iota(I32, (tq, tq), 0)
    col = lax.broadcasted_iota(I32, (tq, tq), 1)
    tri_ref[...] = jnp.where(row > col, 1.0, 0.0).astype(BF16)

    def block_rows(n):
        return pl.ds(pl.multiple_of(n * tq, tq), tq)

    def sweep(n0, i):
        diagonal = isinstance(i, int) and i == 0
        causal = col < row
        keys = [block_rows(jnp.maximum(n0 + g - i, 0)) for g in range(SB_GROUP)]
        for g in range(SB_GROUP):
            z2 = _dot_nt(q_ref[0, 0, block_rows(n0 + g), :], k_ref[0, 0, 0, keys[g], :])
            z2 = z2 * (scale * LOG2_E)
            t = jnp.log2(1.0 + jnp.exp2(-jnp.abs(z2)))
            log_1mb = -(jnp.maximum(z2, 0.0) + t)
            if diagonal:
                log_1mb = jnp.where(causal, log_1mb, 0.0)
            logb_s[g] = jnp.minimum(z2, 0.0) - t
            hi = log_1mb.astype(BF16)
            hi_s[g] = hi
            lo_s[g] = (log_1mb - hi.astype(F32)).astype(BF16)
            rsum_s[g] = jnp.broadcast_to(jnp.sum(log_1mb, axis=1, keepdims=True), (tq, tq))
        tops = []
        for g in range(SB_GROUP):
            tail = _dot(hi_s[g], tri_ref[...]) + _dot(lo_s[g], tri_ref[...])
            if diagonal:
                a = jnp.where(causal, jnp.exp2(logb_s[g] + tail), 0.0)
                c = rsum_s[g]
            else:
                c = carry_ref[g]
                a = jnp.where(n0 + g - i >= 0, jnp.exp2(logb_s[g] + tail + c), 0.0)
                c = c + rsum_s[g]
            a_s[g] = a.astype(BF16)
            carry_ref[g] = c
            tops.append(jnp.where(n0 + g - i >= 1, jnp.max(c), NEG_INF))
        for g in range(SB_GROUP):
            av = _dot(a_s[g], v_ref[0, 0, 0, keys[g], :])
            acc_ref[g] = av if diagonal else acc_ref[g] + av
        return functools.reduce(jnp.maximum, tops)

    def qgroup(it, carry):
        n0 = it * SB_GROUP

        def cond(st):
            _, top = st
            return top > -SB_SKIP

        def body(st):
            i, _ = st
            return i + 1, sweep(n0, i)

        lax.while_loop(cond, body, (1, sweep(n0, 0)))
        for g in range(SB_GROUP):
            o_ref[0, 0, block_rows(n0 + g), :] = acc_ref[g].astype(BF16)
        return carry

    lax.fori_loop(0, nq // SB_GROUP, qgroup, 0)


def _stick_breaking_attention(q, kv):
    _, B, H, S, Dh = q.shape
    tq = _tile(SB_TQ, S)
    seq = lambda which: pl.BlockSpec((1, 1, 1, S, Dh), lambda b, h: (which, b, h, 0, 0))
    return pl.pallas_call(
        functools.partial(_sb_kernel, scale=Dh ** -0.5, tq=tq, nq=S // tq),
        grid=(B, H),
        in_specs=[pl.BlockSpec((1, 1, S, Dh), lambda b, h: (b, h, 0, 0)), seq(0), seq(1)],
        out_specs=pl.BlockSpec((1, 1, S, Dh), lambda b, h: (b, h, 0, 0)),
        out_shape=jax.ShapeDtypeStruct((B, H, S, Dh), BF16),
        scratch_shapes=[pltpu.VMEM((tq, tq), BF16), pltpu.VMEM((SB_GROUP, tq, Dh), F32)]
        + [pltpu.VMEM((SB_GROUP, tq, tq), F32)] * 3 + [pltpu.VMEM((SB_GROUP, tq, tq), BF16)] * 3,
        compiler_params=_cparams("parallel", "parallel"),
        name="stick_breaking_attention",
    )(q.reshape(B, H, S, Dh), kv, kv)


def _router_kernel(h_ref, rw_ref, rb_ref, e_ref, r_ref, w_ref, cnt_ref, tri_ref, carry_ref,
                   *, n_exp, tm, nsteps):
    i = pl.program_id(0)
    gsz = n_exp // N_GROUPS

    @pl.when(i == 0)
    def _():
        row = lax.broadcasted_iota(I32, (tm, tm), 0)
        col = lax.broadcasted_iota(I32, (tm, tm), 1)
        tri_ref[...] = jnp.where(row < col, 1.0, 0.0).astype(BF16)
        carry_ref[...] = jnp.zeros_like(carry_ref)

    lo, hi = _unpack_pair(_load_slab_words(h_ref, tm))
    half = lo.shape[1]
    logits = _dot_nt(rw_ref[:, :half], lo) + _dot_nt(rw_ref[:, half:], hi)
    scores = jax.nn.sigmoid(logits)
    sel = scores + rb_ref[...]

    sub = lax.broadcasted_iota(I32, (gsz, tm), 0)
    groups, gscore = [], []
    for g in range(N_GROUPS):
        sg = sel[g * gsz:(g + 1) * gsz, :]
        m1 = jnp.max(sg, axis=0, keepdims=True)
        i1 = jnp.min(jnp.where(sg == m1, sub, gsz), axis=0, keepdims=True)
        m2 = jnp.max(jnp.where(sub == i1, NEG_INF, sg), axis=0, keepdims=True)
        groups.append(sg)
        gscore.append(m1 + m2)
    masked = []
    for g in range(N_GROUPS):
        beat = jnp.zeros((1, tm), F32)
        for o in range(N_GROUPS):
            if o == g:
                continue
            beat += jnp.where(gscore[o] > gscore[g], 1.0, 0.0)
            if o < g:
                beat += jnp.where(gscore[o] == gscore[g], 1.0, 0.0)
        masked.append(jnp.where(beat < TOPK_GROUPS, groups[g], NEG_INF))
    x = jnp.concatenate(masked, axis=0)

    eid = lax.broadcasted_iota(I32, (n_exp, tm), 0)
    chosen = jnp.zeros((n_exp, tm), F32)
    picks, vals = [], []
    for _ in range(TOP_K):
        mx = jnp.max(x, axis=0, keepdims=True)
        idx = jnp.min(jnp.where(x == mx, eid, n_exp), axis=0, keepdims=True)
        hit = eid == idx
        picks.append(idx)
        vals.append(jnp.sum(jnp.where(hit, scores, 0.0), axis=0, keepdims=True))
        chosen += jnp.where(hit, 1.0, 0.0)
        x = jnp.where(hit, NEG_INF, x)
    total = vals[0]
    for v in vals[1:]:
        total = total + v

    rank_all = _dot(chosen.astype(BF16), tri_ref[...]) + carry_ref[...]
    for k in range(TOP_K):
        e_ref[k:k + 1, :] = picks[k]
        w_ref[k:k + 1, :] = vals[k] / total * ROUTED_SCALE
        rk = jnp.sum(jnp.where(eid == picks[k], rank_all, 0.0), axis=0, keepdims=True)
        r_ref[k:k + 1, :] = rk.astype(I32)
    carry_ref[...] += jnp.broadcast_to(jnp.sum(chosen, axis=1, keepdims=True), (n_exp, tm))

    @pl.when(i == nsteps - 1)
    def _():
        cnt_ref[...] = carry_ref[:, 0:LANES].astype(I32)


def _router(h32, rw_t, rb):
    E, D = rw_t.shape
    ch = D // 2 // LANES
    T = h32.shape[0] // ch
    tm = _tile(ROUTER_TM, T)
    nsteps = T // tm
    tok = pl.BlockSpec((TOP_K, tm), lambda i: (0, i))
    return pl.pallas_call(
        functools.partial(_router_kernel, n_exp=E, tm=tm, nsteps=nsteps),
        grid=(nsteps,),
        in_specs=[pl.BlockSpec((tm * ch, LANES), lambda i: (i, 0)),
                  pl.BlockSpec((E, D), lambda i: (0, 0)),
                  pl.BlockSpec((E, 1), lambda i: (0, 0))],
        out_specs=[tok, tok, tok, pl.BlockSpec((E, LANES), lambda i: (0, 0))],
        out_shape=[jax.ShapeDtypeStruct((TOP_K, T), I32), jax.ShapeDtypeStruct((TOP_K, T), I32),
                   jax.ShapeDtypeStruct((TOP_K, T), F32), jax.ShapeDtypeStruct((E, LANES), I32)],
        scratch_shapes=[pltpu.VMEM((tm, tm), BF16), pltpu.VMEM((E, tm), F32)],
        compiler_params=_cparams("arbitrary"),
        name="moe_router",
    )(h32, rw_t, rb)


def _dispatch_kernel(last_ref, slots_ref, h_ref, xs_hbm, zbuf, zsem, sem, *, n_exp, td, ch):
    i = pl.program_id(0)
    tile_rows = zbuf.shape[0]

    def zero_copy(e):
        dst = pl.ds(pl.multiple_of(last_ref[e], tile_rows), tile_rows)
        return pltpu.make_async_copy(zbuf, xs_hbm.at[dst], zsem)

    @pl.when(i == 0)
    def _():
        zbuf[...] = jnp.zeros_like(zbuf)

        def start(e, c):
            @pl.when(last_ref[e] >= 0)
            def _():
                zero_copy(e).start()
            return c

        def wait(e, c):
            @pl.when(last_ref[e] >= 0)
            def _():
                zero_copy(e).wait()
            return c

        lax.fori_loop(0, n_exp, start, 0)
        lax.fori_loop(0, n_exp, wait, 0)

    def row_copy(t, k):
        src = pl.ds(pl.multiple_of(t * ch, ch), ch)
        dst = pl.ds(pl.multiple_of(slots_ref[k, t], ch), ch)
        return pltpu.make_async_copy(h_ref.at[src], xs_hbm.at[dst], sem)

    def start_rows(t, c):
        for k in range(TOP_K):
            row_copy(t, k).start(priority=k % 2)
        return c

    def wait_rows(t, c):
        for k in range(TOP_K):
            row_copy(t, k).wait()
        return c

    lax.fori_loop(0, td, start_rows, 0)
    lax.fori_loop(0, td, wait_rows, 0)


def _dispatch(h32, slot_rows, last_rows, n_slots, tm_e, ch):
    T = slot_rows.shape[1]
    td = _tile(DISPATCH_TD, T)
    E = last_rows.shape[0]
    return pl.pallas_call(
        functools.partial(_dispatch_kernel, n_exp=E, td=td, ch=ch),
        grid_spec=pltpu.PrefetchScalarGridSpec(
            num_scalar_prefetch=1,
            grid=(T // td,),
            in_specs=[pl.BlockSpec((TOP_K, td), lambda i, last: (0, i), memory_space=pltpu.SMEM),
                      pl.BlockSpec((td * ch, LANES), lambda i, last: (i, 0))],
            out_specs=pl.BlockSpec(memory_space=pl.ANY),
            scratch_shapes=[pltpu.VMEM((tm_e * ch, LANES), U32), pltpu.SemaphoreType.DMA,
                            pltpu.SemaphoreType.DMA]),
        out_shape=jax.ShapeDtypeStruct((n_slots * ch, LANES), U32),
        compiler_params=_cparams("arbitrary"),
        name="moe_dispatch",
    )(last_rows, slot_rows, h32)


def _mlp_hidden(x_ref, wgu_ref, n_tok):
    f = wgu_ref.shape[1] // 2
    lo, hi = _unpack_pair(_load_slab_words(x_ref, n_tok))
    half = lo.shape[1]
    gu = _dot(lo, wgu_ref[:half, :]) + _dot(hi, wgu_ref[half:, :])
    return (jax.nn.silu(gu[:, :f]) * gu[:, f:]).astype(BF16)


def _expert_kernel(te_ref, nt_ref, x_ref, wg_ref, wu_ref, wd_ref, y_ref, wgu_s, wd_s, *, tm_e):
    i = pl.program_id(0)
    f, d_model = wd_s.shape
    half = d_model // 2
    ch = half // LANES

    @pl.when(i < nt_ref[0])
    def _():
        @pl.when(jnp.logical_or(i == 0, te_ref[i] != te_ref[jnp.maximum(i - 1, 0)]))
        def _():
            wgu_s[:, :f] = wg_ref[0, 0].astype(BF16)
            wgu_s[:, f:] = wu_ref[0, 0].astype(BF16)
            for c in range(ch):
                lo_cols = slice(c * LANES, (c + 1) * LANES)
                hi_cols = slice(half + c * LANES, half + (c + 1) * LANES)
                wd_s[:, 2 * c * LANES:(2 * c + 1) * LANES] = wd_ref[0, 0, :, lo_cols].astype(BF16)
                wd_s[:, (2 * c + 1) * LANES:(2 * c + 2) * LANES] = wd_ref[0, 0, :, hi_cols].astype(BF16)

        a = _mlp_hidden(x_ref, wgu_s, tm_e)
        for c in range(ch):
            yc = _dot(a, wd_s[:, 2 * c * LANES:(2 * c + 2) * LANES])
            y_ref[pl.ds(c, tm_e, stride=ch), :] = _pack_pair(yc[:, :LANES], yc[:, LANES:])


def _shared_kernel(x_ref, wgu_ref, wd_ref, y_ref, *, tm):
    y_ref[...] = _dot(_mlp_hidden(x_ref, wgu_ref, tm), wd_ref[...])


def _experts(xs, layer, w_gate, w_up, w_down, tile_expert, n_tiles, tm_e, ch):
    _, E, D, F = w_gate.shape
    n_slots = xs.shape[0] // ch
    rows = pl.BlockSpec((tm_e * ch, LANES), lambda i, te, nt: (jnp.minimum(i, nt[0] - 1), 0))
    w_in = pl.BlockSpec((1, 1, D, F), lambda i, te, nt: (layer, te[i], 0, 0))
    return pl.pallas_call(
        functools.partial(_expert_kernel, tm_e=tm_e),
        grid_spec=pltpu.PrefetchScalarGridSpec(
            num_scalar_prefetch=2,
            grid=(n_slots // tm_e,),
            in_specs=[rows, w_in, w_in,
                      pl.BlockSpec((1, 1, F, D), lambda i, te, nt: (layer, te[i], 0, 0))],
            out_specs=rows,
            scratch_shapes=[pltpu.VMEM((D, 2 * F), BF16), pltpu.VMEM((F, D), BF16)]),
        out_shape=jax.ShapeDtypeStruct(xs.shape, U32),
        compiler_params=_cparams("arbitrary"),
        name="moe_experts",
    )(tile_expert, n_tiles, xs, w_gate, w_up, w_down)


def _shared_expert(h32, wgu, wd, ch):
    D, F2 = wgu.shape
    T = h32.shape[0] // ch
    tm = _tile(512, T)
    return pl.pallas_call(
        functools.partial(_shared_kernel, tm=tm),
        grid=(T // tm,),
        in_specs=[pl.BlockSpec((tm * ch, LANES), lambda i: (i, 0)),
                  pl.BlockSpec((D, F2), lambda i: (0, 0)),
                  pl.BlockSpec((F2 // 2, D), lambda i: (0, 0))],
        out_specs=pl.BlockSpec((tm, D), lambda i: (i, 0)),
        out_shape=jax.ShapeDtypeStruct((T, D), F32),
        compiler_params=_cparams("parallel"),
        name="moe_shared_expert",
    )(h32, wgu, wd)


def _combine_kernel(slots_ref, next_slots_ref, w_ref, ysh_ref, y_hbm, o_ref, bufs, gate_s, sems,
                    *, tc, ch, nsteps):
    i = pl.program_id(0)
    slot = i % 2

    def row_copy(table, which, t, k):
        src = pl.ds(pl.multiple_of(table[k, t], ch), ch)
        dst = pl.ds(pl.multiple_of(t * ch, ch), ch)
        return pltpu.make_async_copy(y_hbm.at[src], bufs.at[which, k, dst], sems.at[which])

    def gather_token(table, which, t):
        for k in range(TOP_K):
            row_copy(table, which, t, k).start(priority=k % 2)

    def wait_all(table, which):
        def wait_rows(t, c):
            for k in range(TOP_K):
                row_copy(table, which, t, k).wait()
            return c

        lax.fori_loop(0, tc, wait_rows, 0)

    @pl.when(i == 0)
    def _():
        def start_rows(t, c):
            gather_token(slots_ref, 0, t)
            return c

        lax.fori_loop(0, tc, start_rows, 0)

    wait_all(slots_ref, slot)

    half = ch * LANES
    w_cols = w_ref[...].T
    for k in range(TOP_K):
        gate_s[k] = jnp.broadcast_to(w_cols[:, k:k + 1], (tc, LANES))
    per_chunk = tc // ch
    for c in range(ch):
        lanes_lo = slice(c * LANES, (c + 1) * LANES)
        lanes_hi = slice(half + c * LANES, half + (c + 1) * LANES)
        acc_lo = ysh_ref[:, lanes_lo]
        acc_hi = ysh_ref[:, lanes_hi]
        for k in range(TOP_K):
            lo, hi = _unpack_pair_f32(bufs.at[slot, k][pl.ds(c, tc, stride=ch), :])
            acc_lo = acc_lo + gate_s[k] * lo
            acc_hi = acc_hi + gate_s[k] * hi
        o_ref[:, lanes_lo] = acc_lo
        o_ref[:, lanes_hi] = acc_hi
        for t in range(c * per_chunk, (c + 1) * per_chunk):
            gather_token(next_slots_ref, 1 - slot, t)

    @pl.when(i == nsteps - 1)
    def _():
        wait_all(next_slots_ref, 1 - slot)


def _combine(slot_rows, wts, ysh, ys, ch):
    T, D = ysh.shape
    tc = _tile(COMBINE_TC, T)
    nsteps = T // tc
    return pl.pallas_call(
        functools.partial(_combine_kernel, tc=tc, ch=ch, nsteps=nsteps),
        grid=(nsteps,),
        in_specs=[pl.BlockSpec((TOP_K, tc), lambda i: (0, i), memory_space=pltpu.SMEM),
                  pl.BlockSpec((TOP_K, tc), lambda i: (0, jnp.minimum(i + 1, nsteps - 1)),
                               memory_space=pltpu.SMEM),
                  pl.BlockSpec((TOP_K, tc), lambda i: (0, i)),
                  pl.BlockSpec((tc, D), lambda i: (i, 0)),
                  pl.BlockSpec(memory_space=pl.ANY)],
        out_specs=pl.BlockSpec((tc, D), lambda i: (i, 0)),
        out_shape=jax.ShapeDtypeStruct((T, D), F32),
        scratch_shapes=[pltpu.VMEM((2, TOP_K, tc * ch, LANES), U32),
                        pltpu.VMEM((TOP_K, tc, LANES), F32),
                        pltpu.SemaphoreType.DMA((2,))],
        compiler_params=_cparams("arbitrary"),
        name="moe_combine",
    )(slot_rows, slot_rows, wts, ysh, ys)


def _moe(h32, layer, rw_t, rb, w_gate, w_up, w_down, sh_gu, sh_d):
    E, D = rw_t.shape
    ch = D // 2 // LANES
    assert ch % 8 == 0, "a token slab must be whole (8,128) tiles"
    T = h32.shape[0] // ch
    tm_e = _tile(EXPERT_TM, T)
    eidx, rank, wts, counts = _router(h32, rw_t, rb)

    counts = counts[:, 0]
    tiles = (counts + tm_e - 1) // tm_e
    tile_end = jnp.cumsum(tiles)
    tile_start = tile_end - tiles
    n_tiles = tile_end[-1]
    experts = jnp.arange(E, dtype=I32)
    first_row = jnp.sum(jnp.where(eidx[:, :, None] == experts, tile_start * tm_e, 0), axis=-1)
    slot_rows = (first_row + rank) * ch
    max_tiles = T * TOP_K // tm_e + E
    tile_ids = jnp.minimum(jnp.arange(max_tiles, dtype=I32), n_tiles - 1)
    tile_expert = jnp.sum((tile_ids[:, None] >= tile_end[None, :]).astype(I32), axis=-1)
    last_rows = jnp.where(tiles > 0, (tile_end - 1) * (tm_e * ch), -1).astype(I32)

    xs = _dispatch(h32, slot_rows, last_rows, max_tiles * tm_e, tm_e, ch)
    ys = _experts(xs, layer, w_gate, w_up, w_down, tile_expert,
                  n_tiles.reshape(1).astype(I32), tm_e, ch)
    ysh = _shared_expert(h32, sh_gu, sh_d, ch)
    return _combine(slot_rows, wts, ysh, ys, ch)


def kernel(x, c, positions, ada_w, ada_b, ln_g, ln_b, a_w_qkv, a_w_o, kv_ada_w, kv_ada_b, b_w_kv, b_w_q, b_w_o, router_w, router_b, w_gate, w_up, w_down, sh_gate, sh_up, sh_down):
    B, S, D = x.shape
    T = B * S
    assert D % HEAD_DIM == 0 and DEPTH == ada_w.shape[0] == 2
    xf = x.reshape(T, D)

    c8 = jnp.zeros((8, D), F32).at[:B].set(c)
    mods = _adaln(c8, ada_w, ada_b)[:, :B].reshape(DEPTH, B, 6, D)
    kv_mods = _adaln(c8, kv_ada_w[None], kv_ada_b[None])[0, :B].reshape(B, 2, D)
    sh_a, sc_a, g_a, sh_m, sc_m, g_m = (mods[:, :, r] for r in range(6))
    rope = _rope_tables(positions)

    def moe_weights(l):
        sh_gu = jnp.concatenate([sh_gate[l], sh_up[l]], axis=-1).astype(BF16)
        return (l, router_w[l].T.astype(BF16), router_b[l].reshape(-1, 1), w_gate, w_up, w_down,
                sh_gu, sh_down[l].astype(BF16))

    h = _modulate(xf, _mod_rows(None, [(sh_a[0], sc_a[0])]), S)
    qkv_views = _mm(h, a_w_qkv[0].astype(BF16), B=B, S=S, head_major_out=True, rope=rope,
                    n_rope_cols=2 * D, dilations=DILATIONS[1:])
    o = _dilated_attention(*qkv_views)
    y = _mm(o, a_w_o[0].astype(BF16), B=B, S=S)
    xf, h32 = _ln(xf, y, _mod_rows(g_a[0], [(sh_m[0], sc_m[0])]), ln_g[0, 0], ln_b[0, 0],
                  ["u32"], S)
    y = _moe(h32, *moe_weights(0))
    xf, h_kv, h_q = _ln(xf, y, _mod_rows(g_m[0], [(kv_mods[:, 0], kv_mods[:, 1]),
                                                  (sh_a[1], sc_a[1])]),
                        ln_g[0, 1], ln_b[0, 1], ["bf16", "bf16"], S)

    kv = _mm(h_kv, b_w_kv.astype(BF16), B=B, S=S, head_major_out=True)
    q = _mm(h_q, b_w_q[0].astype(BF16), B=B, S=S, head_major_out=True)
    o = _stick_breaking_attention(q, kv)
    y = _mm(o, b_w_o[0].astype(BF16), B=B, S=S)
    xf, h32 = _ln(xf, y, _mod_rows(g_a[1], [(sh_m[1], sc_m[1])]), ln_g[1, 0], ln_b[1, 0],
                  ["u32"], S)
    y = _moe(h32, *moe_weights(1))
    (xf,) = _ln(xf, y, _mod_rows(g_m[1], []), ln_g[1, 1], ln_b[1, 1], [], S)
    return xf.reshape(B, S, D)
```

```python
import functools
import math

import jax
import jax.numpy as jnp
from jax import lax
from jax.experimental import pallas as pl
from jax.experimental.pallas import tpu as pltpu

F32 = jnp.float32
BF16 = jnp.bfloat16
U32 = jnp.uint32
I32 = jnp.int32

HEAD_DIM = 128
ROT_DIM = HEAD_DIM // 4
ROPE_THETA = 500000.0
DILATED_BRANCHES = ((128, 1), (512, 4), (2048, 16))
DILATIONS = tuple(d for _, d in DILATED_BRANCHES)
BLOCK = 128
TOP_K = 8
N_GROUPS = 8
TOPK_GROUPS = 4
ROUTED_SCALE = 2.5
DEPTH = 2
DEEPNORM_ALPHA = (2 * DEPTH) ** 0.25
LN_EPS = 1e-5

LANES = 128
VMEM_LIMIT_BYTES = 56 * 1024 * 1024

MM_TM = 1024
MM_TN = 512
LN_TM = 256
ROUTER_TM = 512
EXPERT_TM = 512
DISPATCH_TD = 128
COMBINE_TC = 128
SUPER = 2048
DIL_GROUP = 8
SB_TQ = 128
SB_GROUP = 8
SB_SKIP = 127.0
LOG2_E = 1.4426950408889634

NEG_INF = float("-inf")
HI_MASK = 0xFFFF0000


def _tile(pref, dim):
    t = min(pref, dim)
    assert dim % t == 0, (pref, dim)
    return t


def _cparams(*sem):
    return pltpu.CompilerParams(dimension_semantics=sem, vmem_limit_bytes=VMEM_LIMIT_BYTES)


def _dot(a, b):
    return jnp.dot(a, b, preferred_element_type=F32)


def _dot_nt(a, b):
    return lax.dot_general(a, b, (((1,), (1,)), ((), ())), preferred_element_type=F32)


def _unpack_pair(w):
    lo = lax.bitcast_convert_type(w << 16, F32).astype(BF16)
    hi = lax.bitcast_convert_type(w & jnp.uint32(HI_MASK), F32).astype(BF16)
    return lo, hi


def _unpack_pair_f32(w):
    lo = lax.bitcast_convert_type(w << 16, F32)
    hi = lax.bitcast_convert_type(w & jnp.uint32(HI_MASK), F32)
    return lo, hi


def _pack_pair(lo, hi):
    lo_b = lax.bitcast_convert_type(lo.astype(BF16).astype(F32), U32)
    hi_b = lax.bitcast_convert_type(hi.astype(BF16).astype(F32), U32)
    return (hi_b & jnp.uint32(HI_MASK)) | (lo_b >> 16)


def _load_slab_words(ref, n_tok):
    ch = ref.shape[0] // n_tok
    return jnp.concatenate([ref[pl.ds(c, n_tok, stride=ch), :] for c in range(ch)], axis=1)


def _store_slab_words(ref, words):
    n_tok = words.shape[0]
    ch = ref.shape[0] // n_tok
    for c in range(ch):
        ref[pl.ds(c, n_tok, stride=ch), :] = words[:, c * LANES:(c + 1) * LANES]


def _adaln_kernel(c_ref, w_ref, b_ref, o_ref, *, nk):
    k = pl.program_id(2)

    @pl.when(k == 0)
    def _():
        o_ref[...] = jnp.zeros_like(o_ref)

    cond = jax.nn.silu(c_ref[...])
    o_ref[0] += jnp.dot(cond, w_ref[0], preferred_element_type=F32,
                        precision=lax.Precision.HIGHEST)

    @pl.when(k == nk - 1)
    def _():
        o_ref[0] += b_ref[0]


def _adaln(c8, w, b):
    L, D, N = w.shape
    tk = _tile(1024, D)
    tn = _tile(2048, N)
    nk = D // tk
    return pl.pallas_call(
        functools.partial(_adaln_kernel, nk=nk),
        grid=(L, N // tn, nk),
        in_specs=[pl.BlockSpec((8, tk), lambda l, j, k: (0, k)),
                  pl.BlockSpec((1, tk, tn), lambda l, j, k: (l, k, j)),
                  pl.BlockSpec((1, 1, tn), lambda l, j, k: (l, 0, j))],
        out_specs=pl.BlockSpec((1, 8, tn), lambda l, j, k: (l, 0, j)),
        out_shape=jax.ShapeDtypeStruct((L, 8, N), F32),
        compiler_params=_cparams("parallel", "parallel", "arbitrary"),
        name="adaln",
    )(c8, w, b.reshape(L, 1, N))


def _rope_kernel(pos_ref, freq_ref, sign_ref, cos_ref, sin_ref):
    ang = pos_ref[...] * freq_ref[...]
    cos_ref[...] = jnp.cos(ang)
    sin_ref[...] = jnp.sin(ang) * sign_ref[...]


def _rope_tables(positions):
    T = positions.size
    half = ROT_DIM // 2
    inv_freq = ROPE_THETA ** (-jnp.arange(0, ROT_DIM, 2, dtype=F32) / ROT_DIM)
    freq = jnp.concatenate([inv_freq, inv_freq, jnp.zeros((HEAD_DIM - ROT_DIM,), F32)])[None, :]
    sign = jnp.concatenate([-jnp.ones((half,), F32), jnp.ones((HEAD_DIM - half,), F32)])[None, :]
    pos = jnp.broadcast_to(positions.astype(F32).reshape(T, 1), (T, HEAD_DIM))
    tm = _tile(2048, T)
    row = pl.BlockSpec((tm, HEAD_DIM), lambda i: (i, 0))
    vec = pl.BlockSpec((1, HEAD_DIM), lambda i: (0, 0))
    return pl.pallas_call(
        _rope_kernel,
        grid=(T // tm,),
        in_specs=[row, vec, vec],
        out_specs=[row, row],
        out_shape=[jax.ShapeDtypeStruct((T, HEAD_DIM), F32)] * 2,
        compiler_params=_cparams("parallel"),
        name="rope_tables",
    )(pos, freq, sign)


def _emit_mod(xn, rows_ref, idx, fmt, out_ref):
    shift = rows_ref[0, 1 + 2 * idx:2 + 2 * idx, :]
    scale = rows_ref[0, 2 + 2 * idx:3 + 2 * idx, :]
    h = xn * (1.0 + scale) + shift
    if fmt == "bf16":
        out_ref[...] = h.astype(BF16)
    else:
        half = h.shape[1] // 2
        _store_slab_words(out_ref, _pack_pair(h[:, :half], h[:, half:]))


def _modulate_kernel(x_ref, rows_ref, o_ref):
    _emit_mod(x_ref[...], rows_ref, 0, "bf16", o_ref)


def _ln_kernel(x_ref, y_ref, rows_ref, g_ref, b_ref, xo_ref, *h_refs, fmts):
    gate = rows_ref[0, 0:1, :]
    z = DEEPNORM_ALPHA * x_ref[...] + (1.0 + gate) * y_ref[...]
    mu = jnp.mean(z, axis=-1, keepdims=True)
    zc = z - mu
    var = jnp.mean(zc * zc, axis=-1, keepdims=True)
    xn = zc * lax.rsqrt(var + LN_EPS) * g_ref[...] + b_ref[...]
    xo_ref[...] = xn
    for idx, (fmt, h_ref) in enumerate(zip(fmts, h_refs)):
        _emit_mod(xn, rows_ref, idx, fmt, h_ref)


def _mod_rows(gate, pairs):
    B, D = pairs[0][0].shape if pairs else gate.shape
    rows = [gate if gate is not None else jnp.zeros((B, D), F32)]
    for shift, scale in pairs:
        rows += [shift, scale]
    rows += [jnp.zeros((B, D), F32)] * (8 - len(rows))
    return jnp.stack(rows, axis=1)


def _out_struct(T, D, tm, fmt):
    if fmt == "bf16":
        return jax.ShapeDtypeStruct((T, D), BF16), (tm, D)
    ch = D // 2 // LANES
    return jax.ShapeDtypeStruct((T * ch, LANES), U32), (tm * ch, LANES)


def _modulate(x, rows, S):
    T, D = x.shape
    tm = _tile(LN_TM, S)
    nb = S // tm
    return pl.pallas_call(
        _modulate_kernel,
        grid=(T // tm,),
        in_specs=[pl.BlockSpec((tm, D), lambda i: (i, 0)),
                  pl.BlockSpec((1, 8, D), lambda i: (i // nb, 0, 0))],
        out_specs=pl.BlockSpec((tm, D), lambda i: (i, 0)),
        out_shape=jax.ShapeDtypeStruct((T, D), BF16),
        compiler_params=_cparams("parallel"),
        name="modulate",
    )(x, rows)


def _ln(x, y, rows, g, b, fmts, S):
    T, D = x.shape
    tm = _tile(LN_TM, S)
    nb = S // tm
    row = pl.BlockSpec((tm, D), lambda i: (i, 0))
    vec = pl.BlockSpec((1, D), lambda i: (0, 0))
    out_shape = [jax.ShapeDtypeStruct((T, D), F32)]
    out_specs = [row]
    for fmt in fmts:
        st, blk = _out_struct(T, D, tm, fmt)
        out_shape.append(st)
        out_specs.append(pl.BlockSpec(blk, lambda i: (i, 0)))
    return pl.pallas_call(
        functools.partial(_ln_kernel, fmts=tuple(fmts)),
        grid=(T // tm,),
        in_specs=[row, row, pl.BlockSpec((1, 8, D), lambda i: (i // nb, 0, 0)), vec, vec],
        out_specs=out_specs,
        out_shape=out_shape,
        compiler_params=_cparams("parallel"),
        name="deepnorm_ln",
    )(x, y, rows, g.reshape(1, D), b.reshape(1, D))


def _swap_halves(x):
    half = ROT_DIM // 2
    lane = lax.broadcasted_iota(I32, x.shape, 1)
    return jnp.where(lane < half, pltpu.roll(x, HEAD_DIM - half, 1), pltpu.roll(x, half, 1))


def _mm_kernel(*refs, a_heads, out_heads, n_rope_tiles, dilations, col_axis):
    a_ref, w_ref = refs[:2]
    refs = refs[2:]
    if n_rope_tiles:
        cos_ref, sin_ref = refs[:2]
        refs = refs[2:]
    o_ref = refs[0]
    view_refs = refs[1:1 + len(dilations)]
    stage = refs[1 + len(dilations)] if dilations else None
    if a_heads:
        a = jnp.concatenate([a_ref[0, h] for h in range(a_heads)], axis=-1)
    else:
        a = a_ref[...]
    if not out_heads:
        o_ref[...] = _dot(a, w_ref[...])
        return
    tm = a.shape[0]
    heads_per_dot = min(out_heads, MM_TN // HEAD_DIM)

    def write(rot):
        for h0 in range(0, out_heads, heads_per_dot):
            acc = _dot(a, w_ref[:, h0 * HEAD_DIM:(h0 + heads_per_dot) * HEAD_DIM])
            for hh in range(h0, h0 + heads_per_dot):
                xh = acc[:, (hh - h0) * HEAD_DIM:(hh - h0 + 1) * HEAD_DIM]
                if rot:
                    xh = xh * cos_ref[...] + _swap_halves(xh) * sin_ref[...]
                o_ref[0, 0, hh] = xh.astype(BF16)
                if dilations:
                    stage[hh] = xh
                    for d, v_ref in zip(dilations, view_refs):
                        for r in range(d):
                            v_ref[0, 0, hh, :, r * HEAD_DIM:(r + 1) * HEAD_DIM] = (
                                stage[hh, pl.ds(r, tm // d, stride=d), :].astype(BF16))

    if not n_rope_tiles:
        write(False)
    else:
        j = pl.program_id(col_axis)

        @pl.when(j < n_rope_tiles)
        def _():
            write(True)

        @pl.when(j >= n_rope_tiles)
        def _():
            write(False)


def _mm(a, w, *, B, S, head_major_out=False, rope=None, n_rope_cols=0, dilations=()):
    K, N = w.shape
    T = B * S
    H = K // HEAD_DIM
    tm = _tile(MM_TM // 2 if dilations else MM_TM, S)
    tn = _tile(2 * MM_TN if dilations else MM_TN, N)
    nsb = S // tm
    cols_outer = bool(dilations)
    ix = (lambda f: (lambda j, i: f(i, j))) if cols_outer else (lambda f: f)
    a_heads = H if a.ndim == 4 else 0
    if a_heads:
        a_spec = pl.BlockSpec((1, H, tm, HEAD_DIM), ix(lambda i, j: (i // nsb, 0, i % nsb, 0)))
    else:
        a_spec = pl.BlockSpec((tm, K), ix(lambda i, j: (i, 0)))
    in_specs = [a_spec, pl.BlockSpec((K, tn), ix(lambda i, j: (0, j)))]
    args = [a, w]
    out_heads = tn // HEAD_DIM if head_major_out else 0
    n_rope_tiles = 0
    if head_major_out:
        D = H * HEAD_DIM
        G = N // D
        tiles_per_group = D // tn
        if rope is not None:
            assert n_rope_cols % tn == 0
            n_rope_tiles = n_rope_cols // tn
            tab = pl.BlockSpec((tm, HEAD_DIM), ix(lambda i, j: (i, 0)))
            in_specs += [tab, tab]
            args += list(rope)
        head_index = ix(lambda i, j: (j // tiles_per_group, i // nsb, j % tiles_per_group,
                                      i % nsb, 0))
        out_spec = [pl.BlockSpec((1, 1, out_heads, tm // d, d * HEAD_DIM), head_index)
                    for d in (1,) + tuple(dilations)]
        out_shape = [jax.ShapeDtypeStruct((G, B, H, S // d, d * HEAD_DIM), BF16)
                     for d in (1,) + tuple(dilations)]
    else:
        out_spec = pl.BlockSpec((tm, tn), ix(lambda i, j: (i, j)))
        out_shape = jax.ShapeDtypeStruct((T, N), F32)
    scratch = [pltpu.VMEM((out_heads, tm, HEAD_DIM), F32)] if dilations else []
    out = pl.pallas_call(
        functools.partial(_mm_kernel, a_heads=a_heads, out_heads=out_heads,
                          n_rope_tiles=n_rope_tiles, dilations=tuple(dilations),
                          col_axis=0 if cols_outer else 1),
        grid=(N // tn, T // tm) if cols_outer else (T // tm, N // tn),
        in_specs=in_specs,
        out_specs=out_spec,
        out_shape=out_shape,
        scratch_shapes=scratch,
        compiler_params=_cparams("parallel", "parallel"),
        name="proj",
    )(*args)
    if head_major_out and not dilations:
        return out[0]
    return out


def _dil_attn_kernel(q1, q4, q16, k1c, k4c, k16c, k1p, k4p, k16p, v1c, v4c, v16c, v1p, v4p, v16p,
                     o_ref, *scratch, scale):
    acc_s, m_s, l_s = scratch[0:4], scratch[4:8], scratch[8:12]
    bias_s, s_s, p_s, mu_s, st_acc, st_m, st_l, out_s = scratch[12:20]

    qi = lax.broadcasted_iota(I32, (BLOCK, 2 * BLOCK), 0)
    kc = lax.broadcasted_iota(I32, (BLOCK, 2 * BLOCK), 1)
    band = jnp.where(kc >= qi, jnp.where(kc <= qi + BLOCK, 0.0, NEG_INF), NEG_INF).astype(F32)
    bias_s[0] = band
    no_prev = jnp.where(pl.program_id(2) == 0, NEG_INF, 0.0).astype(F32)
    bias_s[1] = jnp.where(kc < BLOCK, band + no_prev, band)
    ones = jnp.ones((2 * BLOCK, HEAD_DIM), BF16)

    def run(units):
        for g, (q, kp, kc_, _, _, first, _) in enumerate(units):
            keys = jnp.concatenate([kp(), kc_()], axis=0)
            s_s[g] = _dot_nt(q(), keys) * scale + bias_s[1 if first else 0]
        for g in range(len(units)):
            s = s_s[g]
            m = jnp.max(jnp.maximum(s[:, :BLOCK], s[:, BLOCK:]), axis=1, keepdims=True)
            p_s[g] = jnp.exp(s - m).astype(BF16)
            mu_s[g] = jnp.broadcast_to(m, (BLOCK, HEAD_DIM))
        for g, (_, _, _, vp, vc, _, sink) in enumerate(units):
            vals = jnp.concatenate([vp(), vc()], axis=0)
            pv = _dot(p_s[g], jnp.concatenate([vals, ones], axis=1))
            sink(g, pv[:, :HEAD_DIM], mu_s[g], pv[:, HEAD_DIM:])

    def fold(slab, rows, acc_u, m_u, l_u):
        m_o = m_s[slab][rows, :]
        m_n = jnp.maximum(m_o, m_u)
        e_o = jnp.exp(m_o - m_n)
        e_u = jnp.exp(m_u - m_n)
        acc_s[slab][rows, :] = acc_s[slab][rows, :] * e_o + acc_u * e_u
        l_s[slab][rows, :] = l_s[slab][rows, :] * e_o + l_u * e_u
        m_s[slab][rows, :] = m_n

    def tile(ref, rows, lanes):
        return lambda: ref[0, 0, 0, rows, lanes]

    def banded(q, kc_, kp, vc, vp, n, lanes, sink):
        cur = slice(n * BLOCK, (n + 1) * BLOCK)
        if n == 0:
            k_prev, v_prev = tile(kp, slice(None), lanes), tile(vp, slice(None), lanes)
        else:
            prv = slice((n - 1) * BLOCK, n * BLOCK)
            k_prev, v_prev = tile(kc_, prv, lanes), tile(vc, prv, lanes)
        return (tile(q, cur, lanes), k_prev, tile(kc_, cur, lanes), v_prev, tile(vc, cur, lanes),
                n == 0, sink)

    units = []
    for n in range(SUPER // 4 // BLOCK):
        for r in range(4):
            def init(g, acc, m, l, r=r, n=n):
                rows = slice(n * BLOCK, (n + 1) * BLOCK)
                acc_s[r][rows, :], m_s[r][rows, :], l_s[r][rows, :] = acc, m, l

            units.append(banded(q4, k4c, k4p, v4c, v4p, n,
                                slice(r * HEAD_DIM, (r + 1) * HEAD_DIM), init))

    for n in range(SUPER // BLOCK):
        def fold1(g, acc, m, l, n=n):
            st_acc[g], st_m[g], st_l[g] = acc, m, l
            rows = slice(n * (BLOCK // 4), (n + 1) * (BLOCK // 4))
            for r in range(4):
                pick = pl.ds(r, BLOCK // 4, stride=4)
                fold(r, rows, st_acc[g, pick, :], st_m[g, pick, :], st_l[g, pick, :])

        units.append(banded(q1, k1c, k1p, v1c, v1p, n, slice(None), fold1))

    for r in range(16):
        def fold16(g, acc, m, l, r=r):
            fold(r % 4, pl.ds(r // 4, BLOCK, stride=4), acc, m, l)

        units.append(banded(q16, k16c, k16p, v16c, v16p, 0,
                            slice(r * HEAD_DIM, (r + 1) * HEAD_DIM), fold16))

    for start in range(0, len(units), DIL_GROUP):
        run(units[start:start + DIL_GROUP])

    for r in range(4):
        out_s[pl.ds(r, SUPER // 4, stride=4), :] = acc_s[r][...] / l_s[r][...]
    o_ref[0, 0] = out_s[...].astype(BF16)


def _dilated_attention(qkv1, qkv4, qkv16):
    _, B, H, S, Dh = qkv1.shape
    assert Dh == HEAD_DIM and S % SUPER == 0
    assert tuple(d for _, d in DILATED_BRANCHES) == DILATIONS
    assert all(w // d == BLOCK for w, d in DILATED_BRANCHES)
    views = {1: qkv1, 4: qkv4, 16: qkv16}

    def cur(which, d):
        return pl.BlockSpec((1, 1, 1, SUPER // d, d * Dh), lambda b, h, m: (which, b, h, m, 0))

    def prev(which, d):
        per = SUPER // d // BLOCK
        return pl.BlockSpec((1, 1, 1, BLOCK, d * Dh),
                            lambda b, h, m: (which, b, h, jnp.maximum(m * per - 1, 0), 0))

    in_specs, args = [], []
    for make, which in ((cur, 0), (cur, 1), (prev, 1), (cur, 2), (prev, 2)):
        for d in (1, 4, 16):
            in_specs.append(make(which, d))
            args.append(views[d])
    state = [pltpu.VMEM((SUPER // 4, Dh), F32)] * 12
    stage = [pltpu.VMEM((2, BLOCK, 2 * BLOCK), F32),
             pltpu.VMEM((DIL_GROUP, BLOCK, 2 * BLOCK), F32),
             pltpu.VMEM((DIL_GROUP, BLOCK, 2 * BLOCK), BF16),
             pltpu.VMEM((DIL_GROUP, BLOCK, Dh), F32)] + [
             pltpu.VMEM((DIL_GROUP, BLOCK, Dh), F32)] * 3 + [
             pltpu.VMEM((SUPER, Dh), F32)]
    return pl.pallas_call(
        functools.partial(_dil_attn_kernel, scale=Dh ** -0.5),
        grid=(B, H, S // SUPER),
        in_specs=in_specs,
        out_specs=pl.BlockSpec((1, 1, SUPER, Dh), lambda b, h, m: (b, h, m, 0)),
        out_shape=jax.ShapeDtypeStruct((B, H, S, Dh), BF16),
        scratch_shapes=state + stage,
        compiler_params=_cparams("parallel", "parallel", "parallel"),
        name="dilated_attention",
    )(*args)


def _sb_kernel(q_ref, k_ref, v_ref, o_ref, tri_ref, acc_ref, carry_ref, logb_s, rsum_s,
               hi_s, lo_s, a_s, *, scale, tq, nq):
    row = lax.broadcasted_iota(I32, (tq, tq), 0)
    col = lax.broadcasted_iota(I32, (tq, tq), 1)
    tri_ref[...] = jnp.where(row > col, 1.0, 0.0).astype(BF16)

    def block_rows(n):
        return pl.ds(pl.multiple_of(n * tq, tq), tq)

    def sweep(n0, i):
        diagonal = isinstance(i, int) and i == 0
        causal = col < row
        keys = [block_rows(jnp.maximum(n0 + g - i, 0)) for g in range(SB_GROUP)]
        for g in range(SB_GROUP):
            z2 = _dot_nt(q_ref[0, 0, block_rows(n0 + g), :], k_ref[0, 0, 0, keys[g], :])
            z2 = z2 * (scale * LOG2_E)
            t = jnp.log2(1.0 + jnp.exp2(-jnp.abs(z2)))
            log_1mb = -(jnp.maximum(z2, 0.0) + t)
            if diagonal:
                log_1mb = jnp.where(causal, log_1mb, 0.0)
            logb_s[g] = jnp.minimum(z2, 0.0) - t
            hi = log_1mb.astype(BF16)
            hi_s[g] = hi
            lo_s[g] = (log_1mb - hi.astype(F32)).astype(BF16)
            rsum_s[g] = jnp.broadcast_to(jnp.sum(log_1mb, axis=1, keepdims=True), (tq, tq))
        tops = []
        for g in range(SB_GROUP):
            tail = _dot(hi_s[g], tri_ref[...]) + _dot(lo_s[g], tri_ref[...])
            if diagonal:
                a = jnp.where(causal, jnp.exp2(logb_s[g] + tail), 0.0)
                c = rsum_s[g]
            else:
                c = carry_ref[g]
                a = jnp.where(n0 + g - i >= 0, jnp.exp2(logb_s[g] + tail + c), 0.0)
                c = c + rsum_s[g]
            a_s[g] = a.astype(BF16)
            carry_ref[g] = c
            tops.append(jnp.where(n0 + g - i >= 1, jnp.max(c), NEG_INF))
        for g in range(SB_GROUP):
            av = _dot(a_s[g], v_ref[0, 0, 0, keys[g], :])
            acc_ref[g] = av if diagonal else acc_ref[g] + av
        return functools.reduce(jnp.maximum, tops)

    def qgroup(it, carry):
        n0 = it * SB_GROUP

        def cond(st):
            _, top = st
            return top > -SB_SKIP

        def body(st):
            i, _ = st
            return i + 1, sweep(n0, i)

        lax.while_loop(cond, body, (1, sweep(n0, 0)))
        for g in range(SB_GROUP):
            o_ref[0, 0, block_rows(n0 + g), :] = acc_ref[g].astype(BF16)
        return carry

    lax.fori_loop(0, nq // SB_GROUP, qgroup, 0)


def _stick_breaking_attention(q, kv):
    _, B, H, S, Dh = q.shape
    tq = _tile(SB_TQ, S)
    seq = lambda which: pl.BlockSpec((1, 1, 1, S, Dh), lambda b, h: (which, b, h, 0, 0))
    return pl.pallas_call(
        functools.partial(_sb_kernel, scale=Dh ** -0.5, tq=tq, nq=S // tq),
        grid=(B, H),
        in_specs=[pl.BlockSpec((1, 1, S, Dh), lambda b, h: (b, h, 0, 0)), seq(0), seq(1)],
        out_specs=pl.BlockSpec((1, 1, S, Dh), lambda b, h: (b, h, 0, 0)),
        out_shape=jax.ShapeDtypeStruct((B, H, S, Dh), BF16),
        scratch_shapes=[pltpu.VMEM((tq, tq), BF16), pltpu.VMEM((SB_GROUP, tq, Dh), F32)]
        + [pltpu.VMEM((SB_GROUP, tq, tq), F32)] * 3 + [pltpu.VMEM((SB_GROUP, tq, tq), BF16)] * 3,
        compiler_params=_cparams("parallel", "parallel"),
        name="stick_breaking_attention",
    )(q.reshape(B, H, S, Dh), kv, kv)


def _router_kernel(h_ref, rw_ref, rb_ref, e_ref, r_ref, w_ref, cnt_ref, tri_ref, carry_ref,
                   *, n_exp, tm, nsteps):
    i = pl.program_id(0)
    gsz = n_exp // N_GROUPS

    @pl.when(i == 0)
    def _():
        row = lax.broadcasted_iota(I32, (tm, tm), 0)
        col = lax.broadcasted_iota(I32, (tm, tm), 1)
        tri_ref[...] = jnp.where(row < col, 1.0, 0.0).astype(BF16)
        carry_ref[...] = jnp.zeros_like(carry_ref)

    lo, hi = _unpack_pair(_load_slab_words(h_ref, tm))
    half = lo.shape[1]
    logits = _dot_nt(rw_ref[:, :half], lo) + _dot_nt(rw_ref[:, half:], hi)
    scores = jax.nn.sigmoid(logits)
    sel = scores + rb_ref[...]

    sub = lax.broadcasted_iota(I32, (gsz, tm), 0)
    groups, gscore = [], []
    for g in range(N_GROUPS):
        sg = sel[g * gsz:(g + 1) * gsz, :]
        m1 = jnp.max(sg, axis=0, keepdims=True)
        i1 = jnp.min(jnp.where(sg == m1, sub, gsz), axis=0, keepdims=True)
        m2 = jnp.max(jnp.where(sub == i1, NEG_INF, sg), axis=0, keepdims=True)
        groups.append(sg)
        gscore.append(m1 + m2)
    masked = []
    for g in range(N_GROUPS):
        beat = jnp.zeros((1, tm), F32)
        for o in range(N_GROUPS):
            if o == g:
                continue
            beat += jnp.where(gscore[o] > gscore[g], 1.0, 0.0)
            if o < g:
                beat += jnp.where(gscore[o] == gscore[g], 1.0, 0.0)
        masked.append(jnp.where(beat < TOPK_GROUPS, groups[g], NEG_INF))
    x = jnp.concatenate(masked, axis=0)

    eid = lax.broadcasted_iota(I32, (n_exp, tm), 0)
    chosen = jnp.zeros((n_exp, tm), F32)
    picks, vals = [], []
    for _ in range(TOP_K):
        mx = jnp.max(x, axis=0, keepdims=True)
        idx = jnp.min(jnp.where(x == mx, eid, n_exp), axis=0, keepdims=True)
        hit = eid == idx
        picks.append(idx)
        vals.append(jnp.sum(jnp.where(hit, scores, 0.0), axis=0, keepdims=True))
        chosen += jnp.where(hit, 1.0, 0.0)
        x = jnp.where(hit, NEG_INF, x)
    total = vals[0]
    for v in vals[1:]:
        total = total + v

    rank_all = _dot(chosen.astype(BF16), tri_ref[...]) + carry_ref[...]
    for k in range(TOP_K):
        e_ref[k:k + 1, :] = picks[k]
        w_ref[k:k + 1, :] = vals[k] / total * ROUTED_SCALE
        rk = jnp.sum(jnp.where(eid == picks[k], rank_all, 0.0), axis=0, keepdims=True)
        r_ref[k:k + 1, :] = rk.astype(I32)
    carry_ref[...] += jnp.broadcast_to(jnp.sum(chosen, axis=1, keepdims=True), (n_exp, tm))

    @pl.when(i == nsteps - 1)
    def _():
        cnt_ref[...] = carry_ref[:, 0:LANES].astype(I32)


def _router(h32, rw_t, rb):
    E, D = rw_t.shape
    ch = D // 2 // LANES
    T = h32.shape[0] // ch
    tm = _tile(ROUTER_TM, T)
    nsteps = T // tm
    tok = pl.BlockSpec((TOP_K, tm), lambda i: (0, i))
    return pl.pallas_call(
        functools.partial(_router_kernel, n_exp=E, tm=tm, nsteps=nsteps),
        grid=(nsteps,),
        in_specs=[pl.BlockSpec((tm * ch, LANES), lambda i: (i, 0)),
                  pl.BlockSpec((E, D), lambda i: (0, 0)),
                  pl.BlockSpec((E, 1), lambda i: (0, 0))],
        out_specs=[tok, tok, tok, pl.BlockSpec((E, LANES), lambda i: (0, 0))],
        out_shape=[jax.ShapeDtypeStruct((TOP_K, T), I32), jax.ShapeDtypeStruct((TOP_K, T), I32),
                   jax.ShapeDtypeStruct((TOP_K, T), F32), jax.ShapeDtypeStruct((E, LANES), I32)],
        scratch_shapes=[pltpu.VMEM((tm, tm), BF16), pltpu.VMEM((E, tm), F32)],
        compiler_params=_cparams("arbitrary"),
        name="moe_router",
    )(h32, rw_t, rb)


def _dispatch_kernel(last_ref, slots_ref, h_ref, xs_hbm, zbuf, zsem, sem, *, n_exp, td, ch):
    i = pl.program_id(0)
    tile_rows = zbuf.shape[0]

    def zero_copy(e):
        dst = pl.ds(pl.multiple_of(last_ref[e], tile_rows), tile_rows)
        return pltpu.make_async_copy(zbuf, xs_hbm.at[dst], zsem)

    @pl.when(i == 0)
    def _():
        zbuf[...] = jnp.zeros_like(zbuf)

        def start(e, c):
            @pl.when(last_ref[e] >= 0)
            def _():
                zero_copy(e).start()
            return c

        def wait(e, c):
            @pl.when(last_ref[e] >= 0)
            def _():
                zero_copy(e).wait()
            return c

        lax.fori_loop(0, n_exp, start, 0)
        lax.fori_loop(0, n_exp, wait, 0)

    def row_copy(t, k):
        src = pl.ds(pl.multiple_of(t * ch, ch), ch)
        dst = pl.ds(pl.multiple_of(slots_ref[k, t], ch), ch)
        return pltpu.make_async_copy(h_ref.at[src], xs_hbm.at[dst], sem)

    def start_rows(t, c):
        for k in range(TOP_K):
            row_copy(t, k).start(priority=k % 2)
        return c

    def wait_rows(t, c):
        for k in range(TOP_K):
            row_copy(t, k).wait()
        return c

    lax.fori_loop(0, td, start_rows, 0)
    lax.fori_loop(0, td, wait_rows, 0)


def _dispatch(h32, slot_rows, last_rows, n_slots, tm_e, ch):
    T = slot_rows.shape[1]
    td = _tile(DISPATCH_TD, T)
    E = last_rows.shape[0]
    return pl.pallas_call(
        functools.partial(_dispatch_kernel, n_exp=E, td=td, ch=ch),
        grid_spec=pltpu.PrefetchScalarGridSpec(
            num_scalar_prefetch=1,
            grid=(T // td,),
            in_specs=[pl.BlockSpec((TOP_K, td), lambda i, last: (0, i), memory_space=pltpu.SMEM),
                      pl.BlockSpec((td * ch, LANES), lambda i, last: (i, 0))],
            out_specs=pl.BlockSpec(memory_space=pl.ANY),
            scratch_shapes=[pltpu.VMEM((tm_e * ch, LANES), U32), pltpu.SemaphoreType.DMA,
                            pltpu.SemaphoreType.DMA]),
        out_shape=jax.ShapeDtypeStruct((n_slots * ch, LANES), U32),
        compiler_params=_cparams("arbitrary"),
        name="moe_dispatch",
    )(last_rows, slot_rows, h32)


def _mlp_hidden(x_ref, wgu_ref, n_tok):
    f = wgu_ref.shape[1] // 2
    lo, hi = _unpack_pair(_load_slab_words(x_ref, n_tok))
    half = lo.shape[1]
    gu = _dot(lo, wgu_ref[:half, :]) + _dot(hi, wgu_ref[half:, :])
    return (jax.nn.silu(gu[:, :f]) * gu[:, f:]).astype(BF16)


def _expert_kernel(te_ref, nx_ref, nt_ref, x_ref, wg_hbm, wu_hbm, wd_hbm, y_ref,
                   wg_f, wu_f, wd_f, wgu_s, wd_s, sems, *, tm_e, layer):
    i = pl.program_id(0)
    f, d_model = wd_s.shape
    half = d_model // 2
    ch = half // LANES

    def weight_copies(e):
        return (pltpu.make_async_copy(wg_hbm.at[layer, e], wg_f, sems.at[0]),
                pltpu.make_async_copy(wu_hbm.at[layer, e], wu_f, sems.at[1]),
                pltpu.make_async_copy(wd_hbm.at[layer, e], wd_f, sems.at[2]))

    @pl.when(i < nt_ref[0])
    def _():
        expert = te_ref[i]

        @pl.when(i == 0)
        def _():
            for cp in weight_copies(expert):
                cp.start()

        @pl.when(jnp.logical_or(i == 0, expert != te_ref[jnp.maximum(i - 1, 0)]))
        def _():
            for cp in weight_copies(expert):
                cp.wait()
            wgu_s[:, :f] = wg_f[...].astype(BF16)
            wgu_s[:, f:] = wu_f[...].astype(BF16)
            for c in range(ch):
                lo_cols = slice(c * LANES, (c + 1) * LANES)
                hi_cols = slice(half + c * LANES, half + (c + 1) * LANES)
                wd_s[:, 2 * c * LANES:(2 * c + 1) * LANES] = wd_f[:, lo_cols].astype(BF16)
                wd_s[:, (2 * c + 1) * LANES:(2 * c + 2) * LANES] = wd_f[:, hi_cols].astype(BF16)

            @pl.when(nx_ref[i] >= 0)
            def _():
                for cp in weight_copies(nx_ref[i]):
                    cp.start()

        a = _mlp_hidden(x_ref, wgu_s, tm_e)
        for c in range(ch):
            yc = _dot(a, wd_s[:, 2 * c * LANES:(2 * c + 2) * LANES])
            y_ref[pl.ds(c, tm_e, stride=ch), :] = _pack_pair(yc[:, :LANES], yc[:, LANES:])


def _shared_kernel(x_ref, wgu_ref, wd_ref, y_ref, *, tm):
    y_ref[...] = _dot(_mlp_hidden(x_ref, wgu_ref, tm), wd_ref[...])


def _experts(xs, layer, w_gate, w_up, w_down, tile_expert, next_expert, n_tiles, tm_e, ch):
    _, E, D, F = w_gate.shape
    n_slots = xs.shape[0] // ch
    rows = pl.BlockSpec((tm_e * ch, LANES),
                        lambda i, te, nx, nt: (jnp.minimum(i, nt[0] - 1), 0))
    hbm = pl.BlockSpec(memory_space=pl.ANY)
    return pl.pallas_call(
        functools.partial(_expert_kernel, tm_e=tm_e, layer=layer),
        grid_spec=pltpu.PrefetchScalarGridSpec(
            num_scalar_prefetch=3,
            grid=(n_slots // tm_e,),
            in_specs=[rows, hbm, hbm, hbm],
            out_specs=rows,
            scratch_shapes=[pltpu.VMEM((D, F), F32), pltpu.VMEM((D, F), F32),
                            pltpu.VMEM((F, D), F32), pltpu.VMEM((D, 2 * F), BF16),
                            pltpu.VMEM((F, D), BF16), pltpu.SemaphoreType.DMA((3,))]),
        out_shape=jax.ShapeDtypeStruct(xs.shape, U32),
        compiler_params=_cparams("arbitrary"),
        name="moe_experts",
    )(tile_expert, next_expert, n_tiles, xs, w_gate, w_up, w_down)


def _shared_expert(h32, wgu, wd, ch):
    D, F2 = wgu.shape
    T = h32.shape[0] // ch
    tm = _tile(512, T)
    return pl.pallas_call(
        functools.partial(_shared_kernel, tm=tm),
        grid=(T // tm,),
        in_specs=[pl.BlockSpec((tm * ch, LANES), lambda i: (i, 0)),
                  pl.BlockSpec((D, F2), lambda i: (0, 0)),
                  pl.BlockSpec((F2 // 2, D), lambda i: (0, 0))],
        out_specs=pl.BlockSpec((tm, D), lambda i: (i, 0)),
        out_shape=jax.ShapeDtypeStruct((T, D), F32),
        compiler_params=_cparams("parallel"),
        name="moe_shared_expert",
    )(h32, wgu, wd)


def _combine_kernel(slots_ref, next_slots_ref, w_ref, ysh_ref, y_hbm, o_ref, bufs, gate_s, sems,
                    *, tc, ch, nsteps):
    i = pl.program_id(0)
    slot = i % 2

    def row_copy(table, which, t, k):
        src = pl.ds(pl.multiple_of(table[k, t], ch), ch)
        dst = pl.ds(pl.multiple_of(t * ch, ch), ch)
        return pltpu.make_async_copy(y_hbm.at[src], bufs.at[which, k, dst], sems.at[which])

    def gather_token(table, which, t):
        for k in range(TOP_K):
            row_copy(table, which, t, k).start(priority=k % 2)

    def wait_all(table, which):
        def wait_rows(t, c):
            for k in range(TOP_K):
                row_copy(table, which, t, k).wait()
            return c

        lax.fori_loop(0, tc, wait_rows, 0)

    @pl.when(i == 0)
    def _():
        def start_rows(t, c):
            gather_token(slots_ref, 0, t)
            return c

        lax.fori_loop(0, tc, start_rows, 0)

    wait_all(slots_ref, slot)

    half = ch * LANES
    w_cols = w_ref[...].T
    for k in range(TOP_K):
        gate_s[k] = jnp.broadcast_to(w_cols[:, k:k + 1], (tc, LANES))
    per_chunk = tc // ch
    for c in range(ch):
        lanes_lo = slice(c * LANES, (c + 1) * LANES)
        lanes_hi = slice(half + c * LANES, half + (c + 1) * LANES)
        acc_lo = ysh_ref[:, lanes_lo]
        acc_hi = ysh_ref[:, lanes_hi]
        for k in range(TOP_K):
            lo, hi = _unpack_pair_f32(bufs.at[slot, k][pl.ds(c, tc, stride=ch), :])
            acc_lo = acc_lo + gate_s[k] * lo
            acc_hi = acc_hi + gate_s[k] * hi
        o_ref[:, lanes_lo] = acc_lo
        o_ref[:, lanes_hi] = acc_hi
        for t in range(c * per_chunk, (c + 1) * per_chunk):
            gather_token(next_slots_ref, 1 - slot, t)

    @pl.when(i == nsteps - 1)
    def _():
        wait_all(next_slots_ref, 1 - slot)


def _combine(slot_rows, wts, ysh, ys, ch):
    T, D = ysh.shape
    tc = _tile(COMBINE_TC, T)
    nsteps = T // tc
    return pl.pallas_call(
        functools.partial(_combine_kernel, tc=tc, ch=ch, nsteps=nsteps),
        grid=(nsteps,),
        in_specs=[pl.BlockSpec((TOP_K, tc), lambda i: (0, i), memory_space=pltpu.SMEM),
                  pl.BlockSpec((TOP_K, tc), lambda i: (0, jnp.minimum(i + 1, nsteps - 1)),
                               memory_space=pltpu.SMEM),
                  pl.BlockSpec((TOP_K, tc), lambda i: (0, i)),
                  pl.BlockSpec((tc, D), lambda i: (i, 0)),
                  pl.BlockSpec(memory_space=pl.ANY)],
        out_specs=pl.BlockSpec((tc, D), lambda i: (i, 0)),
        out_shape=jax.ShapeDtypeStruct((T, D), F32),
        scratch_shapes=[pltpu.VMEM((2, TOP_K, tc * ch, LANES), U32),
                        pltpu.VMEM((TOP_K, tc, LANES), F32),
                        pltpu.SemaphoreType.DMA((2,))],
        compiler_params=_cparams("arbitrary"),
        name="moe_combine",
    )(slot_rows, slot_rows, wts, ysh, ys)


def _moe(h32, layer, rw_t, rb, w_gate, w_up, w_down, sh_gu, sh_d):
    E, D = rw_t.shape
    ch = D // 2 // LANES
    assert ch % 8 == 0, "a token slab must be whole (8,128) tiles"
    T = h32.shape[0] // ch
    tm_e = _tile(EXPERT_TM, T)
    eidx, rank, wts, counts = _router(h32, rw_t, rb)

    counts = counts[:, 0]
    tiles = (counts + tm_e - 1) // tm_e
    tile_end = jnp.cumsum(tiles)
    tile_start = tile_end - tiles
    n_tiles = tile_end[-1]
    experts = jnp.arange(E, dtype=I32)
    first_row = jnp.sum(jnp.where(eidx[:, :, None] == experts, tile_start * tm_e, 0), axis=-1)
    slot_rows = (first_row + rank) * ch
    max_tiles = T * TOP_K // tm_e + E
    tile_ids = jnp.minimum(jnp.arange(max_tiles, dtype=I32), n_tiles - 1)
    tile_expert = jnp.sum((tile_ids[:, None] >= tile_end[None, :]).astype(I32), axis=-1)
    group_end = jnp.sum(jnp.where(tile_expert[:, None] == experts, tile_end, 0), axis=-1)
    after = jnp.sum((group_end[:, None] >= tile_end[None, :]).astype(I32), axis=-1)
    next_expert = jnp.where(group_end < n_tiles, after, -1).astype(I32)
    last_rows = jnp.where(tiles > 0, (tile_end - 1) * (tm_e * ch), -1).astype(I32)

    xs = _dispatch(h32, slot_rows, last_rows, max_tiles * tm_e, tm_e, ch)
    ys = _experts(xs, layer, w_gate, w_up, w_down, tile_expert, next_expert,
                  n_tiles.reshape(1).astype(I32), tm_e, ch)
    ysh = _shared_expert(h32, sh_gu, sh_d, ch)
    return _combine(slot_rows, wts, ysh, ys, ch)


def kernel(x, c, positions, ada_w, ada_b, ln_g, ln_b, a_w_qkv, a_w_o, kv_ada_w, kv_ada_b, b_w_kv, b_w_q, b_w_o, router_w, router_b, w_gate, w_up, w_down, sh_gate, sh_up, sh_down):
    B, S, D = x.shape
    T = B * S
    assert D % HEAD_DIM == 0 and DEPTH == ada_w.shape[0] == 2
    xf = x.reshape(T, D)

    c8 = jnp.zeros((8, D), F32).at[:B].set(c)
    mods = _adaln(c8, ada_w, ada_b)[:, :B].reshape(DEPTH, B, 6, D)
    kv_mods = _adaln(c8, kv_ada_w[None], kv_ada_b[None])[0, :B].reshape(B, 2, D)
    sh_a, sc_a, g_a, sh_m, sc_m, g_m = (mods[:, :, r] for r in range(6))
    rope = _rope_tables(positions)

    def moe_weights(l):
        sh_gu = jnp.concatenate([sh_gate[l], sh_up[l]], axis=-1).astype(BF16)
        return (l, router_w[l].T.astype(BF16), router_b[l].reshape(-1, 1), w_gate, w_up, w_down,
                sh_gu, sh_down[l].astype(BF16))

    h = _modulate(xf, _mod_rows(None, [(sh_a[0], sc_a[0])]), S)
    qkv_views = _mm(h, a_w_qkv[0].astype(BF16), B=B, S=S, head_major_out=True, rope=rope,
                    n_rope_cols=2 * D, dilations=DILATIONS[1:])
    o = _dilated_attention(*qkv_views)
    y = _mm(o, a_w_o[0].astype(BF16), B=B, S=S)
    xf, h32 = _ln(xf, y, _mod_rows(g_a[0], [(sh_m[0], sc_m[0])]), ln_g[0, 0], ln_b[0, 0],
                  ["u32"], S)
    y = _moe(h32, *moe_weights(0))
    xf, h_kv, h_q = _ln(xf, y, _mod_rows(g_m[0], [(kv_mods[:, 0], kv_mods[:, 1]),
                                                  (sh_a[1], sc_a[1])]),
                        ln_g[0, 1], ln_b[0, 1], ["bf16", "bf16"], S)

    kv = _mm(h_kv, b_w_kv.astype(BF16), B=B, S=S, head_major_out=True)
    q = _mm(h_q, b_w_q[0].astype(BF16), B=B, S=S, head_major_out=True)
    o = _stick_breaking_attention(q, kv)
    y = _mm(o, b_w_o[0].astype(BF16), B=B, S=S)
    xf, h32 = _ln(xf, y, _mod_rows(g_a[1], [(sh_m[1], sc_m[1])]), ln_g[1, 0], ln_b[1, 0],
                  ["u32"], S)
    y = _moe(h32, *moe_weights(1))
    (xf,) = _ln(xf, y, _mod_rows(g_m[1], []), ln_g[1, 1], ln_b[1, 1], [], S)
    return xf.reshape(B, S, D)
```

```python
import functools
import math

import jax
import jax.numpy as jnp
from jax import lax
from jax.experimental import pallas as pl
from jax.experimental.pallas import tpu as pltpu

F32 = jnp.float32
BF16 = jnp.bfloat16
U32 = jnp.uint32
I32 = jnp.int32

HEAD_DIM = 128
ROT_DIM = HEAD_DIM // 4
ROPE_THETA = 500000.0
DILATED_BRANCHES = ((128, 1), (512, 4), (2048, 16))
DILATIONS = tuple(d for _, d in DILATED_BRANCHES)
BLOCK = 128
TOP_K = 8
N_GROUPS = 8
TOPK_GROUPS = 4
ROUTED_SCALE = 2.5
DEPTH = 2
DEEPNORM_ALPHA = (2 * DEPTH) ** 0.25
LN_EPS = 1e-5

LANES = 128
VMEM_LIMIT_BYTES = 56 * 1024 * 1024
VMEM_LIMIT_BIG_TILES = 58 * 1024 * 1024

MM_TM = 1024
MM_TN = 512
LN_TM = 256
ROUTER_TM = 512
EXPERT_TM = 512
DISPATCH_TD = 128
COMBINE_TC = 128
SUPER = 2048
DIL_GROUP = 8
SB_TQ = 128
SB_GROUP = 8
SB_SKIP = 127.0
LOG2_E = 1.4426950408889634

NEG_INF = float("-inf")
HI_MASK = 0xFFFF0000


def _tile(pref, dim):
    t = min(pref, dim)
    assert dim % t == 0, (pref, dim)
    return t


def _cparams(*sem, vmem_bytes=VMEM_LIMIT_BYTES):
    return pltpu.CompilerParams(dimension_semantics=sem, vmem_limit_bytes=vmem_bytes)


def _dot(a, b):
    return jnp.dot(a, b, preferred_element_type=F32)


def _dot_nt(a, b):
    return lax.dot_general(a, b, (((1,), (1,)), ((), ())), preferred_element_type=F32)


def _unpack_pair(w):
    lo = lax.bitcast_convert_type(w << 16, F32).astype(BF16)
    hi = lax.bitcast_convert_type(w & jnp.uint32(HI_MASK), F32).astype(BF16)
    return lo, hi


def _unpack_pair_f32(w):
    lo = lax.bitcast_convert_type(w << 16, F32)
    hi = lax.bitcast_convert_type(w & jnp.uint32(HI_MASK), F32)
    return lo, hi


def _pack_pair(lo, hi):
    lo_b = lax.bitcast_convert_type(lo.astype(BF16).astype(F32), U32)
    hi_b = lax.bitcast_convert_type(hi.astype(BF16).astype(F32), U32)
    return (hi_b & jnp.uint32(HI_MASK)) | (lo_b >> 16)


def _load_slab_words(ref, n_tok):
    ch = ref.shape[0] // n_tok
    return jnp.concatenate([ref[pl.ds(c, n_tok, stride=ch), :] for c in range(ch)], axis=1)


def _store_slab_words(ref, words):
    n_tok = words.shape[0]
    ch = ref.shape[0] // n_tok
    for c in range(ch):
        ref[pl.ds(c, n_tok, stride=ch), :] = words[:, c * LANES:(c + 1) * LANES]


def _adaln_kernel(c_ref, w_ref, b_ref, o_ref, *, nk):
    k = pl.program_id(2)

    @pl.when(k == 0)
    def _():
        o_ref[...] = jnp.zeros_like(o_ref)

    cond = jax.nn.silu(c_ref[...])
    o_ref[0] += jnp.dot(cond, w_ref[0], preferred_element_type=F32,
                        precision=lax.Precision.HIGHEST)

    @pl.when(k == nk - 1)
    def _():
        o_ref[0] += b_ref[0]


def _adaln(c8, w, b):
    L, D, N = w.shape
    tk = _tile(1024, D)
    tn = _tile(2048, N)
    nk = D // tk
    return pl.pallas_call(
        functools.partial(_adaln_kernel, nk=nk),
        grid=(L, N // tn, nk),
        in_specs=[pl.BlockSpec((8, tk), lambda l, j, k: (0, k)),
                  pl.BlockSpec((1, tk, tn), lambda l, j, k: (l, k, j)),
                  pl.BlockSpec((1, 1, tn), lambda l, j, k: (l, 0, j))],
        out_specs=pl.BlockSpec((1, 8, tn), lambda l, j, k: (l, 0, j)),
        out_shape=jax.ShapeDtypeStruct((L, 8, N), F32),
        compiler_params=_cparams("parallel", "parallel", "arbitrary"),
        name="adaln",
    )(c8, w, b.reshape(L, 1, N))


def _rope_kernel(pos_ref, freq_ref, sign_ref, cos_ref, sin_ref):
    ang = pos_ref[...] * freq_ref[...]
    cos_ref[...] = jnp.cos(ang)
    sin_ref[...] = jnp.sin(ang) * sign_ref[...]


def _rope_tables(positions):
    T = positions.size
    half = ROT_DIM // 2
    inv_freq = ROPE_THETA ** (-jnp.arange(0, ROT_DIM, 2, dtype=F32) / ROT_DIM)
    freq = jnp.concatenate([inv_freq, inv_freq, jnp.zeros((HEAD_DIM - ROT_DIM,), F32)])[None, :]
    sign = jnp.concatenate([-jnp.ones((half,), F32), jnp.ones((HEAD_DIM - half,), F32)])[None, :]
    pos = jnp.broadcast_to(positions.astype(F32).reshape(T, 1), (T, HEAD_DIM))
    tm = _tile(2048, T)
    row = pl.BlockSpec((tm, HEAD_DIM), lambda i: (i, 0))
    vec = pl.BlockSpec((1, HEAD_DIM), lambda i: (0, 0))
    return pl.pallas_call(
        _rope_kernel,
        grid=(T // tm,),
        in_specs=[row, vec, vec],
        out_specs=[row, row],
        out_shape=[jax.ShapeDtypeStruct((T, HEAD_DIM), F32)] * 2,
        compiler_params=_cparams("parallel"),
        name="rope_tables",
    )(pos, freq, sign)


def _emit_mod(xn, rows_ref, idx, fmt, out_ref):
    shift = rows_ref[0, 1 + 2 * idx:2 + 2 * idx, :]
    scale = rows_ref[0, 2 + 2 * idx:3 + 2 * idx, :]
    h = xn * (1.0 + scale) + shift
    if fmt == "bf16":
        out_ref[...] = h.astype(BF16)
    else:
        half = h.shape[1] // 2
        _store_slab_words(out_ref, _pack_pair(h[:, :half], h[:, half:]))


def _modulate_kernel(x_ref, rows_ref, o_ref):
    _emit_mod(x_ref[...], rows_ref, 0, "bf16", o_ref)


def _ln_kernel(x_ref, y_ref, rows_ref, g_ref, b_ref, xo_ref, *h_refs, fmts):
    gate = rows_ref[0, 0:1, :]
    z = DEEPNORM_ALPHA * x_ref[...] + (1.0 + gate) * y_ref[...]
    mu = jnp.mean(z, axis=-1, keepdims=True)
    zc = z - mu
    var = jnp.mean(zc * zc, axis=-1, keepdims=True)
    xn = zc * lax.rsqrt(var + LN_EPS) * g_ref[...] + b_ref[...]
    xo_ref[...] = xn
    for idx, (fmt, h_ref) in enumerate(zip(fmts, h_refs)):
        _emit_mod(xn, rows_ref, idx, fmt, h_ref)


def _mod_rows(gate, pairs):
    B, D = pairs[0][0].shape if pairs else gate.shape
    rows = [gate if gate is not None else jnp.zeros((B, D), F32)]
    for shift, scale in pairs:
        rows += [shift, scale]
    rows += [jnp.zeros((B, D), F32)] * (8 - len(rows))
    return jnp.stack(rows, axis=1)


def _out_struct(T, D, tm, fmt):
    if fmt == "bf16":
        return jax.ShapeDtypeStruct((T, D), BF16), (tm, D)
    ch = D // 2 // LANES
    return jax.ShapeDtypeStruct((T * ch, LANES), U32), (tm * ch, LANES)


def _modulate(x, rows, S):
    T, D = x.shape
    tm = _tile(LN_TM, S)
    nb = S // tm
    return pl.pallas_call(
        _modulate_kernel,
        grid=(T // tm,),
        in_specs=[pl.BlockSpec((tm, D), lambda i: (i, 0)),
                  pl.BlockSpec((1, 8, D), lambda i: (i // nb, 0, 0))],
        out_specs=pl.BlockSpec((tm, D), lambda i: (i, 0)),
        out_shape=jax.ShapeDtypeStruct((T, D), BF16),
        compiler_params=_cparams("parallel"),
        name="modulate",
    )(x, rows)


def _ln(x, y, rows, g, b, fmts, S):
    T, D = x.shape
    tm = _tile(LN_TM, S)
    nb = S // tm
    row = pl.BlockSpec((tm, D), lambda i: (i, 0))
    vec = pl.BlockSpec((1, D), lambda i: (0, 0))
    out_shape = [jax.ShapeDtypeStruct((T, D), F32)]
    out_specs = [row]
    for fmt in fmts:
        st, blk = _out_struct(T, D, tm, fmt)
        out_shape.append(st)
        out_specs.append(pl.BlockSpec(blk, lambda i: (i, 0)))
    return pl.pallas_call(
        functools.partial(_ln_kernel, fmts=tuple(fmts)),
        grid=(T // tm,),
        in_specs=[row, row, pl.BlockSpec((1, 8, D), lambda i: (i // nb, 0, 0)), vec, vec],
        out_specs=out_specs,
        out_shape=out_shape,
        compiler_params=_cparams("parallel"),
        name="deepnorm_ln",
    )(x, y, rows, g.reshape(1, D), b.reshape(1, D))


def _swap_halves(x):
    half = ROT_DIM // 2
    lane = lax.broadcasted_iota(I32, x.shape, 1)
    return jnp.where(lane < half, pltpu.roll(x, HEAD_DIM - half, 1), pltpu.roll(x, half, 1))


def _mm_kernel(*refs, a_heads, out_heads, n_rope_tiles, dilations, col_axis):
    a_ref, w_ref = refs[:2]
    refs = refs[2:]
    if n_rope_tiles:
        cos_ref, sin_ref = refs[:2]
        refs = refs[2:]
    o_ref = refs[0]
    view_refs = refs[1:1 + len(dilations)]
    stage = refs[1 + len(dilations)] if dilations else None
    if a_heads:
        a = jnp.concatenate([a_ref[0, h] for h in range(a_heads)], axis=-1)
    else:
        a = a_ref[...]
    if not out_heads:
        o_ref[...] = _dot(a, w_ref[...])
        return
    tm = a.shape[0]
    heads_per_dot = min(out_heads, MM_TN // HEAD_DIM)

    def write(rot):
        for h0 in range(0, out_heads, heads_per_dot):
            acc = _dot(a, w_ref[:, h0 * HEAD_DIM:(h0 + heads_per_dot) * HEAD_DIM])
            for hh in range(h0, h0 + heads_per_dot):
                xh = acc[:, (hh - h0) * HEAD_DIM:(hh - h0 + 1) * HEAD_DIM]
                if rot:
                    xh = xh * cos_ref[...] + _swap_halves(xh) * sin_ref[...]
                o_ref[0, 0, hh] = xh.astype(BF16)
                if dilations:
                    stage[hh - h0] = xh
                    for d, v_ref in zip(dilations, view_refs):
                        for r in range(d):
                            v_ref[0, 0, hh, :, r * HEAD_DIM:(r + 1) * HEAD_DIM] = (
                                stage[hh - h0, pl.ds(r, tm // d, stride=d), :].astype(BF16))

    if not n_rope_tiles:
        write(False)
    else:
        j = pl.program_id(col_axis)

        @pl.when(j < n_rope_tiles)
        def _():
            write(True)

        @pl.when(j >= n_rope_tiles)
        def _():
            write(False)


def _mm(a, w, *, B, S, head_major_out=False, rope=None, n_rope_cols=0, dilations=()):
    K, N = w.shape
    T = B * S
    H = K // HEAD_DIM
    tm = _tile(MM_TM, S)
    tn = _tile(2 * MM_TN if dilations else MM_TN, N)
    nsb = S // tm
    cols_outer = bool(dilations)
    ix = (lambda f: (lambda j, i: f(i, j))) if cols_outer else (lambda f: f)
    a_heads = H if a.ndim == 4 else 0
    if a_heads:
        a_spec = pl.BlockSpec((1, H, tm, HEAD_DIM), ix(lambda i, j: (i // nsb, 0, i % nsb, 0)))
    else:
        a_spec = pl.BlockSpec((tm, K), ix(lambda i, j: (i, 0)))
    w_mode = dict(pipeline_mode=pl.Buffered(1)) if cols_outer else {}
    in_specs = [a_spec, pl.BlockSpec((K, tn), ix(lambda i, j: (0, j)), **w_mode)]
    args = [a, w]
    out_heads = tn // HEAD_DIM if head_major_out else 0
    n_rope_tiles = 0
    if head_major_out:
        D = H * HEAD_DIM
        G = N // D
        tiles_per_group = D // tn
        if rope is not None:
            assert n_rope_cols % tn == 0
            n_rope_tiles = n_rope_cols // tn
            tab = pl.BlockSpec((tm, HEAD_DIM), ix(lambda i, j: (i, 0)))
            in_specs += [tab, tab]
            args += list(rope)
        head_index = ix(lambda i, j: (j // tiles_per_group, i // nsb, j % tiles_per_group,
                                      i % nsb, 0))
        out_spec = [pl.BlockSpec((1, 1, out_heads, tm // d, d * HEAD_DIM), head_index)
                    for d in (1,) + tuple(dilations)]
        out_shape = [jax.ShapeDtypeStruct((G, B, H, S // d, d * HEAD_DIM), BF16)
                     for d in (1,) + tuple(dilations)]
    else:
        out_spec = pl.BlockSpec((tm, tn), ix(lambda i, j: (i, j)))
        out_shape = jax.ShapeDtypeStruct((T, N), F32)
    heads_per_dot = min(out_heads, MM_TN // HEAD_DIM)
    scratch = [pltpu.VMEM((heads_per_dot, tm, HEAD_DIM), F32)] if dilations else []
    out = pl.pallas_call(
        functools.partial(_mm_kernel, a_heads=a_heads, out_heads=out_heads,
                          n_rope_tiles=n_rope_tiles, dilations=tuple(dilations),
                          col_axis=0 if cols_outer else 1),
        grid=(N // tn, T // tm) if cols_outer else (T // tm, N // tn),
        in_specs=in_specs,
        out_specs=out_spec,
        out_shape=out_shape,
        scratch_shapes=scratch,
        compiler_params=_cparams("parallel", "parallel",
                                 vmem_bytes=VMEM_LIMIT_BIG_TILES if dilations else VMEM_LIMIT_BYTES),
        name="proj",
    )(*args)
    if head_major_out and not dilations:
        return out[0]
    return out


def _dil_attn_kernel(q1, q4, q16, k1c, k4c, k16c, k1p, k4p, k16p, v1c, v4c, v16c, v1p, v4p, v16p,
                     o_ref, *scratch, scale):
    acc_s, m_s, l_s = scratch[0:4], scratch[4:8], scratch[8:12]
    bias_s, s_s, p_s, mu_s, st_acc, st_m, st_l, out_s = scratch[12:20]

    qi = lax.broadcasted_iota(I32, (BLOCK, 2 * BLOCK), 0)
    kc = lax.broadcasted_iota(I32, (BLOCK, 2 * BLOCK), 1)
    band = jnp.where(kc >= qi, jnp.where(kc <= qi + BLOCK, 0.0, NEG_INF), NEG_INF).astype(F32)
    bias_s[0] = band
    no_prev = jnp.where(pl.program_id(2) == 0, NEG_INF, 0.0).astype(F32)
    bias_s[1] = jnp.where(kc < BLOCK, band + no_prev, band)
    ones = jnp.ones((2 * BLOCK, HEAD_DIM), BF16)

    def run(units):
        for g, (q, kp, kc_, _, _, first, _) in enumerate(units):
            keys = jnp.concatenate([kp(), kc_()], axis=0)
            s_s[g] = _dot_nt(q(), keys) * scale + bias_s[1 if first else 0]
        for g in range(len(units)):
            s = s_s[g]
            m = jnp.max(jnp.maximum(s[:, :BLOCK], s[:, BLOCK:]), axis=1, keepdims=True)
            p_s[g] = jnp.exp(s - m).astype(BF16)
            mu_s[g] = jnp.broadcast_to(m, (BLOCK, HEAD_DIM))
        for g, (_, _, _, vp, vc, _, sink) in enumerate(units):
            vals = jnp.concatenate([vp(), vc()], axis=0)
            pv = _dot(p_s[g], jnp.concatenate([vals, ones], axis=1))
            sink(g, pv[:, :HEAD_DIM], mu_s[g], pv[:, HEAD_DIM:])

    def fold(slab, rows, acc_u, m_u, l_u):
        m_o = m_s[slab][rows, :]
        m_n = jnp.maximum(m_o, m_u)
        e_o = jnp.exp(m_o - m_n)
        e_u = jnp.exp(m_u - m_n)
        acc_s[slab][rows, :] = acc_s[slab][rows, :] * e_o + acc_u * e_u
        l_s[slab][rows, :] = l_s[slab][rows, :] * e_o + l_u * e_u
        m_s[slab][rows, :] = m_n

    def tile(ref, rows, lanes):
        return lambda: ref[0, 0, 0, rows, lanes]

    def banded(q, kc_, kp, vc, vp, n, lanes, sink):
        cur = slice(n * BLOCK, (n + 1) * BLOCK)
        if n == 0:
            k_prev, v_prev = tile(kp, slice(None), lanes), tile(vp, slice(None), lanes)
        else:
            prv = slice((n - 1) * BLOCK, n * BLOCK)
            k_prev, v_prev = tile(kc_, prv, lanes), tile(vc, prv, lanes)
        return (tile(q, cur, lanes), k_prev, tile(kc_, cur, lanes), v_prev, tile(vc, cur, lanes),
                n == 0, sink)

    units = []
    for n in range(SUPER // 4 // BLOCK):
        for r in range(4):
            def init(g, acc, m, l, r=r, n=n):
                rows = slice(n * BLOCK, (n + 1) * BLOCK)
                acc_s[r][rows, :], m_s[r][rows, :], l_s[r][rows, :] = acc, m, l

            units.append(banded(q4, k4c, k4p, v4c, v4p, n,
                                slice(r * HEAD_DIM, (r + 1) * HEAD_DIM), init))

    for n in range(SUPER // BLOCK):
        def fold1(g, acc, m, l, n=n):
            st_acc[g], st_m[g], st_l[g] = acc, m, l
            rows = slice(n * (BLOCK // 4), (n + 1) * (BLOCK // 4))
            for r in range(4):
                pick = pl.ds(r, BLOCK // 4, stride=4)
                fold(r, rows, st_acc[g, pick, :], st_m[g, pick, :], st_l[g, pick, :])

        units.append(banded(q1, k1c, k1p, v1c, v1p, n, slice(None), fold1))

    for r in range(16):
        def fold16(g, acc, m, l, r=r):
            fold(r % 4, pl.ds(r // 4, BLOCK, stride=4), acc, m, l)

        units.append(banded(q16, k16c, k16p, v16c, v16p, 0,
                            slice(r * HEAD_DIM, (r + 1) * HEAD_DIM), fold16))

    for start in range(0, len(units), DIL_GROUP):
        run(units[start:start + DIL_GROUP])

    for r in range(4):
        out_s[pl.ds(r, SUPER // 4, stride=4), :] = acc_s[r][...] / l_s[r][...]
    o_ref[0, 0] = out_s[...].astype(BF16)


def _dilated_attention(qkv1, qkv4, qkv16):
    _, B, H, S, Dh = qkv1.shape
    assert Dh == HEAD_DIM and S % SUPER == 0
    assert tuple(d for _, d in DILATED_BRANCHES) == DILATIONS
    assert all(w // d == BLOCK for w, d in DILATED_BRANCHES)
    views = {1: qkv1, 4: qkv4, 16: qkv16}

    def cur(which, d):
        return pl.BlockSpec((1, 1, 1, SUPER // d, d * Dh), lambda b, h, m: (which, b, h, m, 0))

    def prev(which, d):
        per = SUPER // d // BLOCK
        return pl.BlockSpec((1, 1, 1, BLOCK, d * Dh),
                            lambda b, h, m: (which, b, h, jnp.maximum(m * per - 1, 0), 0))

    in_specs, args = [], []
    for make, which in ((cur, 0), (cur, 1), (prev, 1), (cur, 2), (prev, 2)):
        for d in (1, 4, 16):
            in_specs.append(make(which, d))
            args.append(views[d])
    state = [pltpu.VMEM((SUPER // 4, Dh), F32)] * 12
    stage = [pltpu.VMEM((2, BLOCK, 2 * BLOCK), F32),
             pltpu.VMEM((DIL_GROUP, BLOCK, 2 * BLOCK), F32),
             pltpu.VMEM((DIL_GROUP, BLOCK, 2 * BLOCK), BF16),
             pltpu.VMEM((DIL_GROUP, BLOCK, Dh), F32)] + [
             pltpu.VMEM((DIL_GROUP, BLOCK, Dh), F32)] * 3 + [
             pltpu.VMEM((SUPER, Dh), F32)]
    return pl.pallas_call(
        functools.partial(_dil_attn_kernel, scale=Dh ** -0.5),
        grid=(B, H, S // SUPER),
        in_specs=in_specs,
        out_specs=pl.BlockSpec((1, 1, SUPER, Dh), lambda b, h, m: (b, h, m, 0)),
        out_shape=jax.ShapeDtypeStruct((B, H, S, Dh), BF16),
        scratch_shapes=state + stage,
        compiler_params=_cparams("parallel", "parallel", "parallel"),
        name="dilated_attention",
    )(*args)


def _sb_kernel(q_ref, k_ref, v_ref, o_ref, tri_ref, acc_ref, carry_ref, logb_s, rsum_s, tail_s,
               hi_s, lo_s, a_s, *, scale, tq, nq):
    row = lax.broadcasted_iota(I32, (tq, tq), 0)
    col = lax.broadcasted_iota(I32, (tq, tq), 1)
    tri_ref[...] = jnp.where(row > col, 1.0, 0.0).astype(BF16)

    def block_rows(n):
        return pl.ds(pl.multiple_of(n * tq, tq), tq)

    def sweep(n0, i):
        diagonal = isinstance(i, int) and i == 0
        causal = col < row
        keys = [block_rows(jnp.maximum(n0 + g - i, 0)) for g in range(SB_GROUP)]
        for g in range(SB_GROUP):
            z2 = _dot_nt(q_ref[0, 0, block_rows(n0 + g), :], k_ref[0, 0, 0, keys[g], :])
            z2 = z2 * (scale * LOG2_E)
            t = jnp.log2(1.0 + jnp.exp2(-jnp.abs(z2)))
            log_1mb = -(jnp.maximum(z2, 0.0) + t)
            if diagonal:
                log_1mb = jnp.where(causal, log_1mb, 0.0)
            logb_s[g] = jnp.minimum(z2, 0.0) - t
            hi = log_1mb.astype(BF16)
            hi_s[g] = hi
            lo_s[g] = (log_1mb - hi.astype(F32)).astype(BF16)
            rsum_s[g] = jnp.broadcast_to(jnp.sum(log_1mb, axis=1, keepdims=True), (tq, tq))
        tail_s[...] = (_dot(hi_s[...].reshape(SB_GROUP * tq, tq), tri_ref[...])
                       + _dot(lo_s[...].reshape(SB_GROUP * tq, tq), tri_ref[...])
                       ).reshape(SB_GROUP, tq, tq)
        tops = []
        for g in range(SB_GROUP):
            tail = tail_s[g]
            if diagonal:
                a = jnp.where(causal, jnp.exp2(logb_s[g] + tail), 0.0)
                c = rsum_s[g]
            else:
                c = carry_ref[g]
                a = jnp.where(n0 + g - i >= 0, jnp.exp2(logb_s[g] + tail + c), 0.0)
                c = c + rsum_s[g]
            a_s[g] = a.astype(BF16)
            carry_ref[g] = c
            tops.append(jnp.where(n0 + g - i >= 1, jnp.max(c), NEG_INF))
        for g in range(SB_GROUP):
            av = _dot(a_s[g], v_ref[0, 0, 0, keys[g], :])
            acc_ref[g] = av if diagonal else acc_ref[g] + av
        return functools.reduce(jnp.maximum, tops)

    def qgroup(it, carry):
        n0 = it * SB_GROUP

        def cond(st):
            _, top = st
            return top > -SB_SKIP

        def body(st):
            i, _ = st
            return i + 1, sweep(n0, i)

        lax.while_loop(cond, body, (1, sweep(n0, 0)))
        for g in range(SB_GROUP):
            o_ref[0, 0, block_rows(n0 + g), :] = acc_ref[g].astype(BF16)
        return carry

    lax.fori_loop(0, nq // SB_GROUP, qgroup, 0)


def _stick_breaking_attention(q, kv):
    _, B, H, S, Dh = q.shape
    tq = _tile(SB_TQ, S)
    seq = lambda which: pl.BlockSpec((1, 1, 1, S, Dh), lambda b, h: (which, b, h, 0, 0))
    return pl.pallas_call(
        functools.partial(_sb_kernel, scale=Dh ** -0.5, tq=tq, nq=S // tq),
        grid=(B, H),
        in_specs=[pl.BlockSpec((1, 1, S, Dh), lambda b, h: (b, h, 0, 0)), seq(0), seq(1)],
        out_specs=pl.BlockSpec((1, 1, S, Dh), lambda b, h: (b, h, 0, 0)),
        out_shape=jax.ShapeDtypeStruct((B, H, S, Dh), BF16),
        scratch_shapes=[pltpu.VMEM((tq, tq), BF16), pltpu.VMEM((SB_GROUP, tq, Dh), F32)]
        + [pltpu.VMEM((SB_GROUP, tq, tq), F32)] * 4 + [pltpu.VMEM((SB_GROUP, tq, tq), BF16)] * 3,
        compiler_params=_cparams("parallel", "parallel"),
        name="stick_breaking_attention",
    )(q.reshape(B, H, S, Dh), kv, kv)


def _router_kernel(h_ref, rw_ref, rb_ref, e_ref, r_ref, w_ref, cnt_ref, tri_ref, carry_ref,
                   *, n_exp, tm, nsteps):
    i = pl.program_id(0)
    gsz = n_exp // N_GROUPS

    @pl.when(i == 0)
    def _():
        row = lax.broadcasted_iota(I32, (tm, tm), 0)
        col = lax.broadcasted_iota(I32, (tm, tm), 1)
        tri_ref[...] = jnp.where(row < col, 1.0, 0.0).astype(BF16)
        carry_ref[...] = jnp.zeros_like(carry_ref)

    lo, hi = _unpack_pair(_load_slab_words(h_ref, tm))
    half = lo.shape[1]
    logits = _dot_nt(rw_ref[:, :half], lo) + _dot_nt(rw_ref[:, half:], hi)
    scores = jax.nn.sigmoid(logits)
    sel = scores + rb_ref[...]

    sub = lax.broadcasted_iota(I32, (gsz, tm), 0)
    groups, gscore = [], []
    for g in range(N_GROUPS):
        sg = sel[g * gsz:(g + 1) * gsz, :]
        m1 = jnp.max(sg, axis=0, keepdims=True)
        i1 = jnp.min(jnp.where(sg == m1, sub, gsz), axis=0, keepdims=True)
        m2 = jnp.max(jnp.where(sub == i1, NEG_INF, sg), axis=0, keepdims=True)
        groups.append(sg)
        gscore.append(m1 + m2)
    masked = []
    for g in range(N_GROUPS):
        beat = jnp.zeros((1, tm), F32)
        for o in range(N_GROUPS):
            if o == g:
                continue
            beat += jnp.where(gscore[o] > gscore[g], 1.0, 0.0)
            if o < g:
                beat += jnp.where(gscore[o] == gscore[g], 1.0, 0.0)
        masked.append(jnp.where(beat < TOPK_GROUPS, groups[g], NEG_INF))
    x = jnp.concatenate(masked, axis=0)

    eid = lax.broadcasted_iota(I32, (n_exp, tm), 0)
    chosen = jnp.zeros((n_exp, tm), F32)
    picks, vals = [], []
    for _ in range(TOP_K):
        mx = jnp.max(x, axis=0, keepdims=True)
        idx = jnp.min(jnp.where(x == mx, eid, n_exp), axis=0, keepdims=True)
        hit = eid == idx
        picks.append(idx)
        vals.append(jnp.sum(jnp.where(hit, scores, 0.0), axis=0, keepdims=True))
        chosen += jnp.where(hit, 1.0, 0.0)
        x = jnp.where(hit, NEG_INF, x)
    total = vals[0]
    for v in vals[1:]:
        total = total + v

    rank_all = _dot(chosen.astype(BF16), tri_ref[...]) + carry_ref[...]
    for k in range(TOP_K):
        e_ref[k:k + 1, :] = picks[k]
        w_ref[k:k + 1, :] = vals[k] / total * ROUTED_SCALE
        rk = jnp.sum(jnp.where(eid == picks[k], rank_all, 0.0), axis=0, keepdims=True)
        r_ref[k:k + 1, :] = rk.astype(I32)
    carry_ref[...] += jnp.broadcast_to(jnp.sum(chosen, axis=1, keepdims=True), (n_exp, tm))

    @pl.when(i == nsteps - 1)
    def _():
        cnt_ref[...] = carry_ref[:, 0:LANES].astype(I32)


def _router(h32, rw_t, rb):
    E, D = rw_t.shape
    ch = D // 2 // LANES
    T = h32.shape[0] // ch
    tm = _tile(ROUTER_TM, T)
    nsteps = T // tm
    tok = pl.BlockSpec((TOP_K, tm), lambda i: (0, i))
    return pl.pallas_call(
        functools.partial(_router_kernel, n_exp=E, tm=tm, nsteps=nsteps),
        grid=(nsteps,),
        in_specs=[pl.BlockSpec((tm * ch, LANES), lambda i: (i, 0)),
                  pl.BlockSpec((E, D), lambda i: (0, 0)),
                  pl.BlockSpec((E, 1), lambda i: (0, 0))],
        out_specs=[tok, tok, tok, pl.BlockSpec((E, LANES), lambda i: (0, 0))],
        out_shape=[jax.ShapeDtypeStruct((TOP_K, T), I32), jax.ShapeDtypeStruct((TOP_K, T), I32),
                   jax.ShapeDtypeStruct((TOP_K, T), F32), jax.ShapeDtypeStruct((E, LANES), I32)],
        scratch_shapes=[pltpu.VMEM((tm, tm), BF16), pltpu.VMEM((E, tm), F32)],
        compiler_params=_cparams("arbitrary"),
        name="moe_router",
    )(h32, rw_t, rb)


def _dispatch_kernel(last_ref, slots_ref, h_ref, xs_hbm, zbuf, zsem, sem, *, n_exp, td, ch):
    i = pl.program_id(0)
    tile_rows = zbuf.shape[0]

    def zero_copy(e):
        dst = pl.ds(pl.multiple_of(last_ref[e], tile_rows), tile_rows)
        return pltpu.make_async_copy(zbuf, xs_hbm.at[dst], zsem)

    @pl.when(i == 0)
    def _():
        zbuf[...] = jnp.zeros_like(zbuf)

        def start(e, c):
            @pl.when(last_ref[e] >= 0)
            def _():
                zero_copy(e).start()
            return c

        def wait(e, c):
            @pl.when(last_ref[e] >= 0)
            def _():
                zero_copy(e).wait()
            return c

        lax.fori_loop(0, n_exp, start, 0)
        lax.fori_loop(0, n_exp, wait, 0)

    def row_copy(t, k):
        src = pl.ds(pl.multiple_of(t * ch, ch), ch)
        dst = pl.ds(pl.multiple_of(slots_ref[k, t], ch), ch)
        return pltpu.make_async_copy(h_ref.at[src], xs_hbm.at[dst], sem)

    def start_rows(t, c):
        for k in range(TOP_K):
            row_copy(t, k).start(priority=k % 2)
        return c

    def wait_rows(t, c):
        for k in range(TOP_K):
            row_copy(t, k).wait()
        return c

    lax.fori_loop(0, td, start_rows, 0)
    lax.fori_loop(0, td, wait_rows, 0)


def _dispatch(h32, slot_rows, last_rows, n_slots, tm_e, ch):
    T = slot_rows.shape[1]
    td = _tile(DISPATCH_TD, T)
    E = last_rows.shape[0]
    return pl.pallas_call(
        functools.partial(_dispatch_kernel, n_exp=E, td=td, ch=ch),
        grid_spec=pltpu.PrefetchScalarGridSpec(
            num_scalar_prefetch=1,
            grid=(T // td,),
            in_specs=[pl.BlockSpec((TOP_K, td), lambda i, last: (0, i), memory_space=pltpu.SMEM),
                      pl.BlockSpec((td * ch, LANES), lambda i, last: (i, 0))],
            out_specs=pl.BlockSpec(memory_space=pl.ANY),
            scratch_shapes=[pltpu.VMEM((tm_e * ch, LANES), U32), pltpu.SemaphoreType.DMA,
                            pltpu.SemaphoreType.DMA]),
        out_shape=jax.ShapeDtypeStruct((n_slots * ch, LANES), U32),
        compiler_params=_cparams("arbitrary"),
        name="moe_dispatch",
    )(last_rows, slot_rows, h32)


def _mlp_hidden(x_ref, wgu_ref, n_tok):
    f = wgu_ref.shape[1] // 2
    lo, hi = _unpack_pair(_load_slab_words(x_ref, n_tok))
    half = lo.shape[1]
    gu = _dot(lo, wgu_ref[:half, :]) + _dot(hi, wgu_ref[half:, :])
    return (jax.nn.silu(gu[:, :f]) * gu[:, f:]).astype(BF16)


def _expert_kernel(te_ref, nx_ref, nt_ref, x_ref, wg_hbm, wu_hbm, wd_hbm, y_ref,
                   wg_f, wu_f, wd_f, wgu_s, wd_s, sems, *, tm_e, layer):
    i = pl.program_id(0)
    f, d_model = wd_s.shape
    half = d_model // 2
    ch = half // LANES

    def weight_copies(e):
        return (pltpu.make_async_copy(wg_hbm.at[layer, e], wg_f, sems.at[0]),
                pltpu.make_async_copy(wu_hbm.at[layer, e], wu_f, sems.at[1]),
                pltpu.make_async_copy(wd_hbm.at[layer, e], wd_f, sems.at[2]))

    @pl.when(i < nt_ref[0])
    def _():
        expert = te_ref[i]

        @pl.when(i == 0)
        def _():
            for cp in weight_copies(expert):
                cp.start()

        @pl.when(jnp.logical_or(i == 0, expert != te_ref[jnp.maximum(i - 1, 0)]))
        def _():
            for cp in weight_copies(expert):
                cp.wait()
            wgu_s[:, :f] = wg_f[...].astype(BF16)
            wgu_s[:, f:] = wu_f[...].astype(BF16)
            for c in range(ch):
                lo_cols = slice(c * LANES, (c + 1) * LANES)
                hi_cols = slice(half + c * LANES, half + (c + 1) * LANES)
                wd_s[:, 2 * c * LANES:(2 * c + 1) * LANES] = wd_f[:, lo_cols].astype(BF16)
                wd_s[:, (2 * c + 1) * LANES:(2 * c + 2) * LANES] = wd_f[:, hi_cols].astype(BF16)

            @pl.when(nx_ref[i] >= 0)
            def _():
                for cp in weight_copies(nx_ref[i]):
                    cp.start()

        a = _mlp_hidden(x_ref, wgu_s, tm_e)
        for c in range(ch):
            yc = _dot(a, wd_s[:, 2 * c * LANES:(2 * c + 2) * LANES])
            y_ref[pl.ds(c, tm_e, stride=ch), :] = _pack_pair(yc[:, :LANES], yc[:, LANES:])


def _shared_kernel(x_ref, wgu_ref, wd_ref, y_ref, *, tm):
    y_ref[...] = _dot(_mlp_hidden(x_ref, wgu_ref, tm), wd_ref[...])


def _experts(xs, layer, w_gate, w_up, w_down, tile_expert, next_expert, n_tiles, tm_e, ch):
    _, E, D, F = w_gate.shape
    n_slots = xs.shape[0] // ch
    rows = pl.BlockSpec((tm_e * ch, LANES),
                        lambda i, te, nx, nt: (jnp.minimum(i, nt[0] - 1), 0))
    hbm = pl.BlockSpec(memory_space=pl.ANY)
    return pl.pallas_call(
        functools.partial(_expert_kernel, tm_e=tm_e, layer=layer),
        grid_spec=pltpu.PrefetchScalarGridSpec(
            num_scalar_prefetch=3,
            grid=(n_slots // tm_e,),
            in_specs=[rows, hbm, hbm, hbm],
            out_specs=rows,
            scratch_shapes=[pltpu.VMEM((D, F), F32), pltpu.VMEM((D, F), F32),
                            pltpu.VMEM((F, D), F32), pltpu.VMEM((D, 2 * F), BF16),
                            pltpu.VMEM((F, D), BF16), pltpu.SemaphoreType.DMA((3,))]),
        out_shape=jax.ShapeDtypeStruct(xs.shape, U32),
        compiler_params=_cparams("arbitrary"),
        name="moe_experts",
    )(tile_expert, next_expert, n_tiles, xs, w_gate, w_up, w_down)


def _shared_expert(h32, wgu, wd, ch):
    D, F2 = wgu.shape
    T = h32.shape[0] // ch
    tm = _tile(512, T)
    return pl.pallas_call(
        functools.partial(_shared_kernel, tm=tm),
        grid=(T // tm,),
        in_specs=[pl.BlockSpec((tm * ch, LANES), lambda i: (i, 0)),
                  pl.BlockSpec((D, F2), lambda i: (0, 0)),
                  pl.BlockSpec((F2 // 2, D), lambda i: (0, 0))],
        out_specs=pl.BlockSpec((tm, D), lambda i: (i, 0)),
        out_shape=jax.ShapeDtypeStruct((T, D), F32),
        compiler_params=_cparams("parallel"),
        name="moe_shared_expert",
    )(h32, wgu, wd)


def _combine_kernel(slots_ref, next_slots_ref, w_ref, ysh_ref, y_hbm, o_ref, bufs, gate_s, sems,
                    *, tc, ch, nsteps):
    i = pl.program_id(0)
    slot = i % 2

    def row_copy(table, which, t, k):
        src = pl.ds(pl.multiple_of(table[k, t], ch), ch)
        dst = pl.ds(pl.multiple_of(t * ch, ch), ch)
        return pltpu.make_async_copy(y_hbm.at[src], bufs.at[which, k, dst], sems.at[which])

    def gather_token(table, which, t):
        for k in range(TOP_K):
            row_copy(table, which, t, k).start(priority=k % 2)

    def wait_all(table, which):
        def wait_rows(t, c):
            for k in range(TOP_K):
                row_copy(table, which, t, k).wait()
            return c

        lax.fori_loop(0, tc, wait_rows, 0)

    @pl.when(i == 0)
    def _():
        def start_rows(t, c):
            gather_token(slots_ref, 0, t)
            return c

        lax.fori_loop(0, tc, start_rows, 0)

    wait_all(slots_ref, slot)

    half = ch * LANES
    w_cols = w_ref[...].T
    for k in range(TOP_K):
        gate_s[k] = jnp.broadcast_to(w_cols[:, k:k + 1], (tc, LANES))
    per_chunk = tc // ch
    for c in range(ch):
        lanes_lo = slice(c * LANES, (c + 1) * LANES)
        lanes_hi = slice(half + c * LANES, half + (c + 1) * LANES)
        acc_lo = ysh_ref[:, lanes_lo]
        acc_hi = ysh_ref[:, lanes_hi]
        for k in range(TOP_K):
            lo, hi = _unpack_pair_f32(bufs.at[slot, k][pl.ds(c, tc, stride=ch), :])
            acc_lo = acc_lo + gate_s[k] * lo
            acc_hi = acc_hi + gate_s[k] * hi
        o_ref[:, lanes_lo] = acc_lo
        o_ref[:, lanes_hi] = acc_hi
        for t in range(c * per_chunk, (c + 1) * per_chunk):
            gather_token(next_slots_ref, 1 - slot, t)

    @pl.when(i == nsteps - 1)
    def _():
        wait_all(next_slots_ref, 1 - slot)


def _combine(slot_rows, wts, ysh, ys, ch):
    T, D = ysh.shape
    tc = _tile(COMBINE_TC, T)
    nsteps = T // tc
    return pl.pallas_call(
        functools.partial(_combine_kernel, tc=tc, ch=ch, nsteps=nsteps),
        grid=(nsteps,),
        in_specs=[pl.BlockSpec((TOP_K, tc), lambda i: (0, i), memory_space=pltpu.SMEM),
                  pl.BlockSpec((TOP_K, tc), lambda i: (0, jnp.minimum(i + 1, nsteps - 1)),
                               memory_space=pltpu.SMEM),
                  pl.BlockSpec((TOP_K, tc), lambda i: (0, i)),
                  pl.BlockSpec((tc, D), lambda i: (i, 0)),
                  pl.BlockSpec(memory_space=pl.ANY)],
        out_specs=pl.BlockSpec((tc, D), lambda i: (i, 0)),
        out_shape=jax.ShapeDtypeStruct((T, D), F32),
        scratch_shapes=[pltpu.VMEM((2, TOP_K, tc * ch, LANES), U32),
                        pltpu.VMEM((TOP_K, tc, LANES), F32),
                        pltpu.SemaphoreType.DMA((2,))],
        compiler_params=_cparams("arbitrary"),
        name="moe_combine",
    )(slot_rows, slot_rows, wts, ysh, ys)


def _moe(h32, layer, rw_t, rb, w_gate, w_up, w_down, sh_gu, sh_d):
    E, D = rw_t.shape
    ch = D // 2 // LANES
    assert ch % 8 == 0, "a token slab must be whole (8,128) tiles"
    T = h32.shape[0] // ch
    tm_e = _tile(EXPERT_TM, T)
    eidx, rank, wts, counts = _router(h32, rw_t, rb)

    counts = counts[:, 0]
    tiles = (counts + tm_e - 1) // tm_e
    tile_end = jnp.cumsum(tiles)
    tile_start = tile_end - tiles
    n_tiles = tile_end[-1]
    experts = jnp.arange(E, dtype=I32)
    first_row = jnp.sum(jnp.where(eidx[:, :, None] == experts, tile_start * tm_e, 0), axis=-1)
    slot_rows = (first_row + rank) * ch
    max_tiles = T * TOP_K // tm_e + E
    tile_ids = jnp.minimum(jnp.arange(max_tiles, dtype=I32), n_tiles - 1)
    tile_expert = jnp.sum((tile_ids[:, None] >= tile_end[None, :]).astype(I32), axis=-1)
    group_end = jnp.sum(jnp.where(tile_expert[:, None] == experts, tile_end, 0), axis=-1)
    after = jnp.sum((group_end[:, None] >= tile_end[None, :]).astype(I32), axis=-1)
    next_expert = jnp.where(group_end < n_tiles, after, -1).astype(I32)
    last_rows = jnp.where(tiles > 0, (tile_end - 1) * (tm_e * ch), -1).astype(I32)

    xs = _dispatch(h32, slot_rows, last_rows, max_tiles * tm_e, tm_e, ch)
    ys = _experts(xs, layer, w_gate, w_up, w_down, tile_expert, next_expert,
                  n_tiles.reshape(1).astype(I32), tm_e, ch)
    ysh = _shared_expert(h32, sh_gu, sh_d, ch)
    return _combine(slot_rows, wts, ysh, ys, ch)


def kernel(x, c, positions, ada_w, ada_b, ln_g, ln_b, a_w_qkv, a_w_o, kv_ada_w, kv_ada_b, b_w_kv, b_w_q, b_w_o, router_w, router_b, w_gate, w_up, w_down, sh_gate, sh_up, sh_down):
    B, S, D = x.shape
    T = B * S
    assert D % HEAD_DIM == 0 and DEPTH == ada_w.shape[0] == 2
    xf = x.reshape(T, D)

    c8 = jnp.zeros((8, D), F32).at[:B].set(c)
    mods = _adaln(c8, ada_w, ada_b)[:, :B].reshape(DEPTH, B, 6, D)
    kv_mods = _adaln(c8, kv_ada_w[None], kv_ada_b[None])[0, :B].reshape(B, 2, D)
    sh_a, sc_a, g_a, sh_m, sc_m, g_m = (mods[:, :, r] for r in range(6))
    rope = _rope_tables(positions)

    def moe_weights(l):
        sh_gu = jnp.concatenate([sh_gate[l], sh_up[l]], axis=-1).astype(BF16)
        return (l, router_w[l].T.astype(BF16), router_b[l].reshape(-1, 1), w_gate, w_up, w_down,
                sh_gu, sh_down[l].astype(BF16))

    h = _modulate(xf, _mod_rows(None, [(sh_a[0], sc_a[0])]), S)
    qkv_views = _mm(h, a_w_qkv[0].astype(BF16), B=B, S=S, head_major_out=True, rope=rope,
                    n_rope_cols=2 * D, dilations=DILATIONS[1:])
    o = _dilated_attention(*qkv_views)
    y = _mm(o, a_w_o[0].astype(BF16), B=B, S=S)
    xf, h32 = _ln(xf, y, _mod_rows(g_a[0], [(sh_m[0], sc_m[0])]), ln_g[0, 0], ln_b[0, 0],
                  ["u32"], S)
    y = _moe(h32, *moe_weights(0))
    xf, h_kv, h_q = _ln(xf, y, _mod_rows(g_m[0], [(kv_mods[:, 0], kv_mods[:, 1]),
                                                  (sh_a[1], sc_a[1])]),
                        ln_g[0, 1], ln_b[0, 1], ["bf16", "bf16"], S)

    kv = _mm(h_kv, b_w_kv.astype(BF16), B=B, S=S, head_major_out=True)
    q = _mm(h_q, b_w_q[0].astype(BF16), B=B, S=S, head_major_out=True)
    o = _stick_breaking_attention(q, kv)
    y = _mm(o, b_w_o[0].astype(BF16), B=B, S=S)
    xf, h32 = _ln(xf, y, _mod_rows(g_a[1], [(sh_m[1], sc_m[1])]), ln_g[1, 0], ln_b[1, 0],
                  ["u32"], S)
    y = _moe(h32, *moe_weights(1))
    (xf,) = _ln(xf, y, _mod_rows(g_m[1], []), ln_g[1, 1], ln_b[1, 1], [], S)
    return xf.reshape(B, S, D)
```

```python
import functools
import math

import jax
import jax.numpy as jnp
from jax import lax
from jax.experimental import pallas as pl
from jax.experimental.pallas import tpu as pltpu

F32 = jnp.float32
BF16 = jnp.bfloat16
U32 = jnp.uint32
I32 = jnp.int32

HEAD_DIM = 128
ROT_DIM = HEAD_DIM // 4
ROPE_THETA = 500000.0
DILATED_BRANCHES = ((128, 1), (512, 4), (2048, 16))
DILATIONS = tuple(d for _, d in DILATED_BRANCHES)
BLOCK = 128
TOP_K = 8
N_GROUPS = 8
TOPK_GROUPS = 4
ROUTED_SCALE = 2.5
DEPTH = 2
DEEPNORM_ALPHA = (2 * DEPTH) ** 0.25
LN_EPS = 1e-5

LANES = 128
VMEM_LIMIT_BYTES = 56 * 1024 * 1024
VMEM_LIMIT_BIG_TILES = 58 * 1024 * 1024

MM_TM = 1024
MM_TN = 512
LN_TM = 256
ROUTER_TM = 512
EXPERT_TM = 512
DISPATCH_TD = 128
COMBINE_TC = 128
SUPER = 2048
DIL_GROUP = 8
SB_TQ = 128
SB_GROUP = 8
SB_SKIP = 127.0
LOG2_E = 1.4426950408889634

NEG_INF = float("-inf")
HI_MASK = 0xFFFF0000


def _tile(pref, dim):
    t = min(pref, dim)
    assert dim % t == 0, (pref, dim)
    return t


def _cparams(*sem, vmem_bytes=VMEM_LIMIT_BYTES):
    return pltpu.CompilerParams(dimension_semantics=sem, vmem_limit_bytes=vmem_bytes)


def _dot(a, b):
    return jnp.dot(a, b, preferred_element_type=F32)


def _dot_nt(a, b):
    return lax.dot_general(a, b, (((1,), (1,)), ((), ())), preferred_element_type=F32)


def _unpack_pair(w):
    lo = lax.bitcast_convert_type(w << 16, F32).astype(BF16)
    hi = lax.bitcast_convert_type(w & jnp.uint32(HI_MASK), F32).astype(BF16)
    return lo, hi


def _unpack_pair_f32(w):
    lo = lax.bitcast_convert_type(w << 16, F32)
    hi = lax.bitcast_convert_type(w & jnp.uint32(HI_MASK), F32)
    return lo, hi


def _pack_pair(lo, hi):
    lo_b = lax.bitcast_convert_type(lo.astype(BF16).astype(F32), U32)
    hi_b = lax.bitcast_convert_type(hi.astype(BF16).astype(F32), U32)
    return (hi_b & jnp.uint32(HI_MASK)) | (lo_b >> 16)


def _load_slab_words(ref, n_tok):
    ch = ref.shape[0] // n_tok
    return jnp.concatenate([ref[pl.ds(c, n_tok, stride=ch), :] for c in range(ch)], axis=1)


def _store_slab_words(ref, words):
    n_tok = words.shape[0]
    ch = ref.shape[0] // n_tok
    for c in range(ch):
        ref[pl.ds(c, n_tok, stride=ch), :] = words[:, c * LANES:(c + 1) * LANES]


def _adaln_kernel(c_ref, w_ref, b_ref, o_ref, *, nk):
    k = pl.program_id(2)

    @pl.when(k == 0)
    def _():
        o_ref[...] = jnp.zeros_like(o_ref)

    cond = jax.nn.silu(c_ref[...])
    o_ref[0] += jnp.dot(cond, w_ref[0], preferred_element_type=F32,
                        precision=lax.Precision.HIGHEST)

    @pl.when(k == nk - 1)
    def _():
        o_ref[0] += b_ref[0]


def _adaln(c8, w, b):
    L, D, N = w.shape
    tk = _tile(1024, D)
    tn = _tile(2048, N)
    nk = D // tk
    return pl.pallas_call(
        functools.partial(_adaln_kernel, nk=nk),
        grid=(L, N // tn, nk),
        in_specs=[pl.BlockSpec((8, tk), lambda l, j, k: (0, k)),
                  pl.BlockSpec((1, tk, tn), lambda l, j, k: (l, k, j)),
                  pl.BlockSpec((1, 1, tn), lambda l, j, k: (l, 0, j))],
        out_specs=pl.BlockSpec((1, 8, tn), lambda l, j, k: (l, 0, j)),
        out_shape=jax.ShapeDtypeStruct((L, 8, N), F32),
        compiler_params=_cparams("parallel", "parallel", "arbitrary"),
        name="adaln",
    )(c8, w, b.reshape(L, 1, N))


def _rope_kernel(pos_ref, freq_ref, sign_ref, cos_ref, sin_ref):
    ang = pos_ref[...] * freq_ref[...]
    cos_ref[...] = jnp.cos(ang)
    sin_ref[...] = jnp.sin(ang) * sign_ref[...]


def _rope_tables(positions):
    T = positions.size
    half = ROT_DIM // 2
    inv_freq = ROPE_THETA ** (-jnp.arange(0, ROT_DIM, 2, dtype=F32) / ROT_DIM)
    freq = jnp.concatenate([inv_freq, inv_freq, jnp.zeros((HEAD_DIM - ROT_DIM,), F32)])[None, :]
    sign = jnp.concatenate([-jnp.ones((half,), F32), jnp.ones((HEAD_DIM - half,), F32)])[None, :]
    pos = jnp.broadcast_to(positions.astype(F32).reshape(T, 1), (T, HEAD_DIM))
    tm = _tile(2048, T)
    row = pl.BlockSpec((tm, HEAD_DIM), lambda i: (i, 0))
    vec = pl.BlockSpec((1, HEAD_DIM), lambda i: (0, 0))
    return pl.pallas_call(
        _rope_kernel,
        grid=(T // tm,),
        in_specs=[row, vec, vec],
        out_specs=[row, row],
        out_shape=[jax.ShapeDtypeStruct((T, HEAD_DIM), F32)] * 2,
        compiler_params=_cparams("parallel"),
        name="rope_tables",
    )(pos, freq, sign)


def _emit_mod(xn, rows_ref, idx, fmt, out_ref):
    shift = rows_ref[0, 1 + 2 * idx:2 + 2 * idx, :]
    scale = rows_ref[0, 2 + 2 * idx:3 + 2 * idx, :]
    h = xn * (1.0 + scale) + shift
    if fmt == "bf16":
        out_ref[...] = h.astype(BF16)
    else:
        half = h.shape[1] // 2
        _store_slab_words(out_ref, _pack_pair(h[:, :half], h[:, half:]))


def _modulate_kernel(x_ref, rows_ref, o_ref):
    _emit_mod(x_ref[...], rows_ref, 0, "bf16", o_ref)


def _ln_kernel(x_ref, y_ref, rows_ref, g_ref, b_ref, xo_ref, *h_refs, fmts):
    gate = rows_ref[0, 0:1, :]
    z = DEEPNORM_ALPHA * x_ref[...] + (1.0 + gate) * y_ref[...]
    mu = jnp.mean(z, axis=-1, keepdims=True)
    zc = z - mu
    var = jnp.mean(zc * zc, axis=-1, keepdims=True)
    xn = zc * lax.rsqrt(var + LN_EPS) * g_ref[...] + b_ref[...]
    xo_ref[...] = xn
    for idx, (fmt, h_ref) in enumerate(zip(fmts, h_refs)):
        _emit_mod(xn, rows_ref, idx, fmt, h_ref)


def _mod_rows(gate, pairs):
    B, D = pairs[0][0].shape if pairs else gate.shape
    rows = [gate if gate is not None else jnp.zeros((B, D), F32)]
    for shift, scale in pairs:
        rows += [shift, scale]
    rows += [jnp.zeros((B, D), F32)] * (8 - len(rows))
    return jnp.stack(rows, axis=1)


def _out_struct(T, D, tm, fmt):
    if fmt == "bf16":
        return jax.ShapeDtypeStruct((T, D), BF16), (tm, D)
    ch = D // 2 // LANES
    return jax.ShapeDtypeStruct((T * ch, LANES), U32), (tm * ch, LANES)


def _modulate(x, rows, S):
    T, D = x.shape
    tm = _tile(LN_TM, S)
    nb = S // tm
    return pl.pallas_call(
        _modulate_kernel,
        grid=(T // tm,),
        in_specs=[pl.BlockSpec((tm, D), lambda i: (i, 0)),
                  pl.BlockSpec((1, 8, D), lambda i: (i // nb, 0, 0))],
        out_specs=pl.BlockSpec((tm, D), lambda i: (i, 0)),
        out_shape=jax.ShapeDtypeStruct((T, D), BF16),
        compiler_params=_cparams("parallel"),
        name="modulate",
    )(x, rows)


def _ln(x, y, rows, g, b, fmts, S):
    T, D = x.shape
    tm = _tile(LN_TM, S)
    nb = S // tm
    row = pl.BlockSpec((tm, D), lambda i: (i, 0))
    vec = pl.BlockSpec((1, D), lambda i: (0, 0))
    out_shape = [jax.ShapeDtypeStruct((T, D), F32)]
    out_specs = [row]
    for fmt in fmts:
        st, blk = _out_struct(T, D, tm, fmt)
        out_shape.append(st)
        out_specs.append(pl.BlockSpec(blk, lambda i: (i, 0)))
    return pl.pallas_call(
        functools.partial(_ln_kernel, fmts=tuple(fmts)),
        grid=(T // tm,),
        in_specs=[row, row, pl.BlockSpec((1, 8, D), lambda i: (i // nb, 0, 0)), vec, vec],
        out_specs=out_specs,
        out_shape=out_shape,
        compiler_params=_cparams("parallel"),
        name="deepnorm_ln",
    )(x, y, rows, g.reshape(1, D), b.reshape(1, D))


def _swap_halves(x):
    half = ROT_DIM // 2
    lane = lax.broadcasted_iota(I32, x.shape, 1)
    return jnp.where(lane < half, pltpu.roll(x, HEAD_DIM - half, 1), pltpu.roll(x, half, 1))


def _mm_kernel(*refs, a_heads, out_heads, n_rope_tiles, dilations, col_axis):
    a_ref, w_ref = refs[:2]
    refs = refs[2:]
    if n_rope_tiles:
        cos_ref, sin_ref = refs[:2]
        refs = refs[2:]
    o_ref = refs[0]
    view_refs = refs[1:1 + len(dilations)]
    stage = refs[1 + len(dilations)] if dilations else None
    if a_heads:
        a = jnp.concatenate([a_ref[0, h] for h in range(a_heads)], axis=-1)
    else:
        a = a_ref[...]
    if not out_heads:
        o_ref[...] = _dot(a, w_ref[...])
        return
    tm = a.shape[0]
    heads_per_dot = min(out_heads, MM_TN // HEAD_DIM)

    def write(rot):
        for h0 in range(0, out_heads, heads_per_dot):
            acc = _dot(a, w_ref[:, h0 * HEAD_DIM:(h0 + heads_per_dot) * HEAD_DIM])
            for hh in range(h0, h0 + heads_per_dot):
                xh = acc[:, (hh - h0) * HEAD_DIM:(hh - h0 + 1) * HEAD_DIM]
                if rot:
                    xh = xh * cos_ref[...] + _swap_halves(xh) * sin_ref[...]
                o_ref[0, 0, hh] = xh.astype(BF16)
                if dilations:
                    stage[hh - h0] = xh
                    for d, v_ref in zip(dilations, view_refs):
                        for r in range(d):
                            v_ref[0, 0, hh, :, r * HEAD_DIM:(r + 1) * HEAD_DIM] = (
                                stage[hh - h0, pl.ds(r, tm // d, stride=d), :].astype(BF16))

    if not n_rope_tiles:
        write(False)
    else:
        j = pl.program_id(col_axis)

        @pl.when(j < n_rope_tiles)
        def _():
            write(True)

        @pl.when(j >= n_rope_tiles)
        def _():
            write(False)


def _mm(a, w, *, B, S, head_major_out=False, rope=None, n_rope_cols=0, dilations=()):
    K, N = w.shape
    T = B * S
    H = K // HEAD_DIM
    tm = _tile(MM_TM if dilations else 2 * MM_TM, S)
    tn = _tile(2 * MM_TN if dilations else MM_TN, N)
    nsb = S // tm
    cols_outer = bool(dilations)
    ix = (lambda f: (lambda j, i: f(i, j))) if cols_outer else (lambda f: f)
    a_heads = H if a.ndim == 4 else 0
    a_mode = {} if cols_outer else dict(pipeline_mode=pl.Buffered(1))
    w_mode = dict(pipeline_mode=pl.Buffered(1)) if cols_outer else {}
    if a_heads:
        a_spec = pl.BlockSpec((1, H, tm, HEAD_DIM), ix(lambda i, j: (i // nsb, 0, i % nsb, 0)),
                              **a_mode)
    else:
        a_spec = pl.BlockSpec((tm, K), ix(lambda i, j: (i, 0)), **a_mode)
    in_specs = [a_spec, pl.BlockSpec((K, tn), ix(lambda i, j: (0, j)), **w_mode)]
    args = [a, w]
    out_heads = tn // HEAD_DIM if head_major_out else 0
    n_rope_tiles = 0
    if head_major_out:
        D = H * HEAD_DIM
        G = N // D
        tiles_per_group = D // tn
        if rope is not None:
            assert n_rope_cols % tn == 0
            n_rope_tiles = n_rope_cols // tn
            tab = pl.BlockSpec((tm, HEAD_DIM), ix(lambda i, j: (i, 0)))
            in_specs += [tab, tab]
            args += list(rope)
        head_index = ix(lambda i, j: (j // tiles_per_group, i // nsb, j % tiles_per_group,
                                      i % nsb, 0))
        out_spec = [pl.BlockSpec((1, 1, out_heads, tm // d, d * HEAD_DIM), head_index)
                    for d in (1,) + tuple(dilations)]
        out_shape = [jax.ShapeDtypeStruct((G, B, H, S // d, d * HEAD_DIM), BF16)
                     for d in (1,) + tuple(dilations)]
    else:
        out_spec = pl.BlockSpec((tm, tn), ix(lambda i, j: (i, j)))
        out_shape = jax.ShapeDtypeStruct((T, N), F32)
    heads_per_dot = min(out_heads, MM_TN // HEAD_DIM)
    scratch = [pltpu.VMEM((heads_per_dot, tm, HEAD_DIM), F32)] if dilations else []
    out = pl.pallas_call(
        functools.partial(_mm_kernel, a_heads=a_heads, out_heads=out_heads,
                          n_rope_tiles=n_rope_tiles, dilations=tuple(dilations),
                          col_axis=0 if cols_outer else 1),
        grid=(N // tn, T // tm) if cols_outer else (T // tm, N // tn),
        in_specs=in_specs,
        out_specs=out_spec,
        out_shape=out_shape,
        scratch_shapes=scratch,
        compiler_params=_cparams("parallel", "parallel",
                                 vmem_bytes=VMEM_LIMIT_BIG_TILES if dilations else VMEM_LIMIT_BYTES),
        name="proj",
    )(*args)
    if head_major_out and not dilations:
        return out[0]
    return out


def _dil_attn_kernel(q1, q4, q16, k1c, k4c, k16c, k1p, k4p, k16p, v1c, v4c, v16c, v1p, v4p, v16p,
                     o_ref, *scratch, scale):
    acc_s, m_s, l_s = scratch[0:4], scratch[4:8], scratch[8:12]
    bias_s, s_s, p_s, mu_s, st_acc, st_m, st_l, out_s = scratch[12:20]

    qi = lax.broadcasted_iota(I32, (BLOCK, 2 * BLOCK), 0)
    kc = lax.broadcasted_iota(I32, (BLOCK, 2 * BLOCK), 1)
    band = jnp.where(kc >= qi, jnp.where(kc <= qi + BLOCK, 0.0, NEG_INF), NEG_INF).astype(F32)
    bias_s[0] = band
    no_prev = jnp.where(pl.program_id(2) == 0, NEG_INF, 0.0).astype(F32)
    bias_s[1] = jnp.where(kc < BLOCK, band + no_prev, band)
    ones = jnp.ones((2 * BLOCK, HEAD_DIM), BF16)

    def run(units):
        for g, (q, kp, kc_, _, _, first, _) in enumerate(units):
            keys = jnp.concatenate([kp(), kc_()], axis=0)
            s_s[g] = _dot_nt(q(), keys) * scale + bias_s[1 if first else 0]
        for g in range(len(units)):
            s = s_s[g]
            m = jnp.max(jnp.maximum(s[:, :BLOCK], s[:, BLOCK:]), axis=1, keepdims=True)
            p_s[g] = jnp.exp(s - m).astype(BF16)
            mu_s[g] = jnp.broadcast_to(m, (BLOCK, HEAD_DIM))
        for g, (_, _, _, vp, vc, _, sink) in enumerate(units):
            vals = jnp.concatenate([vp(), vc()], axis=0)
            pv = _dot(p_s[g], jnp.concatenate([vals, ones], axis=1))
            sink(g, pv[:, :HEAD_DIM], mu_s[g], pv[:, HEAD_DIM:])

    def fold(slab, rows, acc_u, m_u, l_u):
        m_o = m_s[slab][rows, :]
        m_n = jnp.maximum(m_o, m_u)
        e_o = jnp.exp(m_o - m_n)
        e_u = jnp.exp(m_u - m_n)
        acc_s[slab][rows, :] = acc_s[slab][rows, :] * e_o + acc_u * e_u
        l_s[slab][rows, :] = l_s[slab][rows, :] * e_o + l_u * e_u
        m_s[slab][rows, :] = m_n

    def tile(ref, rows, lanes):
        return lambda: ref[0, 0, 0, rows, lanes]

    def banded(q, kc_, kp, vc, vp, n, lanes, sink):
        cur = slice(n * BLOCK, (n + 1) * BLOCK)
        if n == 0:
            k_prev, v_prev = tile(kp, slice(None), lanes), tile(vp, slice(None), lanes)
        else:
            prv = slice((n - 1) * BLOCK, n * BLOCK)
            k_prev, v_prev = tile(kc_, prv, lanes), tile(vc, prv, lanes)
        return (tile(q, cur, lanes), k_prev, tile(kc_, cur, lanes), v_prev, tile(vc, cur, lanes),
                n == 0, sink)

    units = []
    for n in range(SUPER // 4 // BLOCK):
        for r in range(4):
            def init(g, acc, m, l, r=r, n=n):
                rows = slice(n * BLOCK, (n + 1) * BLOCK)
                acc_s[r][rows, :], m_s[r][rows, :], l_s[r][rows, :] = acc, m, l

            units.append(banded(q4, k4c, k4p, v4c, v4p, n,
                                slice(r * HEAD_DIM, (r + 1) * HEAD_DIM), init))

    for n in range(SUPER // BLOCK):
        def fold1(g, acc, m, l, n=n):
            st_acc[g], st_m[g], st_l[g] = acc, m, l
            rows = slice(n * (BLOCK // 4), (n + 1) * (BLOCK // 4))
            for r in range(4):
                pick = pl.ds(r, BLOCK // 4, stride=4)
                fold(r, rows, st_acc[g, pick, :], st_m[g, pick, :], st_l[g, pick, :])

        units.append(banded(q1, k1c, k1p, v1c, v1p, n, slice(None), fold1))

    for r in range(16):
        def fold16(g, acc, m, l, r=r):
            fold(r % 4, pl.ds(r // 4, BLOCK, stride=4), acc, m, l)

        units.append(banded(q16, k16c, k16p, v16c, v16p, 0,
                            slice(r * HEAD_DIM, (r + 1) * HEAD_DIM), fold16))

    for start in range(0, len(units), DIL_GROUP):
        run(units[start:start + DIL_GROUP])

    for r in range(4):
        out_s[pl.ds(r, SUPER // 4, stride=4), :] = acc_s[r][...] / l_s[r][...]
    o_ref[0, 0] = out_s[...].astype(BF16)


def _dilated_attention(qkv1, qkv4, qkv16):
    _, B, H, S, Dh = qkv1.shape
    assert Dh == HEAD_DIM and S % SUPER == 0
    assert tuple(d for _, d in DILATED_BRANCHES) == DILATIONS
    assert all(w // d == BLOCK for w, d in DILATED_BRANCHES)
    views = {1: qkv1, 4: qkv4, 16: qkv16}

    def cur(which, d):
        return pl.BlockSpec((1, 1, 1, SUPER // d, d * Dh), lambda b, h, m: (which, b, h, m, 0))

    def prev(which, d):
        per = SUPER // d // BLOCK
        return pl.BlockSpec((1, 1, 1, BLOCK, d * Dh),
                            lambda b, h, m: (which, b, h, jnp.maximum(m * per - 1, 0), 0))

    in_specs, args = [], []
    for make, which in ((cur, 0), (cur, 1), (prev, 1), (cur, 2), (prev, 2)):
        for d in (1, 4, 16):
            in_specs.append(make(which, d))
            args.append(views[d])
    state = [pltpu.VMEM((SUPER // 4, Dh), F32)] * 12
    stage = [pltpu.VMEM((2, BLOCK, 2 * BLOCK), F32),
             pltpu.VMEM((DIL_GROUP, BLOCK, 2 * BLOCK), F32),
             pltpu.VMEM((DIL_GROUP, BLOCK, 2 * BLOCK), BF16),
             pltpu.VMEM((DIL_GROUP, BLOCK, Dh), F32)] + [
             pltpu.VMEM((DIL_GROUP, BLOCK, Dh), F32)] * 3 + [
             pltpu.VMEM((SUPER, Dh), F32)]
    return pl.pallas_call(
        functools.partial(_dil_attn_kernel, scale=Dh ** -0.5),
        grid=(B, H, S // SUPER),
        in_specs=in_specs,
        out_specs=pl.BlockSpec((1, 1, SUPER, Dh), lambda b, h, m: (b, h, m, 0)),
        out_shape=jax.ShapeDtypeStruct((B, H, S, Dh), BF16),
        scratch_shapes=state + stage,
        compiler_params=_cparams("parallel", "parallel", "parallel"),
        name="dilated_attention",
    )(*args)


def _sb_kernel(q_ref, k_ref, v_ref, o_ref, tri_ref, acc_ref, carry_ref, logb_s, rsum_s, tail_s,
               hi_s, lo_s, a_s, *, scale, tq, nq):
    row = lax.broadcasted_iota(I32, (tq, tq), 0)
    col = lax.broadcasted_iota(I32, (tq, tq), 1)
    tri_ref[...] = jnp.where(row > col, 1.0, 0.0).astype(BF16)

    def block_rows(n):
        return pl.ds(pl.multiple_of(n * tq, tq), tq)

    def sweep(n0, i):
        diagonal = isinstance(i, int) and i == 0
        causal = col < row
        keys = [block_rows(jnp.maximum(n0 + g - i, 0)) for g in range(SB_GROUP)]
        for g in range(SB_GROUP):
            z2 = _dot_nt(q_ref[0, 0, block_rows(n0 + g), :], k_ref[0, 0, 0, keys[g], :])
            z2 = z2 * (scale * LOG2_E)
            t = jnp.log2(1.0 + jnp.exp2(-jnp.abs(z2)))
            log_1mb = -(jnp.maximum(z2, 0.0) + t)
            if diagonal:
                log_1mb = jnp.where(causal, log_1mb, 0.0)
            logb_s[g] = jnp.minimum(z2, 0.0) - t
            hi = log_1mb.astype(BF16)
            hi_s[g] = hi
            lo_s[g] = (log_1mb - hi.astype(F32)).astype(BF16)
            rsum_s[g] = jnp.broadcast_to(jnp.sum(log_1mb, axis=1, keepdims=True), (tq, tq))
        tail_s[...] = (_dot(hi_s[...].reshape(SB_GROUP * tq, tq), tri_ref[...])
                       + _dot(lo_s[...].reshape(SB_GROUP * tq, tq), tri_ref[...])
                       ).reshape(SB_GROUP, tq, tq)
        tops = []
        for g in range(SB_GROUP):
            tail = tail_s[g]
            if diagonal:
                a = jnp.where(causal, jnp.exp2(logb_s[g] + tail), 0.0)
                c = rsum_s[g]
            else:
                c = carry_ref[g]
                a = jnp.where(n0 + g - i >= 0, jnp.exp2(logb_s[g] + tail + c), 0.0)
                c = c + rsum_s[g]
            a_s[g] = a.astype(BF16)
            carry_ref[g] = c
            tops.append(jnp.where(n0 + g - i >= 1, jnp.max(c), NEG_INF))
        for g in range(SB_GROUP):
            av = _dot(a_s[g], v_ref[0, 0, 0, keys[g], :])
            acc_ref[g] = av if diagonal else acc_ref[g] + av
        return functools.reduce(jnp.maximum, tops)

    def qgroup(it, carry):
        n0 = it * SB_GROUP

        def cond(st):
            _, top = st
            return top > -SB_SKIP

        def body(st):
            i, _ = st
            return i + 1, sweep(n0, i)

        lax.while_loop(cond, body, (1, sweep(n0, 0)))
        for g in range(SB_GROUP):
            o_ref[0, 0, block_rows(n0 + g), :] = acc_ref[g].astype(BF16)
        return carry

    lax.fori_loop(0, nq // SB_GROUP, qgroup, 0)


def _stick_breaking_attention(q, kv):
    _, B, H, S, Dh = q.shape
    tq = _tile(SB_TQ, S)
    seq = lambda which: pl.BlockSpec((1, 1, 1, S, Dh), lambda b, h: (which, b, h, 0, 0))
    return pl.pallas_call(
        functools.partial(_sb_kernel, scale=Dh ** -0.5, tq=tq, nq=S // tq),
        grid=(B, H),
        in_specs=[pl.BlockSpec((1, 1, S, Dh), lambda b, h: (b, h, 0, 0)), seq(0), seq(1)],
        out_specs=pl.BlockSpec((1, 1, S, Dh), lambda b, h: (b, h, 0, 0)),
        out_shape=jax.ShapeDtypeStruct((B, H, S, Dh), BF16),
        scratch_shapes=[pltpu.VMEM((tq, tq), BF16), pltpu.VMEM((SB_GROUP, tq, Dh), F32)]
        + [pltpu.VMEM((SB_GROUP, tq, tq), F32)] * 4 + [pltpu.VMEM((SB_GROUP, tq, tq), BF16)] * 3,
        compiler_params=_cparams("parallel", "parallel"),
        name="stick_breaking_attention",
    )(q.reshape(B, H, S, Dh), kv, kv)


def _router_kernel(h_ref, rw_ref, rb_ref, e_ref, r_ref, w_ref, cnt_ref, tri_ref, carry_ref,
                   *, n_exp, tm, nsteps):
    i = pl.program_id(0)
    gsz = n_exp // N_GROUPS

    @pl.when(i == 0)
    def _():
        row = lax.broadcasted_iota(I32, (tm, tm), 0)
        col = lax.broadcasted_iota(I32, (tm, tm), 1)
        tri_ref[...] = jnp.where(row < col, 1.0, 0.0).astype(BF16)
        carry_ref[...] = jnp.zeros_like(carry_ref)

    lo, hi = _unpack_pair(_load_slab_words(h_ref, tm))
    half = lo.shape[1]
    logits = _dot_nt(rw_ref[:, :half], lo) + _dot_nt(rw_ref[:, half:], hi)
    scores = jax.nn.sigmoid(logits)
    sel = scores + rb_ref[...]

    sub = lax.broadcasted_iota(I32, (gsz, tm), 0)
    groups, gscore = [], []
    for g in range(N_GROUPS):
        sg = sel[g * gsz:(g + 1) * gsz, :]
        m1 = jnp.max(sg, axis=0, keepdims=True)
        i1 = jnp.min(jnp.where(sg == m1, sub, gsz), axis=0, keepdims=True)
        m2 = jnp.max(jnp.where(sub == i1, NEG_INF, sg), axis=0, keepdims=True)
        groups.append(sg)
        gscore.append(m1 + m2)
    masked = []
    for g in range(N_GROUPS):
        beat = jnp.zeros((1, tm), F32)
        for o in range(N_GROUPS):
            if o == g:
                continue
            beat += jnp.where(gscore[o] > gscore[g], 1.0, 0.0)
            if o < g:
                beat += jnp.where(gscore[o] == gscore[g], 1.0, 0.0)
        masked.append(jnp.where(beat < TOPK_GROUPS, groups[g], NEG_INF))
    x = jnp.concatenate(masked, axis=0)

    eid = lax.broadcasted_iota(I32, (n_exp, tm), 0)
    chosen = jnp.zeros((n_exp, tm), F32)
    picks, vals = [], []
    for _ in range(TOP_K):
        mx = jnp.max(x, axis=0, keepdims=True)
        idx = jnp.min(jnp.where(x == mx, eid, n_exp), axis=0, keepdims=True)
        hit = eid == idx
        picks.append(idx)
        vals.append(jnp.sum(jnp.where(hit, scores, 0.0), axis=0, keepdims=True))
        chosen += jnp.where(hit, 1.0, 0.0)
        x = jnp.where(hit, NEG_INF, x)
    total = vals[0]
    for v in vals[1:]:
        total = total + v

    rank_all = _dot(chosen.astype(BF16), tri_ref[...]) + carry_ref[...]
    for k in range(TOP_K):
        e_ref[k:k + 1, :] = picks[k]
        w_ref[k:k + 1, :] = vals[k] / total * ROUTED_SCALE
        rk = jnp.sum(jnp.where(eid == picks[k], rank_all, 0.0), axis=0, keepdims=True)
        r_ref[k:k + 1, :] = rk.astype(I32)
    carry_ref[...] += jnp.broadcast_to(jnp.sum(chosen, axis=1, keepdims=True), (n_exp, tm))

    @pl.when(i == nsteps - 1)
    def _():
        cnt_ref[...] = carry_ref[:, 0:LANES].astype(I32)


def _router(h32, rw_t, rb):
    E, D = rw_t.shape
    ch = D // 2 // LANES
    T = h32.shape[0] // ch
    tm = _tile(ROUTER_TM, T)
    nsteps = T // tm
    tok = pl.BlockSpec((TOP_K, tm), lambda i: (0, i))
    return pl.pallas_call(
        functools.partial(_router_kernel, n_exp=E, tm=tm, nsteps=nsteps),
        grid=(nsteps,),
        in_specs=[pl.BlockSpec((tm * ch, LANES), lambda i: (i, 0)),
                  pl.BlockSpec((E, D), lambda i: (0, 0)),
                  pl.BlockSpec((E, 1), lambda i: (0, 0))],
        out_specs=[tok, tok, tok, pl.BlockSpec((E, LANES), lambda i: (0, 0))],
        out_shape=[jax.ShapeDtypeStruct((TOP_K, T), I32), jax.ShapeDtypeStruct((TOP_K, T), I32),
                   jax.ShapeDtypeStruct((TOP_K, T), F32), jax.ShapeDtypeStruct((E, LANES), I32)],
        scratch_shapes=[pltpu.VMEM((tm, tm), BF16), pltpu.VMEM((E, tm), F32)],
        compiler_params=_cparams("arbitrary"),
        name="moe_router",
    )(h32, rw_t, rb)


def _dispatch_kernel(last_ref, slots_ref, h_ref, xs_hbm, zbuf, zsem, sem, *, n_exp, td, ch):
    i = pl.program_id(0)
    tile_rows = zbuf.shape[0]

    def zero_copy(e):
        dst = pl.ds(pl.multiple_of(last_ref[e], tile_rows), tile_rows)
        return pltpu.make_async_copy(zbuf, xs_hbm.at[dst], zsem)

    @pl.when(i == 0)
    def _():
        zbuf[...] = jnp.zeros_like(zbuf)

        def start(e, c):
            @pl.when(last_ref[e] >= 0)
            def _():
                zero_copy(e).start()
            return c

        def wait(e, c):
            @pl.when(last_ref[e] >= 0)
            def _():
                zero_copy(e).wait()
            return c

        lax.fori_loop(0, n_exp, start, 0)
        lax.fori_loop(0, n_exp, wait, 0)

    def row_copy(t, k):
        src = pl.ds(pl.multiple_of(t * ch, ch), ch)
        dst = pl.ds(pl.multiple_of(slots_ref[k, t], ch), ch)
        return pltpu.make_async_copy(h_ref.at[src], xs_hbm.at[dst], sem)

    def start_rows(t, c):
        for k in range(TOP_K):
            row_copy(t, k).start(priority=k % 2)
        return c

    def wait_rows(t, c):
        for k in range(TOP_K):
            row_copy(t, k).wait()
        return c

    lax.fori_loop(0, td, start_rows, 0)
    lax.fori_loop(0, td, wait_rows, 0)


def _dispatch(h32, slot_rows, last_rows, n_slots, tm_e, ch):
    T = slot_rows.shape[1]
    td = _tile(DISPATCH_TD, T)
    E = last_rows.shape[0]
    return pl.pallas_call(
        functools.partial(_dispatch_kernel, n_exp=E, td=td, ch=ch),
        grid_spec=pltpu.PrefetchScalarGridSpec(
            num_scalar_prefetch=1,
            grid=(T // td,),
            in_specs=[pl.BlockSpec((TOP_K, td), lambda i, last: (0, i), memory_space=pltpu.SMEM),
                      pl.BlockSpec((td * ch, LANES), lambda i, last: (i, 0))],
            out_specs=pl.BlockSpec(memory_space=pl.ANY),
            scratch_shapes=[pltpu.VMEM((tm_e * ch, LANES), U32), pltpu.SemaphoreType.DMA,
                            pltpu.SemaphoreType.DMA]),
        out_shape=jax.ShapeDtypeStruct((n_slots * ch, LANES), U32),
        compiler_params=_cparams("arbitrary"),
        name="moe_dispatch",
    )(last_rows, slot_rows, h32)


def _mlp_hidden(x_ref, wgu_ref, n_tok):
    f = wgu_ref.shape[1] // 2
    lo, hi = _unpack_pair(_load_slab_words(x_ref, n_tok))
    half = lo.shape[1]
    gu = _dot(lo, wgu_ref[:half, :]) + _dot(hi, wgu_ref[half:, :])
    return (jax.nn.silu(gu[:, :f]) * gu[:, f:]).astype(BF16)


def _expert_kernel(te_ref, nx_ref, nt_ref, x_ref, wg_hbm, wu_hbm, wd_hbm, y_ref,
                   wg_f, wu_f, wd_f, wgu_s, wd_s, sems, *, tm_e, layer):
    i = pl.program_id(0)
    f, d_model = wd_s.shape
    half = d_model // 2
    ch = half // LANES

    def weight_copies(e):
        return (pltpu.make_async_copy(wg_hbm.at[layer, e], wg_f, sems.at[0]),
                pltpu.make_async_copy(wu_hbm.at[layer, e], wu_f, sems.at[1]),
                pltpu.make_async_copy(wd_hbm.at[layer, e], wd_f, sems.at[2]))

    @pl.when(i < nt_ref[0])
    def _():
        expert = te_ref[i]

        @pl.when(i == 0)
        def _():
            for cp in weight_copies(expert):
                cp.start()

        @pl.when(jnp.logical_or(i == 0, expert != te_ref[jnp.maximum(i - 1, 0)]))
        def _():
            for cp in weight_copies(expert):
                cp.wait()
            wgu_s[:, :f] = wg_f[...].astype(BF16)
            wgu_s[:, f:] = wu_f[...].astype(BF16)
            for c in range(ch):
                lo_cols = slice(c * LANES, (c + 1) * LANES)
                hi_cols = slice(half + c * LANES, half + (c + 1) * LANES)
                wd_s[:, 2 * c * LANES:(2 * c + 1) * LANES] = wd_f[:, lo_cols].astype(BF16)
                wd_s[:, (2 * c + 1) * LANES:(2 * c + 2) * LANES] = wd_f[:, hi_cols].astype(BF16)

            @pl.when(nx_ref[i] >= 0)
            def _():
                for cp in weight_copies(nx_ref[i]):
                    cp.start()

        a = _mlp_hidden(x_ref, wgu_s, tm_e)
        for c in range(ch):
            yc = _dot(a, wd_s[:, 2 * c * LANES:(2 * c + 2) * LANES])
            y_ref[pl.ds(c, tm_e, stride=ch), :] = _pack_pair(yc[:, :LANES], yc[:, LANES:])


def _shared_kernel(x_ref, wgu_ref, wd_ref, y_ref, *, tm):
    y_ref[...] = _dot(_mlp_hidden(x_ref, wgu_ref, tm), wd_ref[...])


def _experts(xs, layer, w_gate, w_up, w_down, tile_expert, next_expert, n_tiles, tm_e, ch):
    _, E, D, F = w_gate.shape
    n_slots = xs.shape[0] // ch
    rows = pl.BlockSpec((tm_e * ch, LANES),
                        lambda i, te, nx, nt: (jnp.minimum(i, nt[0] - 1), 0))
    hbm = pl.BlockSpec(memory_space=pl.ANY)
    return pl.pallas_call(
        functools.partial(_expert_kernel, tm_e=tm_e, layer=layer),
        grid_spec=pltpu.PrefetchScalarGridSpec(
            num_scalar_prefetch=3,
            grid=(n_slots // tm_e,),
            in_specs=[rows, hbm, hbm, hbm],
            out_specs=rows,
            scratch_shapes=[pltpu.VMEM((D, F), F32), pltpu.VMEM((D, F), F32),
                            pltpu.VMEM((F, D), F32), pltpu.VMEM((D, 2 * F), BF16),
                            pltpu.VMEM((F, D), BF16), pltpu.SemaphoreType.DMA((3,))]),
        out_shape=jax.ShapeDtypeStruct(xs.shape, U32),
        compiler_params=_cparams("arbitrary"),
        name="moe_experts",
    )(tile_expert, next_expert, n_tiles, xs, w_gate, w_up, w_down)


def _shared_expert(h32, wgu, wd, ch):
    D, F2 = wgu.shape
    T = h32.shape[0] // ch
    tm = _tile(512, T)
    return pl.pallas_call(
        functools.partial(_shared_kernel, tm=tm),
        grid=(T // tm,),
        in_specs=[pl.BlockSpec((tm * ch, LANES), lambda i: (i, 0)),
                  pl.BlockSpec((D, F2), lambda i: (0, 0)),
                  pl.BlockSpec((F2 // 2, D), lambda i: (0, 0))],
        out_specs=pl.BlockSpec((tm, D), lambda i: (i, 0)),
        out_shape=jax.ShapeDtypeStruct((T, D), F32),
        compiler_params=_cparams("parallel"),
        name="moe_shared_expert",
    )(h32, wgu, wd)


def _combine_kernel(slots_ref, next_slots_ref, w_ref, ysh_ref, y_hbm, o_ref, bufs, gate_s, sems,
                    *, tc, ch, nsteps):
    i = pl.program_id(0)
    slot = i % 2

    def row_copy(table, which, t, k):
        src = pl.ds(pl.multiple_of(table[k, t], ch), ch)
        dst = pl.ds(pl.multiple_of(t * ch, ch), ch)
        return pltpu.make_async_copy(y_hbm.at[src], bufs.at[which, k, dst], sems.at[which])

    def gather_token(table, which, t):
        for k in range(TOP_K):
            row_copy(table, which, t, k).start(priority=k % 2)

    def wait_all(table, which):
        def wait_rows(t, c):
            for k in range(TOP_K):
                row_copy(table, which, t, k).wait()
            return c

        lax.fori_loop(0, tc, wait_rows, 0)

    @pl.when(i == 0)
    def _():
        def start_rows(t, c):
            gather_token(slots_ref, 0, t)
            return c

        lax.fori_loop(0, tc, start_rows, 0)

    wait_all(slots_ref, slot)

    half = ch * LANES
    w_cols = w_ref[...].T
    for k in range(TOP_K):
        gate_s[k] = jnp.broadcast_to(w_cols[:, k:k + 1], (tc, LANES))
    per_chunk = tc // ch
    for c in range(ch):
        lanes_lo = slice(c * LANES, (c + 1) * LANES)
        lanes_hi = slice(half + c * LANES, half + (c + 1) * LANES)
        acc_lo = ysh_ref[:, lanes_lo]
        acc_hi = ysh_ref[:, lanes_hi]
        for k in range(TOP_K):
            lo, hi = _unpack_pair_f32(bufs.at[slot, k][pl.ds(c, tc, stride=ch), :])
            acc_lo = acc_lo + gate_s[k] * lo
            acc_hi = acc_hi + gate_s[k] * hi
        o_ref[:, lanes_lo] = acc_lo
        o_ref[:, lanes_hi] = acc_hi
        for t in range(c * per_chunk, (c + 1) * per_chunk):
            gather_token(next_slots_ref, 1 - slot, t)

    @pl.when(i == nsteps - 1)
    def _():
        wait_all(next_slots_ref, 1 - slot)


def _combine(slot_rows, wts, ysh, ys, ch):
    T, D = ysh.shape
    tc = _tile(COMBINE_TC, T)
    nsteps = T // tc
    return pl.pallas_call(
        functools.partial(_combine_kernel, tc=tc, ch=ch, nsteps=nsteps),
        grid=(nsteps,),
        in_specs=[pl.BlockSpec((TOP_K, tc), lambda i: (0, i), memory_space=pltpu.SMEM),
                  pl.BlockSpec((TOP_K, tc), lambda i: (0, jnp.minimum(i + 1, nsteps - 1)),
                               memory_space=pltpu.SMEM),
                  pl.BlockSpec((TOP_K, tc), lambda i: (0, i)),
                  pl.BlockSpec((tc, D), lambda i: (i, 0)),
                  pl.BlockSpec(memory_space=pl.ANY)],
        out_specs=pl.BlockSpec((tc, D), lambda i: (i, 0)),
        out_shape=jax.ShapeDtypeStruct((T, D), F32),
        scratch_shapes=[pltpu.VMEM((2, TOP_K, tc * ch, LANES), U32),
                        pltpu.VMEM((TOP_K, tc, LANES), F32),
                        pltpu.SemaphoreType.DMA((2,))],
        compiler_params=_cparams("arbitrary"),
        name="moe_combine",
    )(slot_rows, slot_rows, wts, ysh, ys)


def _moe(h32, layer, rw_t, rb, w_gate, w_up, w_down, sh_gu, sh_d):
    E, D = rw_t.shape
    ch = D // 2 // LANES
    assert ch % 8 == 0, "a token slab must be whole (8,128) tiles"
    T = h32.shape[0] // ch
    tm_e = _tile(EXPERT_TM, T)
    eidx, rank, wts, counts = _router(h32, rw_t, rb)

    counts = counts[:, 0]
    tiles = (counts + tm_e - 1) // tm_e
    tile_end = jnp.cumsum(tiles)
    tile_start = tile_end - tiles
    n_tiles = tile_end[-1]
    experts = jnp.arange(E, dtype=I32)
    first_row = jnp.sum(jnp.where(eidx[:, :, None] == experts, tile_start * tm_e, 0), axis=-1)
    slot_rows = (first_row + rank) * ch
    max_tiles = T * TOP_K // tm_e + E
    tile_ids = jnp.minimum(jnp.arange(max_tiles, dtype=I32), n_tiles - 1)
    tile_expert = jnp.sum((tile_ids[:, None] >= tile_end[None, :]).astype(I32), axis=-1)
    group_end = jnp.sum(jnp.where(tile_expert[:, None] == experts, tile_end, 0), axis=-1)
    after = jnp.sum((group_end[:, None] >= tile_end[None, :]).astype(I32), axis=-1)
    next_expert = jnp.where(group_end < n_tiles, after, -1).astype(I32)
    last_rows = jnp.where(tiles > 0, (tile_end - 1) * (tm_e * ch), -1).astype(I32)

    xs = _dispatch(h32, slot_rows, last_rows, max_tiles * tm_e, tm_e, ch)
    ys = _experts(xs, layer, w_gate, w_up, w_down, tile_expert, next_expert,
                  n_tiles.reshape(1).astype(I32), tm_e, ch)
    ysh = _shared_expert(h32, sh_gu, sh_d, ch)
    return _combine(slot_rows, wts, ysh, ys, ch)


def kernel(x, c, positions, ada_w, ada_b, ln_g, ln_b, a_w_qkv, a_w_o, kv_ada_w, kv_ada_b, b_w_kv, b_w_q, b_w_o, router_w, router_b, w_gate, w_up, w_down, sh_gate, sh_up, sh_down):
    B, S, D = x.shape
    T = B * S
    assert D % HEAD_DIM == 0 and DEPTH == ada_w.shape[0] == 2
    xf = x.reshape(T, D)

    c8 = jnp.zeros((8, D), F32).at[:B].set(c)
    mods = _adaln(c8, ada_w, ada_b)[:, :B].reshape(DEPTH, B, 6, D)
    kv_mods = _adaln(c8, kv_ada_w[None], kv_ada_b[None])[0, :B].reshape(B, 2, D)
    sh_a, sc_a, g_a, sh_m, sc_m, g_m = (mods[:, :, r] for r in range(6))
    rope = _rope_tables(positions)

    def moe_weights(l):
        sh_gu = jnp.concatenate([sh_gate[l], sh_up[l]], axis=-1).astype(BF16)
        return (l, router_w[l].T.astype(BF16), router_b[l].reshape(-1, 1), w_gate, w_up, w_down,
                sh_gu, sh_down[l].astype(BF16))

    h = _modulate(xf, _mod_rows(None, [(sh_a[0], sc_a[0])]), S)
    qkv_views = _mm(h, a_w_qkv[0].astype(BF16), B=B, S=S, head_major_out=True, rope=rope,
                    n_rope_cols=2 * D, dilations=DILATIONS[1:])
    o = _dilated_attention(*qkv_views)
    y = _mm(o, a_w_o[0].astype(BF16), B=B, S=S)
    xf, h32 = _ln(xf, y, _mod_rows(g_a[0], [(sh_m[0], sc_m[0])]), ln_g[0, 0], ln_b[0, 0],
                  ["u32"], S)
    y = _moe(h32, *moe_weights(0))
    xf, h_kv, h_q = _ln(xf, y, _mod_rows(g_m[0], [(kv_mods[:, 0], kv_mods[:, 1]),
                                                  (sh_a[1], sc_a[1])]),
                        ln_g[0, 1], ln_b[0, 1], ["bf16", "bf16"], S)

    kv = _mm(h_kv, b_w_kv.astype(BF16), B=B, S=S, head_major_out=True)
    q = _mm(h_q, b_w_q[0].astype(BF16), B=B, S=S, head_major_out=True)
    o = _stick_breaking_attention(q, kv)
    y = _mm(o, b_w_o[0].astype(BF16), B=B, S=S)
    xf, h32 = _ln(xf, y, _mod_rows(g_a[1], [(sh_m[1], sc_m[1])]), ln_g[1, 0], ln_b[1, 0],
                  ["u32"], S)
    y = _moe(h32, *moe_weights(1))
    (xf,) = _ln(xf, y, _mod_rows(g_m[1], []), ln_g[1, 1], ln_b[1, 1], [], S)
    return xf.reshape(B, S, D)
```

```python
import functools
import math

import jax
import jax.numpy as jnp
from jax import lax
from jax.experimental import pallas as pl
from jax.experimental.pallas import tpu as pltpu

F32 = jnp.float32
BF16 = jnp.bfloat16
U32 = jnp.uint32
I32 = jnp.int32

HEAD_DIM = 128
ROT_DIM = HEAD_DIM // 4
ROPE_THETA = 500000.0
DILATED_BRANCHES = ((128, 1), (512, 4), (2048, 16))
DILATIONS = tuple(d for _, d in DILATED_BRANCHES)
BLOCK = 128
TOP_K = 8
N_GROUPS = 8
TOPK_GROUPS = 4
ROUTED_SCALE = 2.5
DEPTH = 2
DEEPNORM_ALPHA = (2 * DEPTH) ** 0.25
LN_EPS = 1e-5

LANES = 128
VMEM_LIMIT_BYTES = 56 * 1024 * 1024
VMEM_LIMIT_BIG_TILES = 58 * 1024 * 1024

MM_TM = 1024
MM_TN = 512
LN_TM = 256
ROUTER_TM = 512
EXPERT_TM = 512
DISPATCH_TD = 256
COMBINE_TC = 128
SUPER = 2048
DIL_GROUP = 8
SB_TQ = 128
SB_GROUP = 16
SB_SKIP = 127.0
LOG2_E = 1.4426950408889634

NEG_INF = float("-inf")
HI_MASK = 0xFFFF0000


def _tile(pref, dim):
    t = min(pref, dim)
    assert dim % t == 0, (pref, dim)
    return t


def _cparams(*sem, vmem_bytes=VMEM_LIMIT_BYTES):
    return pltpu.CompilerParams(dimension_semantics=sem, vmem_limit_bytes=vmem_bytes)


def _dot(a, b):
    return jnp.dot(a, b, preferred_element_type=F32)


def _dot_nt(a, b):
    return lax.dot_general(a, b, (((1,), (1,)), ((), ())), preferred_element_type=F32)


def _unpack_pair(w):
    lo = lax.bitcast_convert_type(w << 16, F32).astype(BF16)
    hi = lax.bitcast_convert_type(w & jnp.uint32(HI_MASK), F32).astype(BF16)
    return lo, hi


def _unpack_pair_f32(w):
    lo = lax.bitcast_convert_type(w << 16, F32)
    hi = lax.bitcast_convert_type(w & jnp.uint32(HI_MASK), F32)
    return lo, hi


def _pack_pair(lo, hi):
    lo_b = lax.bitcast_convert_type(lo.astype(BF16).astype(F32), U32)
    hi_b = lax.bitcast_convert_type(hi.astype(BF16).astype(F32), U32)
    return (hi_b & jnp.uint32(HI_MASK)) | (lo_b >> 16)


def _load_slab_words(ref, n_tok):
    ch = ref.shape[0] // n_tok
    return jnp.concatenate([ref[pl.ds(c, n_tok, stride=ch), :] for c in range(ch)], axis=1)


def _store_slab_words(ref, words):
    n_tok = words.shape[0]
    ch = ref.shape[0] // n_tok
    for c in range(ch):
        ref[pl.ds(c, n_tok, stride=ch), :] = words[:, c * LANES:(c + 1) * LANES]


def _adaln_kernel(c_ref, w_ref, b_ref, o_ref, *, nk):
    k = pl.program_id(2)

    @pl.when(k == 0)
    def _():
        o_ref[...] = jnp.zeros_like(o_ref)

    cond = jax.nn.silu(c_ref[...])
    o_ref[0] += jnp.dot(cond, w_ref[0], preferred_element_type=F32,
                        precision=lax.Precision.HIGHEST)

    @pl.when(k == nk - 1)
    def _():
        o_ref[0] += b_ref[0]


def _adaln(c8, w, b):
    L, D, N = w.shape
    tk = _tile(1024, D)
    tn = _tile(2048, N)
    nk = D // tk
    return pl.pallas_call(
        functools.partial(_adaln_kernel, nk=nk),
        grid=(L, N // tn, nk),
        in_specs=[pl.BlockSpec((8, tk), lambda l, j, k: (0, k)),
                  pl.BlockSpec((1, tk, tn), lambda l, j, k: (l, k, j)),
                  pl.BlockSpec((1, 1, tn), lambda l, j, k: (l, 0, j))],
        out_specs=pl.BlockSpec((1, 8, tn), lambda l, j, k: (l, 0, j)),
        out_shape=jax.ShapeDtypeStruct((L, 8, N), F32),
        compiler_params=_cparams("parallel", "parallel", "arbitrary"),
        name="adaln",
    )(c8, w, b.reshape(L, 1, N))


def _rope_kernel(pos_ref, freq_ref, sign_ref, cos_ref, sin_ref):
    ang = pos_ref[...] * freq_ref[...]
    cos_ref[...] = jnp.cos(ang)
    sin_ref[...] = jnp.sin(ang) * sign_ref[...]


def _rope_tables(positions):
    T = positions.size
    half = ROT_DIM // 2
    inv_freq = ROPE_THETA ** (-jnp.arange(0, ROT_DIM, 2, dtype=F32) / ROT_DIM)
    freq = jnp.concatenate([inv_freq, inv_freq, jnp.zeros((HEAD_DIM - ROT_DIM,), F32)])[None, :]
    sign = jnp.concatenate([-jnp.ones((half,), F32), jnp.ones((HEAD_DIM - half,), F32)])[None, :]
    pos = jnp.broadcast_to(positions.astype(F32).reshape(T, 1), (T, HEAD_DIM))
    tm = _tile(2048, T)
    row = pl.BlockSpec((tm, HEAD_DIM), lambda i: (i, 0))
    vec = pl.BlockSpec((1, HEAD_DIM), lambda i: (0, 0))
    return pl.pallas_call(
        _rope_kernel,
        grid=(T // tm,),
        in_specs=[row, vec, vec],
        out_specs=[row, row],
        out_shape=[jax.ShapeDtypeStruct((T, HEAD_DIM), F32)] * 2,
        compiler_params=_cparams("parallel"),
        name="rope_tables",
    )(pos, freq, sign)


def _emit_mod(xn, rows_ref, idx, fmt, out_ref):
    shift = rows_ref[0, 1 + 2 * idx:2 + 2 * idx, :]
    scale = rows_ref[0, 2 + 2 * idx:3 + 2 * idx, :]
    h = xn * (1.0 + scale) + shift
    if fmt == "bf16":
        out_ref[...] = h.astype(BF16)
    else:
        half = h.shape[1] // 2
        _store_slab_words(out_ref, _pack_pair(h[:, :half], h[:, half:]))


def _modulate_kernel(x_ref, rows_ref, o_ref):
    _emit_mod(x_ref[...], rows_ref, 0, "bf16", o_ref)


def _ln_kernel(x_ref, y_ref, rows_ref, g_ref, b_ref, xo_ref, *h_refs, fmts):
    gate = rows_ref[0, 0:1, :]
    z = DEEPNORM_ALPHA * x_ref[...] + (1.0 + gate) * y_ref[...]
    mu = jnp.mean(z, axis=-1, keepdims=True)
    zc = z - mu
    var = jnp.mean(zc * zc, axis=-1, keepdims=True)
    xn = zc * lax.rsqrt(var + LN_EPS) * g_ref[...] + b_ref[...]
    xo_ref[...] = xn
    for idx, (fmt, h_ref) in enumerate(zip(fmts, h_refs)):
        _emit_mod(xn, rows_ref, idx, fmt, h_ref)


def _mod_rows(gate, pairs):
    B, D = pairs[0][0].shape if pairs else gate.shape
    rows = [gate if gate is not None else jnp.zeros((B, D), F32)]
    for shift, scale in pairs:
        rows += [shift, scale]
    rows += [jnp.zeros((B, D), F32)] * (8 - len(rows))
    return jnp.stack(rows, axis=1)


def _out_struct(T, D, tm, fmt):
    if fmt == "bf16":
        return jax.ShapeDtypeStruct((T, D), BF16), (tm, D)
    ch = D // 2 // LANES
    return jax.ShapeDtypeStruct((T * ch, LANES), U32), (tm * ch, LANES)


def _modulate(x, rows, S):
    T, D = x.shape
    tm = _tile(LN_TM, S)
    nb = S // tm
    return pl.pallas_call(
        _modulate_kernel,
        grid=(T // tm,),
        in_specs=[pl.BlockSpec((tm, D), lambda i: (i, 0)),
                  pl.BlockSpec((1, 8, D), lambda i: (i // nb, 0, 0))],
        out_specs=pl.BlockSpec((tm, D), lambda i: (i, 0)),
        out_shape=jax.ShapeDtypeStruct((T, D), BF16),
        compiler_params=_cparams("parallel"),
        name="modulate",
    )(x, rows)


def _ln(x, y, rows, g, b, fmts, S):
    T, D = x.shape
    tm = _tile(LN_TM, S)
    nb = S // tm
    row = pl.BlockSpec((tm, D), lambda i: (i, 0))
    vec = pl.BlockSpec((1, D), lambda i: (0, 0))
    out_shape = [jax.ShapeDtypeStruct((T, D), F32)]
    out_specs = [row]
    for fmt in fmts:
        st, blk = _out_struct(T, D, tm, fmt)
        out_shape.append(st)
        out_specs.append(pl.BlockSpec(blk, lambda i: (i, 0)))
    return pl.pallas_call(
        functools.partial(_ln_kernel, fmts=tuple(fmts)),
        grid=(T // tm,),
        in_specs=[row, row, pl.BlockSpec((1, 8, D), lambda i: (i // nb, 0, 0)), vec, vec],
        out_specs=out_specs,
        out_shape=out_shape,
        compiler_params=_cparams("parallel"),
        name="deepnorm_ln",
    )(x, y, rows, g.reshape(1, D), b.reshape(1, D))


def _swap_halves(x):
    half = ROT_DIM // 2
    lane = lax.broadcasted_iota(I32, x.shape, 1)
    return jnp.where(lane < half, pltpu.roll(x, HEAD_DIM - half, 1), pltpu.roll(x, half, 1))


def _mm_kernel(*refs, a_heads, out_heads, n_rope_tiles, dilations, col_axis):
    a_ref, w_ref = refs[:2]
    refs = refs[2:]
    if n_rope_tiles:
        cos_ref, sin_ref = refs[:2]
        refs = refs[2:]
    o_ref = refs[0]
    view_refs = refs[1:1 + len(dilations)]
    stage = refs[1 + len(dilations)] if dilations else None
    if a_heads:
        a = jnp.concatenate([a_ref[0, h] for h in range(a_heads)], axis=-1)
    else:
        a = a_ref[...]
    if not out_heads:
        o_ref[...] = _dot(a, w_ref[...])
        return
    tm = a.shape[0]
    heads_per_dot = min(out_heads, MM_TN // HEAD_DIM)

    def write(rot):
        for h0 in range(0, out_heads, heads_per_dot):
            acc = _dot(a, w_ref[:, h0 * HEAD_DIM:(h0 + heads_per_dot) * HEAD_DIM])
            for hh in range(h0, h0 + heads_per_dot):
                xh = acc[:, (hh - h0) * HEAD_DIM:(hh - h0 + 1) * HEAD_DIM]
                if rot:
                    xh = xh * cos_ref[...] + _swap_halves(xh) * sin_ref[...]
                o_ref[0, 0, hh] = xh.astype(BF16)
                if dilations:
                    stage[hh - h0] = xh
                    for d, v_ref in zip(dilations, view_refs):
                        for r in range(d):
                            v_ref[0, 0, hh, :, r * HEAD_DIM:(r + 1) * HEAD_DIM] = (
                                stage[hh - h0, pl.ds(r, tm // d, stride=d), :].astype(BF16))

    if not n_rope_tiles:
        write(False)
    else:
        j = pl.program_id(col_axis)

        @pl.when(j < n_rope_tiles)
        def _():
            write(True)

        @pl.when(j >= n_rope_tiles)
        def _():
            write(False)


def _mm(a, w, *, B, S, head_major_out=False, rope=None, n_rope_cols=0, dilations=()):
    K, N = w.shape
    T = B * S
    H = K // HEAD_DIM
    tm = _tile(MM_TM, S)
    tn = _tile(2 * MM_TN if dilations else MM_TN, N)
    nsb = S // tm
    cols_outer = bool(dilations)
    ix = (lambda f: (lambda j, i: f(i, j))) if cols_outer else (lambda f: f)
    a_heads = H if a.ndim == 4 else 0
    if a_heads:
        a_spec = pl.BlockSpec((1, H, tm, HEAD_DIM), ix(lambda i, j: (i // nsb, 0, i % nsb, 0)))
    else:
        a_spec = pl.BlockSpec((tm, K), ix(lambda i, j: (i, 0)))
    w_mode = dict(pipeline_mode=pl.Buffered(1)) if cols_outer else {}
    in_specs = [a_spec, pl.BlockSpec((K, tn), ix(lambda i, j: (0, j)), **w_mode)]
    args = [a, w]
    out_heads = tn // HEAD_DIM if head_major_out else 0
    n_rope_tiles = 0
    if head_major_out:
        D = H * HEAD_DIM
        G = N // D
        tiles_per_group = D // tn
        if rope is not None:
            assert n_rope_cols % tn == 0
            n_rope_tiles = n_rope_cols // tn
            tab = pl.BlockSpec((tm, HEAD_DIM), ix(lambda i, j: (i, 0)))
            in_specs += [tab, tab]
            args += list(rope)
        head_index = ix(lambda i, j: (j // tiles_per_group, i // nsb, j % tiles_per_group,
                                      i % nsb, 0))
        out_spec = [pl.BlockSpec((1, 1, out_heads, tm // d, d * HEAD_DIM), head_index)
                    for d in (1,) + tuple(dilations)]
        out_shape = [jax.ShapeDtypeStruct((G, B, H, S // d, d * HEAD_DIM), BF16)
                     for d in (1,) + tuple(dilations)]
    else:
        out_spec = pl.BlockSpec((tm, tn), ix(lambda i, j: (i, j)))
        out_shape = jax.ShapeDtypeStruct((T, N), F32)
    heads_per_dot = min(out_heads, MM_TN // HEAD_DIM)
    scratch = [pltpu.VMEM((heads_per_dot, tm, HEAD_DIM), F32)] if dilations else []
    out = pl.pallas_call(
        functools.partial(_mm_kernel, a_heads=a_heads, out_heads=out_heads,
                          n_rope_tiles=n_rope_tiles, dilations=tuple(dilations),
                          col_axis=0 if cols_outer else 1),
        grid=(N // tn, T // tm) if cols_outer else (T // tm, N // tn),
        in_specs=in_specs,
        out_specs=out_spec,
        out_shape=out_shape,
        scratch_shapes=scratch,
        compiler_params=_cparams("parallel", "parallel",
                                 vmem_bytes=VMEM_LIMIT_BIG_TILES if dilations else VMEM_LIMIT_BYTES),
        name="proj",
    )(*args)
    if head_major_out and not dilations:
        return out[0]
    return out


def _dil_attn_kernel(q1, q4, q16, k1c, k4c, k16c, k1p, k4p, k16p, v1c, v4c, v16c, v1p, v4p, v16p,
                     o_ref, *scratch, scale):
    acc_s, m_s, l_s = scratch[0:4], scratch[4:8], scratch[8:12]
    bias_s, s_s, p_s, mu_s, st_acc, st_m, st_l, out_s = scratch[12:20]

    qi = lax.broadcasted_iota(I32, (BLOCK, 2 * BLOCK), 0)
    kc = lax.broadcasted_iota(I32, (BLOCK, 2 * BLOCK), 1)
    band = jnp.where(kc >= qi, jnp.where(kc <= qi + BLOCK, 0.0, NEG_INF), NEG_INF).astype(F32)
    bias_s[0] = band
    no_prev = jnp.where(pl.program_id(2) == 0, NEG_INF, 0.0).astype(F32)
    bias_s[1] = jnp.where(kc < BLOCK, band + no_prev, band)
    ones = jnp.ones((2 * BLOCK, HEAD_DIM), BF16)

    def run(units):
        for g, (q, kp, kc_, _, _, first, _) in enumerate(units):
            keys = jnp.concatenate([kp(), kc_()], axis=0)
            s_s[g] = _dot_nt(q(), keys) * scale + bias_s[1 if first else 0]
        for g in range(len(units)):
            s = s_s[g]
            m = jnp.max(jnp.maximum(s[:, :BLOCK], s[:, BLOCK:]), axis=1, keepdims=True)
            p_s[g] = jnp.exp(s - m).astype(BF16)
            mu_s[g] = jnp.broadcast_to(m, (BLOCK, HEAD_DIM))
        for g, (_, _, _, vp, vc, _, sink) in enumerate(units):
            vals = jnp.concatenate([vp(), vc()], axis=0)
            pv = _dot(p_s[g], jnp.concatenate([vals, ones], axis=1))
            sink(g, pv[:, :HEAD_DIM], mu_s[g], pv[:, HEAD_DIM:])

    def fold(slab, rows, acc_u, m_u, l_u):
        m_o = m_s[slab][rows, :]
        m_n = jnp.maximum(m_o, m_u)
        e_o = jnp.exp(m_o - m_n)
        e_u = jnp.exp(m_u - m_n)
        acc_s[slab][rows, :] = acc_s[slab][rows, :] * e_o + acc_u * e_u
        l_s[slab][rows, :] = l_s[slab][rows, :] * e_o + l_u * e_u
        m_s[slab][rows, :] = m_n

    def tile(ref, rows, lanes):
        return lambda: ref[0, 0, 0, rows, lanes]

    def banded(q, kc_, kp, vc, vp, n, lanes, sink):
        cur = slice(n * BLOCK, (n + 1) * BLOCK)
        if n == 0:
            k_prev, v_prev = tile(kp, slice(None), lanes), tile(vp, slice(None), lanes)
        else:
            prv = slice((n - 1) * BLOCK, n * BLOCK)
            k_prev, v_prev = tile(kc_, prv, lanes), tile(vc, prv, lanes)
        return (tile(q, cur, lanes), k_prev, tile(kc_, cur, lanes), v_prev, tile(vc, cur, lanes),
                n == 0, sink)

    units = []
    for n in range(SUPER // 4 // BLOCK):
        for r in range(4):
            def init(g, acc, m, l, r=r, n=n):
                rows = slice(n * BLOCK, (n + 1) * BLOCK)
                acc_s[r][rows, :], m_s[r][rows, :], l_s[r][rows, :] = acc, m, l

            units.append(banded(q4, k4c, k4p, v4c, v4p, n,
                                slice(r * HEAD_DIM, (r + 1) * HEAD_DIM), init))

    for n in range(SUPER // BLOCK):
        def fold1(g, acc, m, l, n=n):
            st_acc[g], st_m[g], st_l[g] = acc, m, l
            rows = slice(n * (BLOCK // 4), (n + 1) * (BLOCK // 4))
            for r in range(4):
                pick = pl.ds(r, BLOCK // 4, stride=4)
                fold(r, rows, st_acc[g, pick, :], st_m[g, pick, :], st_l[g, pick, :])

        units.append(banded(q1, k1c, k1p, v1c, v1p, n, slice(None), fold1))

    for r in range(16):
        def fold16(g, acc, m, l, r=r):
            fold(r % 4, pl.ds(r // 4, BLOCK, stride=4), acc, m, l)

        units.append(banded(q16, k16c, k16p, v16c, v16p, 0,
                            slice(r * HEAD_DIM, (r + 1) * HEAD_DIM), fold16))

    for start in range(0, len(units), DIL_GROUP):
        run(units[start:start + DIL_GROUP])

    for r in range(4):
        out_s[pl.ds(r, SUPER // 4, stride=4), :] = acc_s[r][...] / l_s[r][...]
    o_ref[0, 0] = out_s[...].astype(BF16)


def _dilated_attention(qkv1, qkv4, qkv16):
    _, B, H, S, Dh = qkv1.shape
    assert Dh == HEAD_DIM and S % SUPER == 0
    assert tuple(d for _, d in DILATED_BRANCHES) == DILATIONS
    assert all(w // d == BLOCK for w, d in DILATED_BRANCHES)
    views = {1: qkv1, 4: qkv4, 16: qkv16}

    def cur(which, d):
        return pl.BlockSpec((1, 1, 1, SUPER // d, d * Dh), lambda b, h, m: (which, b, h, m, 0))

    def prev(which, d):
        per = SUPER // d // BLOCK
        return pl.BlockSpec((1, 1, 1, BLOCK, d * Dh),
                            lambda b, h, m: (which, b, h, jnp.maximum(m * per - 1, 0), 0))

    in_specs, args = [], []
    for make, which in ((cur, 0), (cur, 1), (prev, 1), (cur, 2), (prev, 2)):
        for d in (1, 4, 16):
            in_specs.append(make(which, d))
            args.append(views[d])
    state = [pltpu.VMEM((SUPER // 4, Dh), F32)] * 12
    stage = [pltpu.VMEM((2, BLOCK, 2 * BLOCK), F32),
             pltpu.VMEM((DIL_GROUP, BLOCK, 2 * BLOCK), F32),
             pltpu.VMEM((DIL_GROUP, BLOCK, 2 * BLOCK), BF16),
             pltpu.VMEM((DIL_GROUP, BLOCK, Dh), F32)] + [
             pltpu.VMEM((DIL_GROUP, BLOCK, Dh), F32)] * 3 + [
             pltpu.VMEM((SUPER, Dh), F32)]
    return pl.pallas_call(
        functools.partial(_dil_attn_kernel, scale=Dh ** -0.5),
        grid=(B, H, S // SUPER),
        in_specs=in_specs,
        out_specs=pl.BlockSpec((1, 1, SUPER, Dh), lambda b, h, m: (b, h, m, 0)),
        out_shape=jax.ShapeDtypeStruct((B, H, S, Dh), BF16),
        scratch_shapes=state + stage,
        compiler_params=_cparams("parallel", "parallel", "parallel"),
        name="dilated_attention",
    )(*args)


def _sb_kernel(q_ref, k_ref, v_ref, o_ref, tri_ref, acc_ref, carry_ref, logb_s, rsum_s, tail_s,
               hi_s, lo_s, a_s, *, scale, tq, nq):
    row = lax.broadcasted_iota(I32, (tq, tq), 0)
    col = lax.broadcasted_iota(I32, (tq, tq), 1)
    tri_ref[...] = jnp.where(row > col, 1.0, 0.0).astype(BF16)

    def block_rows(n):
        return pl.ds(pl.multiple_of(n * tq, tq), tq)

    def sweep(n0, i):
        diagonal = isinstance(i, int) and i == 0
        causal = col < row
        keys = [block_rows(jnp.maximum(n0 + g - i, 0)) for g in range(SB_GROUP)]
        for g in range(SB_GROUP):
            z2 = _dot_nt(q_ref[0, 0, block_rows(n0 + g), :], k_ref[0, 0, 0, keys[g], :])
            z2 = z2 * (scale * LOG2_E)
            t = jnp.log2(1.0 + jnp.exp2(-jnp.abs(z2)))
            log_1mb = -(jnp.maximum(z2, 0.0) + t)
            if diagonal:
                log_1mb = jnp.where(causal, log_1mb, 0.0)
            logb_s[g] = jnp.minimum(z2, 0.0) - t
            hi = log_1mb.astype(BF16)
            hi_s[g] = hi
            lo_s[g] = (log_1mb - hi.astype(F32)).astype(BF16)
            rsum_s[g] = jnp.broadcast_to(jnp.sum(log_1mb, axis=1, keepdims=True), (tq, tq))
        tail_s[...] = (_dot(hi_s[...].reshape(SB_GROUP * tq, tq), tri_ref[...])
                       + _dot(lo_s[...].reshape(SB_GROUP * tq, tq), tri_ref[...])
                       ).reshape(SB_GROUP, tq, tq)
        tops = []
        for g in range(SB_GROUP):
            tail = tail_s[g]
            if diagonal:
                a = jnp.where(causal, jnp.exp2(logb_s[g] + tail), 0.0)
                c = rsum_s[g]
            else:
                c = carry_ref[g]
                a = jnp.where(n0 + g - i >= 0, jnp.exp2(logb_s[g] + tail + c), 0.0)
                c = c + rsum_s[g]
            a_s[g] = a.astype(BF16)
            carry_ref[g] = c
            tops.append(jnp.where(n0 + g - i >= 1, jnp.max(c), NEG_INF))
        for g in range(SB_GROUP):
            av = _dot(a_s[g], v_ref[0, 0, 0, keys[g], :])
            acc_ref[g] = av if diagonal else acc_ref[g] + av
        return functools.reduce(jnp.maximum, tops)

    def qgroup(it, carry):
        n0 = it * SB_GROUP

        def cond(st):
            _, top = st
            return top > -SB_SKIP

        def body(st):
            i, _ = st
            return i + 1, sweep(n0, i)

        lax.while_loop(cond, body, (1, sweep(n0, 0)))
        for g in range(SB_GROUP):
            o_ref[0, 0, block_rows(n0 + g), :] = acc_ref[g].astype(BF16)
        return carry

    lax.fori_loop(0, nq // SB_GROUP, qgroup, 0)


def _stick_breaking_attention(q, kv):
    _, B, H, S, Dh = q.shape
    tq = _tile(SB_TQ, S)
    seq = lambda which: pl.BlockSpec((1, 1, 1, S, Dh), lambda b, h: (which, b, h, 0, 0))
    return pl.pallas_call(
        functools.partial(_sb_kernel, scale=Dh ** -0.5, tq=tq, nq=S // tq),
        grid=(B, H),
        in_specs=[pl.BlockSpec((1, 1, S, Dh), lambda b, h: (b, h, 0, 0)), seq(0), seq(1)],
        out_specs=pl.BlockSpec((1, 1, S, Dh), lambda b, h: (b, h, 0, 0)),
        out_shape=jax.ShapeDtypeStruct((B, H, S, Dh), BF16),
        scratch_shapes=[pltpu.VMEM((tq, tq), BF16), pltpu.VMEM((SB_GROUP, tq, Dh), F32)]
        + [pltpu.VMEM((SB_GROUP, tq, tq), F32)] * 4 + [pltpu.VMEM((SB_GROUP, tq, tq), BF16)] * 3,
        compiler_params=_cparams("parallel", "parallel"),
        name="stick_breaking_attention",
    )(q.reshape(B, H, S, Dh), kv, kv)


def _router_kernel(h_ref, rw_ref, rb_ref, e_ref, r_ref, w_ref, cnt_ref, tri_ref, carry_ref,
                   *, n_exp, tm, nsteps):
    i = pl.program_id(0)
    gsz = n_exp // N_GROUPS

    @pl.when(i == 0)
    def _():
        row = lax.broadcasted_iota(I32, (tm, tm), 0)
        col = lax.broadcasted_iota(I32, (tm, tm), 1)
        tri_ref[...] = jnp.where(row < col, 1.0, 0.0).astype(BF16)
        carry_ref[...] = jnp.zeros_like(carry_ref)

    lo, hi = _unpack_pair(_load_slab_words(h_ref, tm))
    half = lo.shape[1]
    logits = _dot_nt(rw_ref[:, :half], lo) + _dot_nt(rw_ref[:, half:], hi)
    scores = jax.nn.sigmoid(logits)
    sel = scores + rb_ref[...]

    sub = lax.broadcasted_iota(I32, (gsz, tm), 0)
    groups, gscore = [], []
    for g in range(N_GROUPS):
        sg = sel[g * gsz:(g + 1) * gsz, :]
        m1 = jnp.max(sg, axis=0, keepdims=True)
        i1 = jnp.min(jnp.where(sg == m1, sub, gsz), axis=0, keepdims=True)
        m2 = jnp.max(jnp.where(sub == i1, NEG_INF, sg), axis=0, keepdims=True)
        groups.append(sg)
        gscore.append(m1 + m2)
    masked = []
    for g in range(N_GROUPS):
        beat = jnp.zeros((1, tm), F32)
        for o in range(N_GROUPS):
            if o == g:
                continue
            beat += jnp.where(gscore[o] > gscore[g], 1.0, 0.0)
            if o < g:
                beat += jnp.where(gscore[o] == gscore[g], 1.0, 0.0)
        masked.append(jnp.where(beat < TOPK_GROUPS, groups[g], NEG_INF))
    x = jnp.concatenate(masked, axis=0)

    eid = lax.broadcasted_iota(I32, (n_exp, tm), 0)
    chosen = jnp.zeros((n_exp, tm), F32)
    picks, vals = [], []
    for _ in range(TOP_K):
        mx = jnp.max(x, axis=0, keepdims=True)
        idx = jnp.min(jnp.where(x == mx, eid, n_exp), axis=0, keepdims=True)
        hit = eid == idx
        picks.append(idx)
        vals.append(jnp.sum(jnp.where(hit, scores, 0.0), axis=0, keepdims=True))
        chosen += jnp.where(hit, 1.0, 0.0)
        x = jnp.where(hit, NEG_INF, x)
    total = vals[0]
    for v in vals[1:]:
        total = total + v

    rank_all = _dot(chosen.astype(BF16), tri_ref[...]) + carry_ref[...]
    for k in range(TOP_K):
        e_ref[k:k + 1, :] = picks[k]
        w_ref[k:k + 1, :] = vals[k] / total * ROUTED_SCALE
        rk = jnp.sum(jnp.where(eid == picks[k], rank_all, 0.0), axis=0, keepdims=True)
        r_ref[k:k + 1, :] = rk.astype(I32)
    carry_ref[...] += jnp.broadcast_to(jnp.sum(chosen, axis=1, keepdims=True), (n_exp, tm))

    @pl.when(i == nsteps - 1)
    def _():
        cnt_ref[...] = carry_ref[:, 0:LANES].astype(I32)


def _router(h32, rw_t, rb):
    E, D = rw_t.shape
    ch = D // 2 // LANES
    T = h32.shape[0] // ch
    tm = _tile(ROUTER_TM, T)
    nsteps = T // tm
    tok = pl.BlockSpec((TOP_K, tm), lambda i: (0, i))
    return pl.pallas_call(
        functools.partial(_router_kernel, n_exp=E, tm=tm, nsteps=nsteps),
        grid=(nsteps,),
        in_specs=[pl.BlockSpec((tm * ch, LANES), lambda i: (i, 0)),
                  pl.BlockSpec((E, D), lambda i: (0, 0)),
                  pl.BlockSpec((E, 1), lambda i: (0, 0))],
        out_specs=[tok, tok, tok, pl.BlockSpec((E, LANES), lambda i: (0, 0))],
        out_shape=[jax.ShapeDtypeStruct((TOP_K, T), I32), jax.ShapeDtypeStruct((TOP_K, T), I32),
                   jax.ShapeDtypeStruct((TOP_K, T), F32), jax.ShapeDtypeStruct((E, LANES), I32)],
        scratch_shapes=[pltpu.VMEM((tm, tm), BF16), pltpu.VMEM((E, tm), F32)],
        compiler_params=_cparams("arbitrary"),
        name="moe_router",
    )(h32, rw_t, rb)


def _dispatch_kernel(last_ref, slots_ref, h_ref, xs_hbm, zbuf, zsem, sem, *, n_exp, td, ch):
    i = pl.program_id(0)
    tile_rows = zbuf.shape[0]

    def zero_copy(e):
        dst = pl.ds(pl.multiple_of(last_ref[e], tile_rows), tile_rows)
        return pltpu.make_async_copy(zbuf, xs_hbm.at[dst], zsem)

    @pl.when(i == 0)
    def _():
        zbuf[...] = jnp.zeros_like(zbuf)

        def start(e, c):
            @pl.when(last_ref[e] >= 0)
            def _():
                zero_copy(e).start()
            return c

        def wait(e, c):
            @pl.when(last_ref[e] >= 0)
            def _():
                zero_copy(e).wait()
            return c

        lax.fori_loop(0, n_exp, start, 0)
        lax.fori_loop(0, n_exp, wait, 0)

    def row_copy(t, k):
        src = pl.ds(pl.multiple_of(t * ch, ch), ch)
        dst = pl.ds(pl.multiple_of(slots_ref[k, t], ch), ch)
        return pltpu.make_async_copy(h_ref.at[src], xs_hbm.at[dst], sem)

    def start_rows(t, c):
        for k in range(TOP_K):
            row_copy(t, k).start(priority=k % 2)
        return c

    def wait_rows(t, c):
        for k in range(TOP_K):
            row_copy(t, k).wait()
        return c

    lax.fori_loop(0, td, start_rows, 0)
    lax.fori_loop(0, td, wait_rows, 0)


def _dispatch(h32, slot_rows, last_rows, n_slots, tm_e, ch):
    T = slot_rows.shape[1]
    td = _tile(DISPATCH_TD, T)
    E = last_rows.shape[0]
    return pl.pallas_call(
        functools.partial(_dispatch_kernel, n_exp=E, td=td, ch=ch),
        grid_spec=pltpu.PrefetchScalarGridSpec(
            num_scalar_prefetch=1,
            grid=(T // td,),
            in_specs=[pl.BlockSpec((TOP_K, td), lambda i, last: (0, i), memory_space=pltpu.SMEM),
                      pl.BlockSpec((td * ch, LANES), lambda i, last: (i, 0))],
            out_specs=pl.BlockSpec(memory_space=pl.ANY),
            scratch_shapes=[pltpu.VMEM((tm_e * ch, LANES), U32), pltpu.SemaphoreType.DMA,
                            pltpu.SemaphoreType.DMA]),
        out_shape=jax.ShapeDtypeStruct((n_slots * ch, LANES), U32),
        compiler_params=_cparams("arbitrary"),
        name="moe_dispatch",
    )(last_rows, slot_rows, h32)


def _mlp_hidden(x_ref, wgu_ref, n_tok):
    f = wgu_ref.shape[1] // 2
    lo, hi = _unpack_pair(_load_slab_words(x_ref, n_tok))
    half = lo.shape[1]
    gu = _dot(lo, wgu_ref[:half, :]) + _dot(hi, wgu_ref[half:, :])
    return (jax.nn.silu(gu[:, :f]) * gu[:, f:]).astype(BF16)


def _expert_kernel(te_ref, nx_ref, nt_ref, x_ref, wg_hbm, wu_hbm, wd_hbm, y_ref,
                   wg_f, wu_f, wd_f, wgu_s, wd_s, sems, *, tm_e, layer):
    i = pl.program_id(0)
    f, d_model = wd_s.shape
    half = d_model // 2
    ch = half // LANES

    def weight_copies(e):
        return (pltpu.make_async_copy(wg_hbm.at[layer, e], wg_f, sems.at[0]),
                pltpu.make_async_copy(wu_hbm.at[layer, e], wu_f, sems.at[1]),
                pltpu.make_async_copy(wd_hbm.at[layer, e], wd_f, sems.at[2]))

    @pl.when(i < nt_ref[0])
    def _():
        expert = te_ref[i]

        @pl.when(i == 0)
        def _():
            for cp in weight_copies(expert):
                cp.start()

        @pl.when(jnp.logical_or(i == 0, expert != te_ref[jnp.maximum(i - 1, 0)]))
        def _():
            for cp in weight_copies(expert):
                cp.wait()
            wgu_s[:, :f] = wg_f[...].astype(BF16)
            wgu_s[:, f:] = wu_f[...].astype(BF16)
            for c in range(ch):
                lo_cols = slice(c * LANES, (c + 1) * LANES)
                hi_cols = slice(half + c * LANES, half + (c + 1) * LANES)
                wd_s[:, 2 * c * LANES:(2 * c + 1) * LANES] = wd_f[:, lo_cols].astype(BF16)
                wd_s[:, (2 * c + 1) * LANES:(2 * c + 2) * LANES] = wd_f[:, hi_cols].astype(BF16)

            @pl.when(nx_ref[i] >= 0)
            def _():
                for cp in weight_copies(nx_ref[i]):
                    cp.start()

        a = _mlp_hidden(x_ref, wgu_s, tm_e)
        for c in range(ch):
            yc = _dot(a, wd_s[:, 2 * c * LANES:(2 * c + 2) * LANES])
            y_ref[pl.ds(c, tm_e, stride=ch), :] = _pack_pair(yc[:, :LANES], yc[:, LANES:])


def _shared_kernel(x_ref, wgu_ref, wd_ref, y_ref, *, tm):
    y_ref[...] = _dot(_mlp_hidden(x_ref, wgu_ref, tm), wd_ref[...])


def _experts(xs, layer, w_gate, w_up, w_down, tile_expert, next_expert, n_tiles, tm_e, ch):
    _, E, D, F = w_gate.shape
    n_slots = xs.shape[0] // ch
    rows = pl.BlockSpec((tm_e * ch, LANES),
                        lambda i, te, nx, nt: (jnp.minimum(i, nt[0] - 1), 0))
    hbm = pl.BlockSpec(memory_space=pl.ANY)
    return pl.pallas_call(
        functools.partial(_expert_kernel, tm_e=tm_e, layer=layer),
        grid_spec=pltpu.PrefetchScalarGridSpec(
            num_scalar_prefetch=3,
            grid=(n_slots // tm_e,),
            in_specs=[rows, hbm, hbm, hbm],
            out_specs=rows,
            scratch_shapes=[pltpu.VMEM((D, F), F32), pltpu.VMEM((D, F), F32),
                            pltpu.VMEM((F, D), F32), pltpu.VMEM((D, 2 * F), BF16),
                            pltpu.VMEM((F, D), BF16), pltpu.SemaphoreType.DMA((3,))]),
        out_shape=jax.ShapeDtypeStruct(xs.shape, U32),
        compiler_params=_cparams("arbitrary"),
        name="moe_experts",
    )(tile_expert, next_expert, n_tiles, xs, w_gate, w_up, w_down)


def _shared_expert(h32, wgu, wd, ch):
    D, F2 = wgu.shape
    T = h32.shape[0] // ch
    tm = _tile(512, T)
    return pl.pallas_call(
        functools.partial(_shared_kernel, tm=tm),
        grid=(T // tm,),
        in_specs=[pl.BlockSpec((tm * ch, LANES), lambda i: (i, 0)),
                  pl.BlockSpec((D, F2), lambda i: (0, 0)),
                  pl.BlockSpec((F2 // 2, D), lambda i: (0, 0))],
        out_specs=pl.BlockSpec((tm, D), lambda i: (i, 0)),
        out_shape=jax.ShapeDtypeStruct((T, D), F32),
        compiler_params=_cparams("parallel"),
        name="moe_shared_expert",
    )(h32, wgu, wd)


def _combine_kernel(slots_ref, next_slots_ref, w_ref, ysh_ref, y_hbm, o_ref, bufs, gate_s, sems,
                    *, tc, ch, nsteps):
    i = pl.program_id(0)
    slot = i % 2

    def row_copy(table, which, t, k):
        src = pl.ds(pl.multiple_of(table[k, t], ch), ch)
        dst = pl.ds(pl.multiple_of(t * ch, ch), ch)
        return pltpu.make_async_copy(y_hbm.at[src], bufs.at[which, k, dst], sems.at[which])

    def gather_token(table, which, t):
        for k in range(TOP_K):
            row_copy(table, which, t, k).start(priority=k % 2)

    def wait_all(table, which):
        def wait_rows(t, c):
            for k in range(TOP_K):
                row_copy(table, which, t, k).wait()
            return c

        lax.fori_loop(0, tc, wait_rows, 0)

    @pl.when(i == 0)
    def _():
        def start_rows(t, c):
            gather_token(slots_ref, 0, t)
            return c

        lax.fori_loop(0, tc, start_rows, 0)

    wait_all(slots_ref, slot)

    half = ch * LANES
    w_cols = w_ref[...].T
    for k in range(TOP_K):
        gate_s[k] = jnp.broadcast_to(w_cols[:, k:k + 1], (tc, LANES))
    per_chunk = tc // ch
    for c in range(ch):
        lanes_lo = slice(c * LANES, (c + 1) * LANES)
        lanes_hi = slice(half + c * LANES, half + (c + 1) * LANES)
        acc_lo = ysh_ref[:, lanes_lo]
        acc_hi = ysh_ref[:, lanes_hi]
        for k in range(TOP_K):
            lo, hi = _unpack_pair_f32(bufs.at[slot, k][pl.ds(c, tc, stride=ch), :])
            acc_lo = acc_lo + gate_s[k] * lo
            acc_hi = acc_hi + gate_s[k] * hi
        o_ref[:, lanes_lo] = acc_lo
        o_ref[:, lanes_hi] = acc_hi
        for t in range(c * per_chunk, (c + 1) * per_chunk):
            gather_token(next_slots_ref, 1 - slot, t)

    @pl.when(i == nsteps - 1)
    def _():
        wait_all(next_slots_ref, 1 - slot)


def _combine(slot_rows, wts, ysh, ys, ch):
    T, D = ysh.shape
    tc = _tile(COMBINE_TC, T)
    nsteps = T // tc
    return pl.pallas_call(
        functools.partial(_combine_kernel, tc=tc, ch=ch, nsteps=nsteps),
        grid=(nsteps,),
        in_specs=[pl.BlockSpec((TOP_K, tc), lambda i: (0, i), memory_space=pltpu.SMEM),
                  pl.BlockSpec((TOP_K, tc), lambda i: (0, jnp.minimum(i + 1, nsteps - 1)),
                               memory_space=pltpu.SMEM),
                  pl.BlockSpec((TOP_K, tc), lambda i: (0, i)),
                  pl.BlockSpec((tc, D), lambda i: (i, 0)),
                  pl.BlockSpec(memory_space=pl.ANY)],
        out_specs=pl.BlockSpec((tc, D), lambda i: (i, 0)),
        out_shape=jax.ShapeDtypeStruct((T, D), F32),
        scratch_shapes=[pltpu.VMEM((2, TOP_K, tc * ch, LANES), U32),
                        pltpu.VMEM((TOP_K, tc, LANES), F32),
                        pltpu.SemaphoreType.DMA((2,))],
        compiler_params=_cparams("arbitrary"),
        name="moe_combine",
    )(slot_rows, slot_rows, wts, ysh, ys)


def _moe(h32, layer, rw_t, rb, w_gate, w_up, w_down, sh_gu, sh_d):
    E, D = rw_t.shape
    ch = D // 2 // LANES
    assert ch % 8 == 0, "a token slab must be whole (8,128) tiles"
    T = h32.shape[0] // ch
    tm_e = _tile(EXPERT_TM, T)
    eidx, rank, wts, counts = _router(h32, rw_t, rb)

    counts = counts[:, 0]
    tiles = (counts + tm_e - 1) // tm_e
    tile_end = jnp.cumsum(tiles)
    tile_start = tile_end - tiles
    n_tiles = tile_end[-1]
    experts = jnp.arange(E, dtype=I32)
    first_row = jnp.sum(jnp.where(eidx[:, :, None] == experts, tile_start * tm_e, 0), axis=-1)
    slot_rows = (first_row + rank) * ch
    max_tiles = T * TOP_K // tm_e + E
    tile_ids = jnp.minimum(jnp.arange(max_tiles, dtype=I32), n_tiles - 1)
    tile_expert = jnp.sum((tile_ids[:, None] >= tile_end[None, :]).astype(I32), axis=-1)
    group_end = jnp.sum(jnp.where(tile_expert[:, None] == experts, tile_end, 0), axis=-1)
    after = jnp.sum((group_end[:, None] >= tile_end[None, :]).astype(I32), axis=-1)
    next_expert = jnp.where(group_end < n_tiles, after, -1).astype(I32)
    last_rows = jnp.where(tiles > 0, (tile_end - 1) * (tm_e * ch), -1).astype(I32)

    xs = _dispatch(h32, slot_rows, last_rows, max_tiles * tm_e, tm_e, ch)
    ys = _experts(xs, layer, w_gate, w_up, w_down, tile_expert, next_expert,
                  n_tiles.reshape(1).astype(I32), tm_e, ch)
    ysh = _shared_expert(h32, sh_gu, sh_d, ch)
    return _combine(slot_rows, wts, ysh, ys, ch)


def kernel(x, c, positions, ada_w, ada_b, ln_g, ln_b, a_w_qkv, a_w_o, kv_ada_w, kv_ada_b, b_w_kv, b_w_q, b_w_o, router_w, router_b, w_gate, w_up, w_down, sh_gate, sh_up, sh_down):
    B, S, D = x.shape
    T = B * S
    assert D % HEAD_DIM == 0 and DEPTH == ada_w.shape[0] == 2
    xf = x.reshape(T, D)

    c8 = jnp.zeros((8, D), F32).at[:B].set(c)
    mods = _adaln(c8, ada_w, ada_b)[:, :B].reshape(DEPTH, B, 6, D)
    kv_mods = _adaln(c8, kv_ada_w[None], kv_ada_b[None])[0, :B].reshape(B, 2, D)
    sh_a, sc_a, g_a, sh_m, sc_m, g_m = (mods[:, :, r] for r in range(6))
    rope = _rope_tables(positions)

    def moe_weights(l):
        sh_gu = jnp.concatenate([sh_gate[l], sh_up[l]], axis=-1).astype(BF16)
        return (l, router_w[l].T.astype(BF16), router_b[l].reshape(-1, 1), w_gate, w_up, w_down,
                sh_gu, sh_down[l].astype(BF16))

    h = _modulate(xf, _mod_rows(None, [(sh_a[0], sc_a[0])]), S)
    qkv_views = _mm(h, a_w_qkv[0].astype(BF16), B=B, S=S, head_major_out=True, rope=rope,
                    n_rope_cols=2 * D, dilations=DILATIONS[1:])
    o = _dilated_attention(*qkv_views)
    y = _mm(o, a_w_o[0].astype(BF16), B=B, S=S)
    xf, h32 = _ln(xf, y, _mod_rows(g_a[0], [(sh_m[0], sc_m[0])]), ln_g[0, 0], ln_b[0, 0],
                  ["u32"], S)
    y = _moe(h32, *moe_weights(0))
    xf, h_kv, h_q = _ln(xf, y, _mod_rows(g_m[0], [(kv_mods[:, 0], kv_mods[:, 1]),
                                                  (sh_a[1], sc_a[1])]),
                        ln_g[0, 1], ln_b[0, 1], ["bf16", "bf16"], S)

    kv = _mm(h_kv, b_w_kv.astype(BF16), B=B, S=S, head_major_out=True)
    q = _mm(h_q, b_w_q[0].astype(BF16), B=B, S=S, head_major_out=True)
    o = _stick_breaking_attention(q, kv)
    y = _mm(o, b_w_o[0].astype(BF16), B=B, S=S)
    xf, h32 = _ln(xf, y, _mod_rows(g_a[1], [(sh_m[1], sc_m[1])]), ln_g[1, 0], ln_b[1, 0],
                  ["u32"], S)
    y = _moe(h32, *moe_weights(1))
    (xf,) = _ln(xf, y, _mod_rows(g_m[1], []), ln_g[1, 1], ln_b[1, 1], [], S)
    return xf.reshape(B, S, D)
```

```python
import functools
import math

import jax
import jax.numpy as jnp
from jax import lax
from jax.experimental import pallas as pl
from jax.experimental.pallas import tpu as pltpu

F32 = jnp.float32
BF16 = jnp.bfloat16
U32 = jnp.uint32
I32 = jnp.int32

HEAD_DIM = 128
ROT_DIM = HEAD_DIM // 4
ROPE_THETA = 500000.0
DILATED_BRANCHES = ((128, 1), (512, 4), (2048, 16))
DILATIONS = tuple(d for _, d in DILATED_BRANCHES)
BLOCK = 128
TOP_K = 8
N_GROUPS = 8
TOPK_GROUPS = 4
ROUTED_SCALE = 2.5
DEPTH = 2
DEEPNORM_ALPHA = (2 * DEPTH) ** 0.25
LN_EPS = 1e-5

LANES = 128
VMEM_LIMIT_BYTES = 56 * 1024 * 1024
VMEM_LIMIT_BIG_TILES = 58 * 1024 * 1024

MM_TM = 1024
MM_TN = 512
LN_TM = 256
ROUTER_TM = 512
EXPERT_TM = 512
DISPATCH_TD = 256
COMBINE_TC = 128
SUPER = 2048
DIL_GROUP = 8
SB_TQ = 128
SB_GROUP = 16
SB_SKIP = 127.0
LOG2_E = 1.4426950408889634

NEG_INF = float("-inf")
HI_MASK = 0xFFFF0000


def _tile(pref, dim):
    t = min(pref, dim)
    assert dim % t == 0, (pref, dim)
    return t


def _cparams(*sem, vmem_bytes=VMEM_LIMIT_BYTES):
    return pltpu.CompilerParams(dimension_semantics=sem, vmem_limit_bytes=vmem_bytes)


def _dot(a, b):
    return jnp.dot(a, b, preferred_element_type=F32)


def _dot_nt(a, b):
    return lax.dot_general(a, b, (((1,), (1,)), ((), ())), preferred_element_type=F32)


def _unpack_pair(w):
    lo = lax.bitcast_convert_type(w << 16, F32).astype(BF16)
    hi = lax.bitcast_convert_type(w & jnp.uint32(HI_MASK), F32).astype(BF16)
    return lo, hi


def _unpack_pair_f32(w):
    lo = lax.bitcast_convert_type(w << 16, F32)
    hi = lax.bitcast_convert_type(w & jnp.uint32(HI_MASK), F32)
    return lo, hi


def _pack_pair(lo, hi):
    lo_b = lax.bitcast_convert_type(lo.astype(BF16).astype(F32), U32)
    hi_b = lax.bitcast_convert_type(hi.astype(BF16).astype(F32), U32)
    return (hi_b & jnp.uint32(HI_MASK)) | (lo_b >> 16)


def _load_slab_words(ref, n_tok):
    ch = ref.shape[0] // n_tok
    return jnp.concatenate([ref[pl.ds(c, n_tok, stride=ch), :] for c in range(ch)], axis=1)


def _store_slab_words(ref, words):
    n_tok = words.shape[0]
    ch = ref.shape[0] // n_tok
    for c in range(ch):
        ref[pl.ds(c, n_tok, stride=ch), :] = words[:, c * LANES:(c + 1) * LANES]


def _adaln_kernel(c_ref, w_ref, b_ref, o_ref, *, nk):
    k = pl.program_id(2)

    @pl.when(k == 0)
    def _():
        o_ref[...] = jnp.zeros_like(o_ref)

    cond = jax.nn.silu(c_ref[...])
    o_ref[0] += jnp.dot(cond, w_ref[0], preferred_element_type=F32,
                        precision=lax.Precision.HIGHEST)

    @pl.when(k == nk - 1)
    def _():
        o_ref[0] += b_ref[0]


def _adaln(c8, w, b):
    L, D, N = w.shape
    tk = _tile(1024, D)
    tn = _tile(2048, N)
    nk = D // tk
    return pl.pallas_call(
        functools.partial(_adaln_kernel, nk=nk),
        grid=(L, N // tn, nk),
        in_specs=[pl.BlockSpec((8, tk), lambda l, j, k: (0, k)),
                  pl.BlockSpec((1, tk, tn), lambda l, j, k: (l, k, j)),
                  pl.BlockSpec((1, 1, tn), lambda l, j, k: (l, 0, j))],
        out_specs=pl.BlockSpec((1, 8, tn), lambda l, j, k: (l, 0, j)),
        out_shape=jax.ShapeDtypeStruct((L, 8, N), F32),
        compiler_params=_cparams("parallel", "parallel", "arbitrary"),
        name="adaln",
    )(c8, w, b.reshape(L, 1, N))


def _rope_kernel(pos_ref, freq_ref, sign_ref, cos_ref, sin_ref):
    ang = pos_ref[...] * freq_ref[...]
    cos_ref[...] = jnp.cos(ang)
    sin_ref[...] = jnp.sin(ang) * sign_ref[...]


def _rope_tables(positions):
    T = positions.size
    half = ROT_DIM // 2
    inv_freq = ROPE_THETA ** (-jnp.arange(0, ROT_DIM, 2, dtype=F32) / ROT_DIM)
    freq = jnp.concatenate([inv_freq, inv_freq, jnp.zeros((HEAD_DIM - ROT_DIM,), F32)])[None, :]
    sign = jnp.concatenate([-jnp.ones((half,), F32), jnp.ones((HEAD_DIM - half,), F32)])[None, :]
    pos = jnp.broadcast_to(positions.astype(F32).reshape(T, 1), (T, HEAD_DIM))
    tm = _tile(2048, T)
    row = pl.BlockSpec((tm, HEAD_DIM), lambda i: (i, 0))
    vec = pl.BlockSpec((1, HEAD_DIM), lambda i: (0, 0))
    return pl.pallas_call(
        _rope_kernel,
        grid=(T // tm,),
        in_specs=[row, vec, vec],
        out_specs=[row, row],
        out_shape=[jax.ShapeDtypeStruct((T, HEAD_DIM), F32)] * 2,
        compiler_params=_cparams("parallel"),
        name="rope_tables",
    )(pos, freq, sign)


def _emit_mod(xn, rows_ref, idx, fmt, out_ref):
    shift = rows_ref[0, 1 + 2 * idx:2 + 2 * idx, :]
    scale = rows_ref[0, 2 + 2 * idx:3 + 2 * idx, :]
    h = xn * (1.0 + scale) + shift
    if fmt == "bf16":
        out_ref[...] = h.astype(BF16)
    else:
        half = h.shape[1] // 2
        _store_slab_words(out_ref, _pack_pair(h[:, :half], h[:, half:]))


def _modulate_kernel(x_ref, rows_ref, o_ref):
    _emit_mod(x_ref[...], rows_ref, 0, "bf16", o_ref)


def _ln_kernel(x_ref, y_ref, rows_ref, g_ref, b_ref, xo_ref, *h_refs, fmts):
    gate = rows_ref[0, 0:1, :]
    z = DEEPNORM_ALPHA * x_ref[...] + (1.0 + gate) * y_ref[...]
    mu = jnp.mean(z, axis=-1, keepdims=True)
    zc = z - mu
    var = jnp.mean(zc * zc, axis=-1, keepdims=True)
    xn = zc * lax.rsqrt(var + LN_EPS) * g_ref[...] + b_ref[...]
    xo_ref[...] = xn
    for idx, (fmt, h_ref) in enumerate(zip(fmts, h_refs)):
        _emit_mod(xn, rows_ref, idx, fmt, h_ref)


def _mod_rows(gate, pairs):
    B, D = pairs[0][0].shape if pairs else gate.shape
    rows = [gate if gate is not None else jnp.zeros((B, D), F32)]
    for shift, scale in pairs:
        rows += [shift, scale]
    rows += [jnp.zeros((B, D), F32)] * (8 - len(rows))
    return jnp.stack(rows, axis=1)


def _out_struct(T, D, tm, fmt):
    if fmt == "bf16":
        return jax.ShapeDtypeStruct((T, D), BF16), (tm, D)
    ch = D // 2 // LANES
    return jax.ShapeDtypeStruct((T * ch, LANES), U32), (tm * ch, LANES)


def _modulate(x, rows, S):
    T, D = x.shape
    tm = _tile(LN_TM, S)
    nb = S // tm
    return pl.pallas_call(
        _modulate_kernel,
        grid=(T // tm,),
        in_specs=[pl.BlockSpec((tm, D), lambda i: (i, 0)),
                  pl.BlockSpec((1, 8, D), lambda i: (i // nb, 0, 0))],
        out_specs=pl.BlockSpec((tm, D), lambda i: (i, 0)),
        out_shape=jax.ShapeDtypeStruct((T, D), BF16),
        compiler_params=_cparams("parallel"),
        name="modulate",
    )(x, rows)


def _ln(x, y, rows, g, b, fmts, S):
    T, D = x.shape
    tm = _tile(LN_TM, S)
    nb = S // tm
    row = pl.BlockSpec((tm, D), lambda i: (i, 0))
    vec = pl.BlockSpec((1, D), lambda i: (0, 0))
    out_shape = [jax.ShapeDtypeStruct((T, D), F32)]
    out_specs = [row]
    for fmt in fmts:
        st, blk = _out_struct(T, D, tm, fmt)
        out_shape.append(st)
        out_specs.append(pl.BlockSpec(blk, lambda i: (i, 0)))
    return pl.pallas_call(
        functools.partial(_ln_kernel, fmts=tuple(fmts)),
        grid=(T // tm,),
        in_specs=[row, row, pl.BlockSpec((1, 8, D), lambda i: (i // nb, 0, 0)), vec, vec],
        out_specs=out_specs,
        out_shape=out_shape,
        compiler_params=_cparams("parallel"),
        name="deepnorm_ln",
    )(x, y, rows, g.reshape(1, D), b.reshape(1, D))


def _swap_halves(x):
    half = ROT_DIM // 2
    lane = lax.broadcasted_iota(I32, x.shape, 1)
    return jnp.where(lane < half, pltpu.roll(x, HEAD_DIM - half, 1), pltpu.roll(x, half, 1))


def _mm_kernel(*refs, a_heads, out_heads, n_rope_tiles, dilations, col_axis):
    a_ref, w_ref = refs[:2]
    refs = refs[2:]
    if n_rope_tiles:
        cos_ref, sin_ref = refs[:2]
        refs = refs[2:]
    o_ref = refs[0]
    view_refs = refs[1:1 + len(dilations)]
    stage = refs[1 + len(dilations)] if dilations else None
    if a_heads:
        a = jnp.concatenate([a_ref[0, h] for h in range(a_heads)], axis=-1)
    else:
        a = a_ref[...]
    if not out_heads:
        o_ref[...] = _dot(a, w_ref[...]).astype(o_ref.dtype)
        return
    tm = a.shape[0]
    heads_per_dot = min(out_heads, MM_TN // HEAD_DIM)

    def write(rot):
        for h0 in range(0, out_heads, heads_per_dot):
            acc = _dot(a, w_ref[:, h0 * HEAD_DIM:(h0 + heads_per_dot) * HEAD_DIM])
            for hh in range(h0, h0 + heads_per_dot):
                xh = acc[:, (hh - h0) * HEAD_DIM:(hh - h0 + 1) * HEAD_DIM]
                if rot:
                    xh = xh * cos_ref[...] + _swap_halves(xh) * sin_ref[...]
                o_ref[0, 0, hh] = xh.astype(BF16)
                if dilations:
                    stage[hh - h0] = xh
                    for d, v_ref in zip(dilations, view_refs):
                        for r in range(d):
                            v_ref[0, 0, hh, :, r * HEAD_DIM:(r + 1) * HEAD_DIM] = (
                                stage[hh - h0, pl.ds(r, tm // d, stride=d), :].astype(BF16))

    if not n_rope_tiles:
        write(False)
    else:
        j = pl.program_id(col_axis)

        @pl.when(j < n_rope_tiles)
        def _():
            write(True)

        @pl.when(j >= n_rope_tiles)
        def _():
            write(False)


def _mm(a, w, *, B, S, head_major_out=False, rope=None, n_rope_cols=0, dilations=()):
    K, N = w.shape
    T = B * S
    H = K // HEAD_DIM
    tm = _tile(MM_TM, S)
    tn = _tile(2 * MM_TN if dilations else MM_TN, N)
    nsb = S // tm
    cols_outer = bool(dilations)
    ix = (lambda f: (lambda j, i: f(i, j))) if cols_outer else (lambda f: f)
    a_heads = H if a.ndim == 4 else 0
    if a_heads:
        a_spec = pl.BlockSpec((1, H, tm, HEAD_DIM), ix(lambda i, j: (i // nsb, 0, i % nsb, 0)))
    else:
        a_spec = pl.BlockSpec((tm, K), ix(lambda i, j: (i, 0)))
    w_mode = dict(pipeline_mode=pl.Buffered(1)) if cols_outer else {}
    in_specs = [a_spec, pl.BlockSpec((K, tn), ix(lambda i, j: (0, j)), **w_mode)]
    args = [a, w]
    out_heads = tn // HEAD_DIM if head_major_out else 0
    n_rope_tiles = 0
    if head_major_out:
        D = H * HEAD_DIM
        G = N // D
        tiles_per_group = D // tn
        if rope is not None:
            assert n_rope_cols % tn == 0
            n_rope_tiles = n_rope_cols // tn
            tab = pl.BlockSpec((tm, HEAD_DIM), ix(lambda i, j: (i, 0)))
            in_specs += [tab, tab]
            args += list(rope)
        head_index = ix(lambda i, j: (j // tiles_per_group, i // nsb, j % tiles_per_group,
                                      i % nsb, 0))
        out_spec = [pl.BlockSpec((1, 1, out_heads, tm // d, d * HEAD_DIM), head_index)
                    for d in (1,) + tuple(dilations)]
        out_shape = [jax.ShapeDtypeStruct((G, B, H, S // d, d * HEAD_DIM), BF16)
                     for d in (1,) + tuple(dilations)]
    else:
        out_spec = pl.BlockSpec((tm, tn), ix(lambda i, j: (i, j)))
        out_shape = jax.ShapeDtypeStruct((T, N), BF16)
    heads_per_dot = min(out_heads, MM_TN // HEAD_DIM)
    scratch = [pltpu.VMEM((heads_per_dot, tm, HEAD_DIM), F32)] if dilations else []
    out = pl.pallas_call(
        functools.partial(_mm_kernel, a_heads=a_heads, out_heads=out_heads,
                          n_rope_tiles=n_rope_tiles, dilations=tuple(dilations),
                          col_axis=0 if cols_outer else 1),
        grid=(N // tn, T // tm) if cols_outer else (T // tm, N // tn),
        in_specs=in_specs,
        out_specs=out_spec,
        out_shape=out_shape,
        scratch_shapes=scratch,
        compiler_params=_cparams("parallel", "parallel",
                                 vmem_bytes=VMEM_LIMIT_BIG_TILES if dilations else VMEM_LIMIT_BYTES),
        name="proj",
    )(*args)
    if head_major_out and not dilations:
        return out[0]
    return out


def _dil_attn_kernel(q1, q4, q16, k1c, k4c, k16c, k1p, k4p, k16p, v1c, v4c, v16c, v1p, v4p, v16p,
                     o_ref, *scratch, scale):
    acc_s, m_s, l_s = scratch[0:4], scratch[4:8], scratch[8:12]
    bias_s, s_s, p_s, mu_s, st_acc, st_m, st_l, out_s = scratch[12:20]

    qi = lax.broadcasted_iota(I32, (BLOCK, 2 * BLOCK), 0)
    kc = lax.broadcasted_iota(I32, (BLOCK, 2 * BLOCK), 1)
    band = jnp.where(kc >= qi, jnp.where(kc <= qi + BLOCK, 0.0, NEG_INF), NEG_INF).astype(F32)
    bias_s[0] = band
    no_prev = jnp.where(pl.program_id(2) == 0, NEG_INF, 0.0).astype(F32)
    bias_s[1] = jnp.where(kc < BLOCK, band + no_prev, band)
    ones = jnp.ones((2 * BLOCK, HEAD_DIM), BF16)

    def run(units):
        for g, (q, kp, kc_, _, _, first, _) in enumerate(units):
            keys = jnp.concatenate([kp(), kc_()], axis=0)
            s_s[g] = _dot_nt(q(), keys) * scale + bias_s[1 if first else 0]
        for g in range(len(units)):
            s = s_s[g]
            m = jnp.max(jnp.maximum(s[:, :BLOCK], s[:, BLOCK:]), axis=1, keepdims=True)
            p_s[g] = jnp.exp(s - m).astype(BF16)
            mu_s[g] = jnp.broadcast_to(m, (BLOCK, HEAD_DIM))
        for g, (_, _, _, vp, vc, _, sink) in enumerate(units):
            vals = jnp.concatenate([vp(), vc()], axis=0)
            pv = _dot(p_s[g], jnp.concatenate([vals, ones], axis=1))
            sink(g, pv[:, :HEAD_DIM], mu_s[g], pv[:, HEAD_DIM:])

    def fold(slab, rows, acc_u, m_u, l_u):
        m_o = m_s[slab][rows, :]
        m_n = jnp.maximum(m_o, m_u)
        e_o = jnp.exp(m_o - m_n)
        e_u = jnp.exp(m_u - m_n)
        acc_s[slab][rows, :] = acc_s[slab][rows, :] * e_o + acc_u * e_u
        l_s[slab][rows, :] = l_s[slab][rows, :] * e_o + l_u * e_u
        m_s[slab][rows, :] = m_n

    def tile(ref, rows, lanes):
        return lambda: ref[0, 0, 0, rows, lanes]

    def banded(q, kc_, kp, vc, vp, n, lanes, sink):
        cur = slice(n * BLOCK, (n + 1) * BLOCK)
        if n == 0:
            k_prev, v_prev = tile(kp, slice(None), lanes), tile(vp, slice(None), lanes)
        else:
            prv = slice((n - 1) * BLOCK, n * BLOCK)
            k_prev, v_prev = tile(kc_, prv, lanes), tile(vc, prv, lanes)
        return (tile(q, cur, lanes), k_prev, tile(kc_, cur, lanes), v_prev, tile(vc, cur, lanes),
                n == 0, sink)

    units = []
    for n in range(SUPER // 4 // BLOCK):
        for r in range(4):
            def init(g, acc, m, l, r=r, n=n):
                rows = slice(n * BLOCK, (n + 1) * BLOCK)
                acc_s[r][rows, :], m_s[r][rows, :], l_s[r][rows, :] = acc, m, l

            units.append(banded(q4, k4c, k4p, v4c, v4p, n,
                                slice(r * HEAD_DIM, (r + 1) * HEAD_DIM), init))

    for n in range(SUPER // BLOCK):
        def fold1(g, acc, m, l, n=n):
            st_acc[g], st_m[g], st_l[g] = acc, m, l
            rows = slice(n * (BLOCK // 4), (n + 1) * (BLOCK // 4))
            for r in range(4):
                pick = pl.ds(r, BLOCK // 4, stride=4)
                fold(r, rows, st_acc[g, pick, :], st_m[g, pick, :], st_l[g, pick, :])

        units.append(banded(q1, k1c, k1p, v1c, v1p, n, slice(None), fold1))

    for r in range(16):
        def fold16(g, acc, m, l, r=r):
            fold(r % 4, pl.ds(r // 4, BLOCK, stride=4), acc, m, l)

        units.append(banded(q16, k16c, k16p, v16c, v16p, 0,
                            slice(r * HEAD_DIM, (r + 1) * HEAD_DIM), fold16))

    for start in range(0, len(units), DIL_GROUP):
        run(units[start:start + DIL_GROUP])

    for r in range(4):
        out_s[pl.ds(r, SUPER // 4, stride=4), :] = acc_s[r][...] / l_s[r][...]
    o_ref[0, 0] = out_s[...].astype(BF16)


def _dilated_attention(qkv1, qkv4, qkv16):
    _, B, H, S, Dh = qkv1.shape
    assert Dh == HEAD_DIM and S % SUPER == 0
    assert tuple(d for _, d in DILATED_BRANCHES) == DILATIONS
    assert all(w // d == BLOCK for w, d in DILATED_BRANCHES)
    views = {1: qkv1, 4: qkv4, 16: qkv16}

    def cur(which, d):
        return pl.BlockSpec((1, 1, 1, SUPER // d, d * Dh), lambda b, h, m: (which, b, h, m, 0))

    def prev(which, d):
        per = SUPER // d // BLOCK
        return pl.BlockSpec((1, 1, 1, BLOCK, d * Dh),
                            lambda b, h, m: (which, b, h, jnp.maximum(m * per - 1, 0), 0))

    in_specs, args = [], []
    for make, which in ((cur, 0), (cur, 1), (prev, 1), (cur, 2), (prev, 2)):
        for d in (1, 4, 16):
            in_specs.append(make(which, d))
            args.append(views[d])
    state = [pltpu.VMEM((SUPER // 4, Dh), F32)] * 12
    stage = [pltpu.VMEM((2, BLOCK, 2 * BLOCK), F32),
             pltpu.VMEM((DIL_GROUP, BLOCK, 2 * BLOCK), F32),
             pltpu.VMEM((DIL_GROUP, BLOCK, 2 * BLOCK), BF16),
             pltpu.VMEM((DIL_GROUP, BLOCK, Dh), F32)] + [
             pltpu.VMEM((DIL_GROUP, BLOCK, Dh), F32)] * 3 + [
             pltpu.VMEM((SUPER, Dh), F32)]
    return pl.pallas_call(
        functools.partial(_dil_attn_kernel, scale=Dh ** -0.5),
        grid=(B, H, S // SUPER),
        in_specs=in_specs,
        out_specs=pl.BlockSpec((1, 1, SUPER, Dh), lambda b, h, m: (b, h, m, 0)),
        out_shape=jax.ShapeDtypeStruct((B, H, S, Dh), BF16),
        scratch_shapes=state + stage,
        compiler_params=_cparams("parallel", "parallel", "parallel"),
        name="dilated_attention",
    )(*args)


def _sb_kernel(q_ref, k_ref, v_ref, o_ref, tri_ref, acc_ref, carry_ref, logb_s, rsum_s, tail_s,
               hi_s, lo_s, a_s, *, scale, tq, nq):
    row = lax.broadcasted_iota(I32, (tq, tq), 0)
    col = lax.broadcasted_iota(I32, (tq, tq), 1)
    tri_ref[...] = jnp.where(row > col, 1.0, 0.0).astype(BF16)

    def block_rows(n):
        return pl.ds(pl.multiple_of(n * tq, tq), tq)

    def sweep(n0, i):
        diagonal = isinstance(i, int) and i == 0
        causal = col < row
        keys = [block_rows(jnp.maximum(n0 + g - i, 0)) for g in range(SB_GROUP)]
        for g in range(SB_GROUP):
            z2 = _dot_nt(q_ref[0, 0, block_rows(n0 + g), :], k_ref[0, 0, 0, keys[g], :])
            z2 = z2 * (scale * LOG2_E)
            t = jnp.log2(1.0 + jnp.exp2(-jnp.abs(z2)))
            log_1mb = -(jnp.maximum(z2, 0.0) + t)
            if diagonal:
                log_1mb = jnp.where(causal, log_1mb, 0.0)
            logb_s[g] = jnp.minimum(z2, 0.0) - t
            hi = log_1mb.astype(BF16)
            hi_s[g] = hi
            lo_s[g] = (log_1mb - hi.astype(F32)).astype(BF16)
            rsum_s[g] = jnp.broadcast_to(jnp.sum(log_1mb, axis=1, keepdims=True), (tq, tq))
        tail_s[...] = (_dot(hi_s[...].reshape(SB_GROUP * tq, tq), tri_ref[...])
                       + _dot(lo_s[...].reshape(SB_GROUP * tq, tq), tri_ref[...])
                       ).reshape(SB_GROUP, tq, tq)
        tops = []
        for g in range(SB_GROUP):
            tail = tail_s[g]
            if diagonal:
                a = jnp.where(causal, jnp.exp2(logb_s[g] + tail), 0.0)
                c = rsum_s[g]
            else:
                c = carry_ref[g]
                a = jnp.where(n0 + g - i >= 0, jnp.exp2(logb_s[g] + tail + c), 0.0)
                c = c + rsum_s[g]
            a_s[g] = a.astype(BF16)
            carry_ref[g] = c
            tops.append(jnp.where(n0 + g - i >= 1, jnp.max(c), NEG_INF))
        for g in range(SB_GROUP):
            av = _dot(a_s[g], v_ref[0, 0, 0, keys[g], :])
            acc_ref[g] = av if diagonal else acc_ref[g] + av
        return functools.reduce(jnp.maximum, tops)

    def qgroup(it, carry):
        n0 = it * SB_GROUP

        def cond(st):
            _, top = st
            return top > -SB_SKIP

        def body(st):
            i, _ = st
            return i + 1, sweep(n0, i)

        lax.while_loop(cond, body, (1, sweep(n0, 0)))
        for g in range(SB_GROUP):
            o_ref[0, 0, block_rows(n0 + g), :] = acc_ref[g].astype(BF16)
        return carry

    lax.fori_loop(0, nq // SB_GROUP, qgroup, 0)


def _stick_breaking_attention(q, kv):
    _, B, H, S, Dh = q.shape
    tq = _tile(SB_TQ, S)
    seq = lambda which: pl.BlockSpec((1, 1, 1, S, Dh), lambda b, h: (which, b, h, 0, 0))
    return pl.pallas_call(
        functools.partial(_sb_kernel, scale=Dh ** -0.5, tq=tq, nq=S // tq),
        grid=(B, H),
        in_specs=[pl.BlockSpec((1, 1, S, Dh), lambda b, h: (b, h, 0, 0)), seq(0), seq(1)],
        out_specs=pl.BlockSpec((1, 1, S, Dh), lambda b, h: (b, h, 0, 0)),
        out_shape=jax.ShapeDtypeStruct((B, H, S, Dh), BF16),
        scratch_shapes=[pltpu.VMEM((tq, tq), BF16), pltpu.VMEM((SB_GROUP, tq, Dh), F32)]
        + [pltpu.VMEM((SB_GROUP, tq, tq), F32)] * 4 + [pltpu.VMEM((SB_GROUP, tq, tq), BF16)] * 3,
        compiler_params=_cparams("parallel", "parallel"),
        name="stick_breaking_attention",
    )(q.reshape(B, H, S, Dh), kv, kv)


def _router_kernel(h_ref, rw_ref, rb_ref, e_ref, r_ref, w_ref, cnt_ref, tri_ref, carry_ref,
                   *, n_exp, tm, nsteps):
    i = pl.program_id(0)
    gsz = n_exp // N_GROUPS

    @pl.when(i == 0)
    def _():
        row = lax.broadcasted_iota(I32, (tm, tm), 0)
        col = lax.broadcasted_iota(I32, (tm, tm), 1)
        tri_ref[...] = jnp.where(row < col, 1.0, 0.0).astype(BF16)
        carry_ref[...] = jnp.zeros_like(carry_ref)

    lo, hi = _unpack_pair(_load_slab_words(h_ref, tm))
    half = lo.shape[1]
    logits = _dot_nt(rw_ref[:, :half], lo) + _dot_nt(rw_ref[:, half:], hi)
    scores = jax.nn.sigmoid(logits)
    sel = scores + rb_ref[...]

    sub = lax.broadcasted_iota(I32, (gsz, tm), 0)
    groups, gscore = [], []
    for g in range(N_GROUPS):
        sg = sel[g * gsz:(g + 1) * gsz, :]
        m1 = jnp.max(sg, axis=0, keepdims=True)
        i1 = jnp.min(jnp.where(sg == m1, sub, gsz), axis=0, keepdims=True)
        m2 = jnp.max(jnp.where(sub == i1, NEG_INF, sg), axis=0, keepdims=True)
        groups.append(sg)
        gscore.append(m1 + m2)
    masked = []
    for g in range(N_GROUPS):
        beat = jnp.zeros((1, tm), F32)
        for o in range(N_GROUPS):
            if o == g:
                continue
            beat += jnp.where(gscore[o] > gscore[g], 1.0, 0.0)
            if o < g:
                beat += jnp.where(gscore[o] == gscore[g], 1.0, 0.0)
        masked.append(jnp.where(beat < TOPK_GROUPS, groups[g], NEG_INF))
    x = jnp.concatenate(masked, axis=0)

    eid = lax.broadcasted_iota(I32, (n_exp, tm), 0)
    chosen = jnp.zeros((n_exp, tm), F32)
    picks, vals = [], []
    for _ in range(TOP_K):
        mx = jnp.max(x, axis=0, keepdims=True)
        idx = jnp.min(jnp.where(x == mx, eid, n_exp), axis=0, keepdims=True)
        hit = eid == idx
        picks.append(idx)
        vals.append(jnp.sum(jnp.where(hit, scores, 0.0), axis=0, keepdims=True))
        chosen += jnp.where(hit, 1.0, 0.0)
        x = jnp.where(hit, NEG_INF, x)
    total = vals[0]
    for v in vals[1:]:
        total = total + v

    rank_all = _dot(chosen.astype(BF16), tri_ref[...]) + carry_ref[...]
    for k in range(TOP_K):
        e_ref[k:k + 1, :] = picks[k]
        w_ref[k:k + 1, :] = vals[k] / total * ROUTED_SCALE
        rk = jnp.sum(jnp.where(eid == picks[k], rank_all, 0.0), axis=0, keepdims=True)
        r_ref[k:k + 1, :] = rk.astype(I32)
    carry_ref[...] += jnp.broadcast_to(jnp.sum(chosen, axis=1, keepdims=True), (n_exp, tm))

    @pl.when(i == nsteps - 1)
    def _():
        cnt_ref[...] = carry_ref[:, 0:LANES].astype(I32)


def _router(h32, rw_t, rb):
    E, D = rw_t.shape
    ch = D // 2 // LANES
    T = h32.shape[0] // ch
    tm = _tile(ROUTER_TM, T)
    nsteps = T // tm
    tok = pl.BlockSpec((TOP_K, tm), lambda i: (0, i))
    return pl.pallas_call(
        functools.partial(_router_kernel, n_exp=E, tm=tm, nsteps=nsteps),
        grid=(nsteps,),
        in_specs=[pl.BlockSpec((tm * ch, LANES), lambda i: (i, 0)),
                  pl.BlockSpec((E, D), lambda i: (0, 0)),
                  pl.BlockSpec((E, 1), lambda i: (0, 0))],
        out_specs=[tok, tok, tok, pl.BlockSpec((E, LANES), lambda i: (0, 0))],
        out_shape=[jax.ShapeDtypeStruct((TOP_K, T), I32), jax.ShapeDtypeStruct((TOP_K, T), I32),
                   jax.ShapeDtypeStruct((TOP_K, T), F32), jax.ShapeDtypeStruct((E, LANES), I32)],
        scratch_shapes=[pltpu.VMEM((tm, tm), BF16), pltpu.VMEM((E, tm), F32)],
        compiler_params=_cparams("arbitrary"),
        name="moe_router",
    )(h32, rw_t, rb)


def _dispatch_kernel(last_ref, slots_ref, h_ref, xs_hbm, zbuf, zsem, sem, *, n_exp, td, ch):
    i = pl.program_id(0)
    tile_rows = zbuf.shape[0]

    def zero_copy(e):
        dst = pl.ds(pl.multiple_of(last_ref[e], tile_rows), tile_rows)
        return pltpu.make_async_copy(zbuf, xs_hbm.at[dst], zsem)

    @pl.when(i == 0)
    def _():
        zbuf[...] = jnp.zeros_like(zbuf)

        def start(e, c):
            @pl.when(last_ref[e] >= 0)
            def _():
                zero_copy(e).start()
            return c

        def wait(e, c):
            @pl.when(last_ref[e] >= 0)
            def _():
                zero_copy(e).wait()
            return c

        lax.fori_loop(0, n_exp, start, 0)
        lax.fori_loop(0, n_exp, wait, 0)

    def row_copy(t, k):
        src = pl.ds(pl.multiple_of(t * ch, ch), ch)
        dst = pl.ds(pl.multiple_of(slots_ref[k, t], ch), ch)
        return pltpu.make_async_copy(h_ref.at[src], xs_hbm.at[dst], sem)

    def start_rows(t, c):
        for k in range(TOP_K):
            row_copy(t, k).start(priority=k % 2)
        return c

    def wait_rows(t, c):
        for k in range(TOP_K):
            row_copy(t, k).wait()
        return c

    lax.fori_loop(0, td, start_rows, 0)
    lax.fori_loop(0, td, wait_rows, 0)


def _dispatch(h32, slot_rows, last_rows, n_slots, tm_e, ch):
    T = slot_rows.shape[1]
    td = _tile(DISPATCH_TD, T)
    E = last_rows.shape[0]
    return pl.pallas_call(
        functools.partial(_dispatch_kernel, n_exp=E, td=td, ch=ch),
        grid_spec=pltpu.PrefetchScalarGridSpec(
            num_scalar_prefetch=1,
            grid=(T // td,),
            in_specs=[pl.BlockSpec((TOP_K, td), lambda i, last: (0, i), memory_space=pltpu.SMEM),
                      pl.BlockSpec((td * ch, LANES), lambda i, last: (i, 0))],
            out_specs=pl.BlockSpec(memory_space=pl.ANY),
            scratch_shapes=[pltpu.VMEM((tm_e * ch, LANES), U32), pltpu.SemaphoreType.DMA,
                            pltpu.SemaphoreType.DMA]),
        out_shape=jax.ShapeDtypeStruct((n_slots * ch, LANES), U32),
        compiler_params=_cparams("arbitrary"),
        name="moe_dispatch",
    )(last_rows, slot_rows, h32)


def _mlp_hidden(x_ref, wgu_ref, n_tok):
    f = wgu_ref.shape[1] // 2
    lo, hi = _unpack_pair(_load_slab_words(x_ref, n_tok))
    half = lo.shape[1]
    gu = _dot(lo, wgu_ref[:half, :]) + _dot(hi, wgu_ref[half:, :])
    return (jax.nn.silu(gu[:, :f]) * gu[:, f:]).astype(BF16)


def _expert_kernel(te_ref, nx_ref, nt_ref, x_ref, wg_hbm, wu_hbm, wd_hbm, y_ref,
                   wg_f, wu_f, wd_f, wgu_s, wd_s, sems, *, tm_e, layer):
    i = pl.program_id(0)
    f, d_model = wd_s.shape
    half = d_model // 2
    ch = half // LANES

    def weight_copies(e):
        return (pltpu.make_async_copy(wg_hbm.at[layer, e], wg_f, sems.at[0]),
                pltpu.make_async_copy(wu_hbm.at[layer, e], wu_f, sems.at[1]),
                pltpu.make_async_copy(wd_hbm.at[layer, e], wd_f, sems.at[2]))

    @pl.when(i < nt_ref[0])
    def _():
        expert = te_ref[i]

        @pl.when(i == 0)
        def _():
            for cp in weight_copies(expert):
                cp.start()

        @pl.when(jnp.logical_or(i == 0, expert != te_ref[jnp.maximum(i - 1, 0)]))
        def _():
            for cp in weight_copies(expert):
                cp.wait()
            wgu_s[:, :f] = wg_f[...].astype(BF16)
            wgu_s[:, f:] = wu_f[...].astype(BF16)
            for c in range(ch):
                lo_cols = slice(c * LANES, (c + 1) * LANES)
                hi_cols = slice(half + c * LANES, half + (c + 1) * LANES)
                wd_s[:, 2 * c * LANES:(2 * c + 1) * LANES] = wd_f[:, lo_cols].astype(BF16)
                wd_s[:, (2 * c + 1) * LANES:(2 * c + 2) * LANES] = wd_f[:, hi_cols].astype(BF16)

            @pl.when(nx_ref[i] >= 0)
            def _():
                for cp in weight_copies(nx_ref[i]):
                    cp.start()

        a = _mlp_hidden(x_ref, wgu_s, tm_e)
        for c in range(ch):
            yc = _dot(a, wd_s[:, 2 * c * LANES:(2 * c + 2) * LANES])
            y_ref[pl.ds(c, tm_e, stride=ch), :] = _pack_pair(yc[:, :LANES], yc[:, LANES:])


def _shared_kernel(x_ref, wgu_ref, wd_ref, y_ref, *, tm):
    y_ref[...] = _dot(_mlp_hidden(x_ref, wgu_ref, tm), wd_ref[...])


def _experts(xs, layer, w_gate, w_up, w_down, tile_expert, next_expert, n_tiles, tm_e, ch):
    _, E, D, F = w_gate.shape
    n_slots = xs.shape[0] // ch
    rows = pl.BlockSpec((tm_e * ch, LANES),
                        lambda i, te, nx, nt: (jnp.minimum(i, nt[0] - 1), 0))
    hbm = pl.BlockSpec(memory_space=pl.ANY)
    return pl.pallas_call(
        functools.partial(_expert_kernel, tm_e=tm_e, layer=layer),
        grid_spec=pltpu.PrefetchScalarGridSpec(
            num_scalar_prefetch=3,
            grid=(n_slots // tm_e,),
            in_specs=[rows, hbm, hbm, hbm],
            out_specs=rows,
            scratch_shapes=[pltpu.VMEM((D, F), F32), pltpu.VMEM((D, F), F32),
                            pltpu.VMEM((F, D), F32), pltpu.VMEM((D, 2 * F), BF16),
                            pltpu.VMEM((F, D), BF16), pltpu.SemaphoreType.DMA((3,))]),
        out_shape=jax.ShapeDtypeStruct(xs.shape, U32),
        compiler_params=_cparams("arbitrary"),
        name="moe_experts",
    )(tile_expert, next_expert, n_tiles, xs, w_gate, w_up, w_down)


def _shared_expert(h32, wgu, wd, ch):
    D, F2 = wgu.shape
    T = h32.shape[0] // ch
    tm = _tile(512, T)
    return pl.pallas_call(
        functools.partial(_shared_kernel, tm=tm),
        grid=(T // tm,),
        in_specs=[pl.BlockSpec((tm * ch, LANES), lambda i: (i, 0)),
                  pl.BlockSpec((D, F2), lambda i: (0, 0)),
                  pl.BlockSpec((F2 // 2, D), lambda i: (0, 0))],
        out_specs=pl.BlockSpec((tm, D), lambda i: (i, 0)),
        out_shape=jax.ShapeDtypeStruct((T, D), F32),
        compiler_params=_cparams("parallel"),
        name="moe_shared_expert",
    )(h32, wgu, wd)


def _combine_kernel(slots_ref, next_slots_ref, w_ref, ysh_ref, y_hbm, o_ref, bufs, gate_s, sems,
                    *, tc, ch, nsteps):
    i = pl.program_id(0)
    slot = i % 2

    def row_copy(table, which, t, k):
        src = pl.ds(pl.multiple_of(table[k, t], ch), ch)
        dst = pl.ds(pl.multiple_of(t * ch, ch), ch)
        return pltpu.make_async_copy(y_hbm.at[src], bufs.at[which, k, dst], sems.at[which])

    def gather_token(table, which, t):
        for k in range(TOP_K):
            row_copy(table, which, t, k).start(priority=k % 2)

    def wait_all(table, which):
        def wait_rows(t, c):
            for k in range(TOP_K):
                row_copy(table, which, t, k).wait()
            return c

        lax.fori_loop(0, tc, wait_rows, 0)

    @pl.when(i == 0)
    def _():
        def start_rows(t, c):
            gather_token(slots_ref, 0, t)
            return c

        lax.fori_loop(0, tc, start_rows, 0)

    wait_all(slots_ref, slot)

    half = ch * LANES
    w_cols = w_ref[...].T
    for k in range(TOP_K):
        gate_s[k] = jnp.broadcast_to(w_cols[:, k:k + 1], (tc, LANES))
    per_chunk = tc // ch
    for c in range(ch):
        lanes_lo = slice(c * LANES, (c + 1) * LANES)
        lanes_hi = slice(half + c * LANES, half + (c + 1) * LANES)
        acc_lo = ysh_ref[:, lanes_lo]
        acc_hi = ysh_ref[:, lanes_hi]
        for k in range(TOP_K):
            lo, hi = _unpack_pair_f32(bufs.at[slot, k][pl.ds(c, tc, stride=ch), :])
            acc_lo = acc_lo + gate_s[k] * lo
            acc_hi = acc_hi + gate_s[k] * hi
        o_ref[:, lanes_lo] = acc_lo
        o_ref[:, lanes_hi] = acc_hi
        for t in range(c * per_chunk, (c + 1) * per_chunk):
            gather_token(next_slots_ref, 1 - slot, t)

    @pl.when(i == nsteps - 1)
    def _():
        wait_all(next_slots_ref, 1 - slot)


def _combine(slot_rows, wts, ysh, ys, ch):
    T, D = ysh.shape
    tc = _tile(COMBINE_TC, T)
    nsteps = T // tc
    return pl.pallas_call(
        functools.partial(_combine_kernel, tc=tc, ch=ch, nsteps=nsteps),
        grid=(nsteps,),
        in_specs=[pl.BlockSpec((TOP_K, tc), lambda i: (0, i), memory_space=pltpu.SMEM),
                  pl.BlockSpec((TOP_K, tc), lambda i: (0, jnp.minimum(i + 1, nsteps - 1)),
                               memory_space=pltpu.SMEM),
                  pl.BlockSpec((TOP_K, tc), lambda i: (0, i)),
                  pl.BlockSpec((tc, D), lambda i: (i, 0)),
                  pl.BlockSpec(memory_space=pl.ANY)],
        out_specs=pl.BlockSpec((tc, D), lambda i: (i, 0)),
        out_shape=jax.ShapeDtypeStruct((T, D), F32),
        scratch_shapes=[pltpu.VMEM((2, TOP_K, tc * ch, LANES), U32),
                        pltpu.VMEM((TOP_K, tc, LANES), F32),
                        pltpu.SemaphoreType.DMA((2,))],
        compiler_params=_cparams("arbitrary"),
        name="moe_combine",
    )(slot_rows, slot_rows, wts, ysh, ys)


def _moe(h32, layer, rw_t, rb, w_gate, w_up, w_down, sh_gu, sh_d):
    E, D = rw_t.shape
    ch = D // 2 // LANES
    assert ch % 8 == 0, "a token slab must be whole (8,128) tiles"
    T = h32.shape[0] // ch
    tm_e = _tile(EXPERT_TM, T)
    eidx, rank, wts, counts = _router(h32, rw_t, rb)

    counts = counts[:, 0]
    tiles = (counts + tm_e - 1) // tm_e
    tile_end = jnp.cumsum(tiles)
    tile_start = tile_end - tiles
    n_tiles = tile_end[-1]
    experts = jnp.arange(E, dtype=I32)
    first_row = jnp.sum(jnp.where(eidx[:, :, None] == experts, tile_start * tm_e, 0), axis=-1)
    slot_rows = (first_row + rank) * ch
    max_tiles = T * TOP_K // tm_e + E
    tile_ids = jnp.minimum(jnp.arange(max_tiles, dtype=I32), n_tiles - 1)
    tile_expert = jnp.sum((tile_ids[:, None] >= tile_end[None, :]).astype(I32), axis=-1)
    group_end = jnp.sum(jnp.where(tile_expert[:, None] == experts, tile_end, 0), axis=-1)
    after = jnp.sum((group_end[:, None] >= tile_end[None, :]).astype(I32), axis=-1)
    next_expert = jnp.where(group_end < n_tiles, after, -1).astype(I32)
    last_rows = jnp.where(tiles > 0, (tile_end - 1) * (tm_e * ch), -1).astype(I32)

    xs = _dispatch(h32, slot_rows, last_rows, max_tiles * tm_e, tm_e, ch)
    ys = _experts(xs, layer, w_gate, w_up, w_down, tile_expert, next_expert,
                  n_tiles.reshape(1).astype(I32), tm_e, ch)
    ysh = _shared_expert(h32, sh_gu, sh_d, ch)
    return _combine(slot_rows, wts, ysh, ys, ch)


def kernel(x, c, positions, ada_w, ada_b, ln_g, ln_b, a_w_qkv, a_w_o, kv_ada_w, kv_ada_b, b_w_kv, b_w_q, b_w_o, router_w, router_b, w_gate, w_up, w_down, sh_gate, sh_up, sh_down):
    B, S, D = x.shape
    T = B * S
    assert D % HEAD_DIM == 0 and DEPTH == ada_w.shape[0] == 2
    xf = x.reshape(T, D)

    c8 = jnp.zeros((8, D), F32).at[:B].set(c)
    mods = _adaln(c8, ada_w, ada_b)[:, :B].reshape(DEPTH, B, 6, D)
    kv_mods = _adaln(c8, kv_ada_w[None], kv_ada_b[None])[0, :B].reshape(B, 2, D)
    sh_a, sc_a, g_a, sh_m, sc_m, g_m = (mods[:, :, r] for r in range(6))
    rope = _rope_tables(positions)

    def moe_weights(l):
        sh_gu = jnp.concatenate([sh_gate[l], sh_up[l]], axis=-1).astype(BF16)
        return (l, router_w[l].T.astype(BF16), router_b[l].reshape(-1, 1), w_gate, w_up, w_down,
                sh_gu, sh_down[l].astype(BF16))

    h = _modulate(xf, _mod_rows(None, [(sh_a[0], sc_a[0])]), S)
    qkv_views = _mm(h, a_w_qkv[0].astype(BF16), B=B, S=S, head_major_out=True, rope=rope,
                    n_rope_cols=2 * D, dilations=DILATIONS[1:])
    o = _dilated_attention(*qkv_views)
    y = _mm(o, a_w_o[0].astype(BF16), B=B, S=S)
    xf, h32 = _ln(xf, y, _mod_rows(g_a[0], [(sh_m[0], sc_m[0])]), ln_g[0, 0], ln_b[0, 0],
                  ["u32"], S)
    y = _moe(h32, *moe_weights(0))
    xf, h_kv, h_q = _ln(xf, y, _mod_rows(g_m[0], [(kv_mods[:, 0], kv_mods[:, 1]),
                                                  (sh_a[1], sc_a[1])]),
                        ln_g[0, 1], ln_b[0, 1], ["bf16", "bf16"], S)

    kv = _mm(h_kv, b_w_kv.astype(BF16), B=B, S=S, head_major_out=True)
    q = _mm(h_q, b_w_q[0].astype(BF16), B=B, S=S, head_major_out=True)
    o = _stick_breaking_attention(q, kv)
    y = _mm(o, b_w_o[0].astype(BF16), B=B, S=S)
    xf, h32 = _ln(xf, y, _mod_rows(g_a[1], [(sh_m[1], sc_m[1])]), ln_g[1, 0], ln_b[1, 0],
                  ["u32"], S)
    y = _moe(h32, *moe_weights(1))
    (xf,) = _ln(xf, y, _mod_rows(g_m[1], []), ln_g[1, 1], ln_b[1, 1], [], S)
    return xf.reshape(B, S, D)
```
